```python
import math, functools
import jax, jax.numpy as jnp
from jax import lax
import numpy as np

D_MODEL = 2048
BATCH = 8
SEQ = 4096
DEPTH = 1
DEC_BATCH = 32
DEC_SEQ = 64
PAST_LEN = 1024

CHUNK = 64
RMS_EPS = 1e-5
RWKV_HEAD = 64
RWKV_WIDTH = D_MODEL // 2
RWKV_HEADS = RWKV_WIDTH // RWKV_HEAD
W_LORA = 64
A_LORA = 64
G_LORA = 160
GN_EPS = 64e-5
S5_WIDTH = D_MODEL // 2
S5_GROUP = 16
S5_GROUPS = S5_WIDTH // S5_GROUP
S5_STATE = 64
DT_MIN = 1e-3
DT_MAX = 1e-1
SHIFT_W = 3 * RWKV_WIDTH + W_LORA + A_LORA + G_LORA
IN_W = SHIFT_W + S5_WIDTH + 2 * D_MODEL
N_EXPERTS = 32
TOP_K = 4
D_FF = D_MODEL
SWIGLU_LIMIT = 7.0
SWIGLU_ALPHA = 1.702
EXPERT_BLOCK = 256

kernel_name = 'rwkv7_s5_gated_hybrid_moe_stream_step'


def rms_norm(x, g):
    xf = x.astype(jnp.float32)
    y = xf * lax.rsqrt(jnp.mean(xf * xf, axis=-1, keepdims=True) + RMS_EPS)
    return (y * g.astype(jnp.float32)).astype(x.dtype)


def rwkv7_mix(p, p_last, s0, mu_shift, w0, w2, a0, a2, g2, k_k, k_a, r_k, lnx_g, lnx_b):
    f32 = jnp.float32
    Bn, T, _ = p.shape
    pf = p.astype(f32)
    p_prev = jnp.concatenate([p_last.astype(f32)[:, None], pf[:, :-1]], axis=1)
    ps = pf + (p_prev - pf) * mu_shift.astype(f32)
    cuts = [RWKV_WIDTH, 2 * RWKV_WIDTH, 3 * RWKV_WIDTH, 3 * RWKV_WIDTH + W_LORA, 3 * RWKV_WIDTH + W_LORA + A_LORA]
    r, k, v, xw, xa, xg = jnp.split(ps, cuts, axis=-1)
    w = -jax.nn.softplus(-(w0.astype(f32) + jnp.tanh(xw) @ w2.astype(f32))) - 0.5
    a = jax.nn.sigmoid(a0.astype(f32) + xa @ a2.astype(f32))
    g = jax.nn.sigmoid(xg) @ g2.astype(f32)
    heads = lambda t: t.reshape(Bn, T, RWKV_HEADS, RWKV_HEAD)
    kk = heads(k * k_k.astype(f32))
    kk = kk / jnp.maximum(jnp.sqrt(jnp.sum(kk * kk, axis=-1, keepdims=True)), 1e-12)
    k = k * (1.0 + (a - 1.0) * k_a.astype(f32))
    decay = jnp.exp(-jnp.exp(w))
    rh, kh, vh, dh = heads(r), heads(k), heads(v), heads(decay)
    bh = kk * heads(a)

    def step(S, inp):
        r_t, k_t, v_t, d_t, kk_t, b_t = inp
        sa = jnp.einsum('bhij,bhj->bhi', S, -kk_t)
        S = (S * d_t[:, :, None, :] + sa[..., :, None] * b_t[:, :, None, :]
             + v_t[..., :, None] * k_t[:, :, None, :])
        return S, jnp.einsum('bhij,bhj->bhi', S, r_t)

    tm = lambda t: jnp.moveaxis(t, 1, 0)
    s_last, o = lax.scan(step, s0.astype(f32), (tm(rh), tm(kh), tm(vh), tm(dh), tm(kk), tm(bh)))
    o = jnp.moveaxis(o, 0, 1)
    mean = jnp.mean(o, axis=-1, keepdims=True)
    var = jnp.mean(jnp.square(o - mean), axis=-1, keepdims=True)
    o = ((o - mean) * lax.rsqrt(var + GN_EPS)).reshape(Bn, T, RWKV_WIDTH)
    o = o * lnx_g.astype(f32) + lnx_b.astype(f32)
    bonus = jnp.sum(rh * kh * r_k.astype(f32), axis=-1, keepdims=True) * vh
    o = (o + bonus.reshape(Bn, T, RWKV_WIDTH)) * g
    return o.astype(p.dtype), s_last


def s5_mix(u, x0_re, x0_im, lam_re, lam_im, log_dt, b_re, b_im, c_re, c_im, d_skip, w_glu, b_glu):
    f32 = jnp.float32
    Bn, T, _ = u.shape
    uf = u.astype(f32)
    lam = lax.complex(lam_re.astype(f32), lam_im.astype(f32))
    dt = jnp.exp(log_dt.astype(f32))[:, None]
    lam_bar = jnp.exp(lam * dt)
    b_bar = ((lam_bar - 1.0) / lam)[..., None] * lax.complex(b_re.astype(f32), b_im.astype(f32))
    c = lax.complex(c_re.astype(f32), c_im.astype(f32))
    L = min(CHUNK, T)
    n_chunks = T // L
    uc = jnp.moveaxis(uf.reshape(Bn, n_chunks, L, S5_GROUPS, S5_GROUP), 1, 0)

    def combine(e1, e2):
        a1, b1 = e1
        a2, b2 = e2
        return a1 * a2, a2 * b1 + b2

    def chunk_step(x, u_c):
        bu = jnp.einsum('blgp,gsp->blgs', u_c.astype(jnp.complex64), b_bar)
        a_el = jnp.broadcast_to(lam_bar, bu.shape)
        a_cum, b_cum = lax.associative_scan(combine, (a_el, bu), axis=1)
        xs = b_cum + a_cum * x[:, None]
        y = jnp.real(jnp.einsum('gps,blgs->blgp', c, xs))
        return xs[:, -1], y

    x0 = lax.complex(x0_re.astype(f32), x0_im.astype(f32))
    x_last, ys = lax.scan(chunk_step, x0, uc)
    y = jnp.moveaxis(ys, 0, 1).reshape(Bn, T, S5_WIDTH) + d_skip.astype(f32) * uf
    y = jax.nn.gelu(y)
    out = y * jax.nn.sigmoid(y @ w_glu.astype(f32) + b_glu.astype(f32))
    return out.astype(u.dtype), jnp.real(x_last), jnp.imag(x_last)


def moe_ffn(h, w_router, b_router, w_gate_up, b_gate_up, w_down, b_down):
    f32 = jnp.float32
    T = h.shape[0]
    logits = (h @ w_router).astype(f32) + b_router.astype(f32)
    top_val, top_idx = lax.top_k(logits, TOP_K)
    gates = jax.nn.softmax(top_val, axis=-1)
    TK = T * TOP_K
    flat_e = top_idx.reshape(TK).astype(jnp.int32)
    flat_tok = jnp.arange(TK, dtype=jnp.int32) // TOP_K
    flat_gate = gates.reshape(TK)
    order = jnp.argsort(flat_e)
    sorted_e = flat_e[order]
    counts = jnp.bincount(flat_e, length=N_EXPERTS).astype(jnp.int32)
    padded = (counts + EXPERT_BLOCK - 1) // EXPERT_BLOCK * EXPERT_BLOCK
    start = jnp.cumsum(counts) - counts
    pad_end = jnp.cumsum(padded)
    pad_start = pad_end - padded
    dest = pad_start[sorted_e] + jnp.arange(TK, dtype=jnp.int32) - start[sorted_e]
    n_blocks = -(-(TK + N_EXPERTS * (EXPERT_BLOCK - 1)) // EXPERT_BLOCK)
    n_rows = n_blocks * EXPERT_BLOCK
    row_tok = jnp.full((n_rows,), T, jnp.int32).at[dest].set(flat_tok[order])
    row_gate = jnp.zeros((n_rows,), f32).at[dest].set(flat_gate[order])
    block_start = jnp.arange(n_blocks, dtype=jnp.int32) * EXPERT_BLOCK
    block_expert = jnp.minimum(jnp.sum(block_start[:, None] >= pad_end[None, :], axis=1), N_EXPERTS - 1)
    h_pad = jnp.concatenate([h, jnp.zeros((1, h.shape[1]), h.dtype)], axis=0)

    def expert_block(args):
        rows, e = args
        gu = h_pad[rows] @ w_gate_up[e] + b_gate_up[e]
        glu, lin = gu[:, :D_FF], gu[:, D_FF:]
        glu = jnp.minimum(glu, SWIGLU_LIMIT)
        lin = jnp.clip(lin, -SWIGLU_LIMIT, SWIGLU_LIMIT)
        act = glu * jax.nn.sigmoid(SWIGLU_ALPHA * glu) * (lin + 1.0)
        return act @ w_down[e] + b_down[e]

    y_rows = lax.map(expert_block, (row_tok.reshape(n_blocks, EXPERT_BLOCK), block_expert))
    y_rows = y_rows.reshape(n_rows, -1) * row_gate[:, None].astype(y_rows.dtype)
    return jnp.zeros((T + 1, h.shape[1]), h.dtype).at[row_tok].add(y_rows.astype(h.dtype))[:T]


def trunk_layer(x, state, norm_mix_g, w_in, mu_shift, w0, w2, a0, a2, g2, k_k, k_a, r_k, lnx_g, lnx_b,
                lam_re, lam_im, log_dt, b_re, b_im, c_re, c_im, d_skip, w_glu, b_glu,
                w_up_a, w_up_b, w_out, norm_ffn_g, w_router, b_router, w_gate_up, b_gate_up, w_down, b_down):
    Bn, T, _ = x.shape
    if state is None:
        s_rwkv0 = jnp.zeros((Bn, RWKV_HEADS, RWKV_HEAD, RWKV_HEAD), jnp.float32)
        s5_re0 = jnp.zeros((Bn, S5_GROUPS, S5_STATE), jnp.float32)
        s5_im0 = jnp.zeros((Bn, S5_GROUPS, S5_STATE), jnp.float32)
        shift0 = jnp.zeros((Bn, SHIFT_W), x.dtype)
    else:
        s_rwkv0, s5_re0, s5_im0, shift0 = state
    h = rms_norm(x, norm_mix_g)
    proj = h @ w_in
    p_rwkv = proj[..., :SHIFT_W]
    u = proj[..., SHIFT_W:SHIFT_W + S5_WIDTH]
    gate_a = proj[..., SHIFT_W + S5_WIDTH:SHIFT_W + S5_WIDTH + D_MODEL]
    gate_b = proj[..., SHIFT_W + S5_WIDTH + D_MODEL:]
    o_a, s_rwkv = rwkv7_mix(p_rwkv, shift0, s_rwkv0, mu_shift, w0, w2, a0, a2, g2, k_k, k_a, r_k, lnx_g, lnx_b)
    o_b, s5_re, s5_im = s5_mix(u, s5_re0, s5_im0, lam_re, lam_im, log_dt, b_re, b_im, c_re, c_im, d_skip, w_glu, b_glu)
    merged = jax.nn.sigmoid(gate_a) * (o_a @ w_up_a) + jax.nn.sigmoid(gate_b) * (o_b @ w_up_b)
    x = x + (merged @ w_out).astype(x.dtype)
    h2 = rms_norm(x, norm_ffn_g)
    ff = moe_ffn(h2.reshape(Bn * T, D_MODEL), w_router, b_router, w_gate_up, b_gate_up, w_down, b_down)
    x = x + ff.reshape(Bn, T, D_MODEL).astype(x.dtype)
    return x, (s_rwkv, s5_re, s5_im, p_rwkv[:, -1])


def setup_inputs(seed: int = 0) -> dict:
    key = jax.random.key(seed)
    ks = iter(jax.random.split(key, 64))
    f32 = jnp.float32
    nrm = lambda shape, scale: jax.random.normal(next(ks), shape, f32) * scale
    uni = lambda shape, lo, hi: jax.random.uniform(next(ks), shape, f32, lo, hi)
    Ld = DEPTH
    ch = jnp.arange(RWKV_WIDTH, dtype=f32) / (RWKV_WIDTH - 1)
    n_idx = jnp.arange(S5_STATE, dtype=f32)
    return {
        'x_prompt': nrm((BATCH, SEQ, D_MODEL), 1.0),
        'x_sample': nrm((DEC_BATCH, DEC_SEQ, D_MODEL), 1.0),
        'state_rwkv': nrm((Ld, DEC_BATCH, RWKV_HEADS, RWKV_HEAD, RWKV_HEAD), 0.5),
        'state_s5_re': nrm((Ld, DEC_BATCH, S5_GROUPS, S5_STATE), 0.1),
        'state_s5_im': nrm((Ld, DEC_BATCH, S5_GROUPS, S5_STATE), 0.1),
        'state_shift': nrm((Ld, DEC_BATCH, SHIFT_W), 1.0),
        'norm_mix_g': 1.0 + nrm((Ld, D_MODEL), 0.02),
        'w_in': nrm((Ld, D_MODEL, IN_W), D_MODEL ** -0.5),
        'mu_shift': uni((Ld, SHIFT_W), 0.0, 1.0),
        'w0': (-6.0 + 5.0 * ch ** 0.85)[None] + nrm((Ld, RWKV_WIDTH), 0.1),
        'w2': nrm((Ld, W_LORA, RWKV_WIDTH), 0.1),
        'a0': nrm((Ld, RWKV_WIDTH), 0.1),
        'a2': nrm((Ld, A_LORA, RWKV_WIDTH), 0.1),
        'g2': nrm((Ld, G_LORA, RWKV_WIDTH), G_LORA ** -0.5),
        'k_k': 0.85 + nrm((Ld, RWKV_WIDTH), 0.02),
        'k_a': 1.0 + nrm((Ld, RWKV_WIDTH), 0.02),
        'r_k': nrm((Ld, RWKV_HEADS, RWKV_HEAD), 0.1),
        'lnx_g': 1.0 + nrm((Ld, RWKV_WIDTH), 0.02),
        'lnx_b': nrm((Ld, RWKV_WIDTH), 0.01),
        'lam_re': -0.5 + nrm((Ld, S5_GROUPS, S5_STATE), 0.01),
        'lam_im': math.pi * n_idx + nrm((Ld, S5_GROUPS, S5_STATE), 0.01),
        'log_dt': uni((Ld, S5_GROUPS), math.log(DT_MIN), math.log(DT_MAX)),
        'b_re': nrm((Ld, S5_GROUPS, S5_STATE, S5_GROUP), (2.0 * S5_GROUP) ** -0.5),
        'b_im': nrm((Ld, S5_GROUPS, S5_STATE, S5_GROUP), (2.0 * S5_GROUP) ** -0.5),
        'c_re': nrm((Ld, S5_GROUPS, S5_GROUP, S5_STATE), S5_STATE ** -0.5),
        'c_im': nrm((Ld, S5_GROUPS, S5_GROUP, S5_STATE), S5_STATE ** -0.5),
        'd_skip': nrm((Ld, S5_WIDTH), 1.0),
        'w_glu': nrm((Ld, S5_WIDTH, S5_WIDTH), S5_WIDTH ** -0.5),
        'b_glu': nrm((Ld, S5_WIDTH), 0.01),
        'w_up_a': nrm((Ld, RWKV_WIDTH, D_MODEL), RWKV_WIDTH ** -0.5),
        'w_up_b': nrm((Ld, S5_WIDTH, D_MODEL), S5_WIDTH ** -0.5),
        'w_out': nrm((Ld, D_MODEL, D_MODEL), D_MODEL ** -0.5),
        'norm_ffn_g': 1.0 + nrm((Ld, D_MODEL), 0.02),
        'w_router': nrm((Ld, D_MODEL, N_EXPERTS), D_MODEL ** -0.5),
        'b_router': nrm((Ld, N_EXPERTS), 0.01),
        'w_gate_up': nrm((Ld, N_EXPERTS, D_MODEL, 2 * D_FF), D_MODEL ** -0.5),
        'b_gate_up': nrm((Ld, N_EXPERTS, 2 * D_FF), 0.01),
        'w_down': nrm((Ld, N_EXPERTS, D_FF, D_MODEL), D_FF ** -0.5),
        'b_down': nrm((Ld, N_EXPERTS, D_MODEL), 0.01),
        'norm_final_g': 1.0 + nrm((D_MODEL,), 0.02),
    }


def reference(x_prompt, x_sample, state_rwkv, state_s5_re, state_s5_im, state_shift,
              norm_mix_g, w_in, mu_shift, w0, w2, a0, a2, g2, k_k, k_a, r_k, lnx_g, lnx_b,
              lam_re, lam_im, log_dt, b_re, b_im, c_re, c_im, d_skip, w_glu, b_glu,
              w_up_a, w_up_b, w_out, norm_ffn_g, w_router, b_router, w_gate_up, b_gate_up,
              w_down, b_down, norm_final_g):
    per_layer = (norm_mix_g, w_in, mu_shift, w0, w2, a0, a2, g2, k_k, k_a, r_k, lnx_g, lnx_b,
                 lam_re, lam_im, log_dt, b_re, b_im, c_re, c_im, d_skip, w_glu, b_glu,
                 w_up_a, w_up_b, w_out, norm_ffn_g, w_router, b_router, w_gate_up, b_gate_up,
                 w_down, b_down)
    xp, xs = x_prompt, x_sample
    new_p = [[], [], [], []]
    new_s = [[], [], [], []]
    for l in range(DEPTH):
        lw = [w[l] for w in per_layer]
        xp, sp = trunk_layer(xp, None, *lw)
        xs, ss = trunk_layer(xs, (state_rwkv[l], state_s5_re[l], state_s5_im[l], state_shift[l]), *lw)
        for i in range(4):
            new_p[i].append(sp[i])
            new_s[i].append(ss[i])
    y_prompt = rms_norm(xp, norm_final_g)
    y_sample = rms_norm(xs, norm_final_g)
    rwkv_p, s5_re_p, s5_im_p, shift_p = [jnp.stack(v, axis=0) for v in new_p]
    rwkv_s, s5_re_s, s5_im_s, shift_s = [jnp.stack(v, axis=0) for v in new_s]
    return (y_prompt, y_sample, rwkv_p, s5_re_p, s5_im_p, shift_p, rwkv_s, s5_re_s, s5_im_s, shift_s)
```

```python
import functools
import math

import jax
import jax.numpy as jnp
from jax import lax
from jax.experimental import pallas as pl
from jax.experimental.pallas import tpu as pltpu

f32 = jnp.float32
bf16 = jnp.bfloat16
i32 = jnp.int32
u32 = jnp.uint32

V7X_LANES = 128
V7X_SUBLANES = 8
V7X_VMEM_BYTES = 64 * 1024 * 1024
VMEM_LIMIT = 56 * 1024 * 1024

RMS_EPS = 1e-5
GN_EPS = 64e-5
HEAD = 64
HEAD_SHIFT = 6
HEADS_PER_GROUP = 4
GROUP = HEAD * HEADS_PER_GROUP
CHUNK = 64
W_LORA, A_LORA, G_LORA = 64, 64, 160
LORA_PAD = 512
TOP_K = 4
SWIGLU_LIMIT = 7.0
SWIGLU_ALPHA = 1.702


def _cparams(sem):
    return pltpu.CompilerParams(dimension_semantics=sem, vmem_limit_bytes=VMEM_LIMIT)


def _dot(a, b):
    return jnp.dot(a.astype(bf16), b.astype(bf16), preferred_element_type=f32)


def _dot_nt(a, b):
    return lax.dot_general(a.astype(bf16), b.astype(bf16), (((1,), (1,)), ((), ())),
                           preferred_element_type=f32)


def _dot_tn(a, b):
    return lax.dot_general(a.astype(bf16), b.astype(bf16), (((0,), (0,)), ((), ())),
                           preferred_element_type=f32)


def _split(x):
    hi = x.astype(bf16)
    lo = (x - hi.astype(f32)).astype(bf16)
    return hi, lo


def _dot_exact_rhs(a, b_exact):
    hi, lo = _split(a)
    return (jnp.dot(hi, b_exact, preferred_element_type=f32)
            + jnp.dot(lo, b_exact, preferred_element_type=f32))


def _dot_exact_lhs(a_exact, b):
    hi, lo = _split(b)
    return (jnp.dot(a_exact, hi, preferred_element_type=f32)
            + jnp.dot(a_exact, lo, preferred_element_type=f32))


def _norm_proj_body(x_ref, g_ref, w_ref, o_ref, h_ref):
    @pl.when(pl.program_id(2) == 0)
    def _():
        x = x_ref[...]
        ms = jnp.mean(x * x, axis=-1, keepdims=True)
        h_ref[...] = (x * lax.rsqrt(ms + RMS_EPS) * g_ref[...]).astype(bf16)

    o_ref[...] = jnp.dot(h_ref[...], w_ref[...], preferred_element_type=f32)


def _norm_proj(x, gain, w, tm, tn, time_major_out):
    B, T, D = x.shape
    N = w.shape[1]
    assert T % tm == 0 and N % tn == 0
    if time_major_out:
        out_shape = jax.ShapeDtypeStruct((T, B * N), f32)
        out_spec = pl.BlockSpec((tm, tn), lambda b, i, n: (i, b * (N // tn) + n))
    else:
        out_shape = jax.ShapeDtypeStruct((B, T, N), f32)
        out_spec = pl.BlockSpec((None, tm, tn), lambda b, i, n: (b, i, n))
    out = pl.pallas_call(
        _norm_proj_body,
        grid=(B, T // tm, N // tn),
        in_specs=[pl.BlockSpec((None, tm, D), lambda b, i, n: (b, i, 0)),
                  pl.BlockSpec((1, D), lambda b, i, n: (0, 0)),
                  pl.BlockSpec((D, tn), lambda b, i, n: (0, n))],
        out_specs=out_spec,
        out_shape=out_shape,
        scratch_shapes=[pltpu.VMEM((tm, D), bf16)],
        compiler_params=_cparams(("parallel", "parallel", "arbitrary")),
        name="norm_proj",
    )(x, gain.reshape(1, D), w)
    return out.reshape(T, B, N) if time_major_out else out


PV_MU_R, PV_MU_K, PV_MU_V, PV_W0, PV_A0, PV_KK, PV_KA, PV_RK, PV_LNG, PV_LNB = range(10)
PV_ROWS = 16


def _rwkv_body(nc, r_ref, k_ref, v_ref, l_ref, pr_ref, pk_ref, pv_ref, plo_ref,
               lr_ref, lk_ref, lv_ref, llo_ref, pvec_ref, mul_ref, wl_ref, s0_ref,
               o_ref, so_ref, s_ref):
    c = pl.program_id(1)
    C = r_ref.shape[0]
    W = r_ref.shape[1]
    n_groups = W // GROUP

    @pl.when(c == 0)
    def _():
        s_ref[...] = s0_ref[...]

    row = lax.broadcasted_iota(i32, (C, 1), 0)

    def shifted(x_ref, prev8_ref, last_ref, mu):
        x = x_ref[...]
        prev_row = jnp.where(c > 0, prev8_ref[V7X_SUBLANES - 1:V7X_SUBLANES, :], last_ref[...])
        prev = jnp.where(row == 0, prev_row, pltpu.roll(x, shift=1, axis=0))
        return x + (prev - x) * mu

    pvec = pvec_ref[...]
    prm = lambda i: pvec[i:i + 1, :]
    r = shifted(r_ref, pr_ref, lr_ref, prm(PV_MU_R))
    k = shifted(k_ref, pk_ref, lk_ref, prm(PV_MU_K))
    v = shifted(v_ref, pv_ref, lv_ref, prm(PV_MU_V))
    xl = shifted(l_ref, plo_ref, llo_ref, mul_ref[...])

    lane_l = lax.broadcasted_iota(i32, (1, xl.shape[1]), 1)
    act = jnp.where(lane_l < W_LORA, jnp.tanh(xl),
                    jnp.where(lane_l < W_LORA + A_LORA, xl, jax.nn.sigmoid(xl)))
    lo = jnp.dot(act.astype(bf16), wl_ref[...], preferred_element_type=f32)
    z = -(prm(PV_W0) + lo[:, :W])
    w = -(jnp.maximum(z, 0.0) + jnp.log1p(jnp.exp(-jnp.abs(z)))) - 0.5
    logd = -jnp.exp(w)
    a = jax.nn.sigmoid(prm(PV_A0) + lo[:, W:2 * W])
    g = lo[:, 2 * W:]

    gi = lax.broadcasted_iota(i32, (GROUP, GROUP), 0)
    gj = lax.broadcasted_iota(i32, (GROUP, GROUP), 1)
    same_head = (gi >> HEAD_SHIFT) == (gj >> HEAD_SHIFT)
    head_ones = jnp.where(same_head, 1.0, 0.0).astype(bf16)
    ti, tj = gi & (HEAD - 1), gj & (HEAD - 1)
    strict = same_head & (ti > tj)
    incl = same_head & (ti >= tj)
    ci = lax.broadcasted_iota(i32, (C, C), 0)
    cj = lax.broadcasted_iota(i32, (C, C), 1)
    tri = jnp.where(ci >= cj, 1.0, 0.0).astype(bf16)
    lane_head = lax.broadcasted_iota(i32, (1, GROUP), 1) >> HEAD_SHIFT
    eye_quad = jnp.where(lax.broadcasted_iota(i32, (C, GROUP), 0)
                         == (lax.broadcasted_iota(i32, (C, GROUP), 1) & (HEAD - 1)), 1.0, 0.0)

    def head_sum(x, exact):
        parts = []
        for gq in range(n_groups):
            xg = x[:, gq * GROUP:(gq + 1) * GROUP]
            parts.append(_dot_exact_rhs(xg, head_ones) if exact
                         else jnp.dot(xg.astype(bf16), head_ones, preferred_element_type=f32))
        return jnp.concatenate(parts, axis=1)

    kk = k * prm(PV_KK)
    kk = kk / jnp.maximum(jnp.sqrt(head_sum(kk * kk, True)), 1e-12)
    k2 = k * (1.0 + (a - 1.0) * prm(PV_KA))
    cl = _dot_exact_lhs(tri, logd)
    p_incl = jnp.exp(cl)
    p_excl = jnp.exp(cl - logd)
    p_inv = jnp.exp(-cl)
    rt = r * p_incl
    at = -kk * p_excl
    bt = kk * a * p_inv
    kt = k2 * p_inv
    p_end = p_incl[C - 1:C, :]
    bonus = head_sum(r * k2 * prm(PV_RK), False) * v

    def stack(x):
        return jnp.concatenate([jnp.where(lane_head == h, x, 0.0) for h in range(HEADS_PER_GROUP)], axis=0)

    def tile(x):
        return jnp.concatenate([x] * HEADS_PER_GROUP, axis=0)

    def fold(x):
        out = x[0:C]
        for h in range(1, HEADS_PER_GROUP):
            out = out + x[h * C:(h + 1) * C]
        return out

    def block(x):
        return jnp.where(same_head, tile(x), 0.0)

    outs = []
    for gq in range(n_groups):
        sl = slice(gq * GROUP, (gq + 1) * GROUP)
        ag, rg, bg, kg, vg = at[:, sl], rt[:, sl], bt[:, sl], kt[:, sl], v[:, sl]
        sa, sr, sv = stack(ag), stack(rg), stack(vg)
        tb, tk = tile(bg), tile(kg)
        ab = jnp.where(strict, _dot_nt(sa, tb), 0.0)
        ak = jnp.where(strict, _dot_nt(sa, tk), 0.0)
        rb = jnp.where(incl, _dot_nt(sr, tb), 0.0)
        rk = jnp.where(incl, _dot_nt(sr, tk), 0.0)
        q4, qb = fold(ab), ab
        t4 = eye_quad + q4
        n_sq = max(1, (C - 1).bit_length() - 1)
        for _ in range(n_sq):
            q4 = _dot(q4, qb)
            qb = block(q4)
            t4 = t4 + _dot(t4, qb)
        w4 = _dot(t4, sa)
        akv = _dot(fold(ak), sv)
        uv = _dot(t4, stack(akv))
        orv = _dot(fold(rk), sv)
        s_g = s_ref[gq]
        u4 = _dot_nt(w4, s_g) + uv
        o4 = _dot_nt(rg, s_g) + _dot(fold(rb), stack(u4)) + orv
        upd = _dot_tn(jnp.concatenate([u4, vg], axis=0), jnp.concatenate([bg, kg], axis=0))
        s_new = (s_g + jnp.where(same_head, upd, 0.0)) * p_end[:, sl]
        s_ref[gq] = s_new
        outs.append(o4)
    o = jnp.concatenate(outs, axis=1)

    inv_n = 1.0 / HEAD
    mean = head_sum(o, False) * inv_n
    d = o - mean
    var = head_sum(d * d, False) * inv_n
    on = d * lax.rsqrt(var + GN_EPS)
    o_ref[...] = (on * prm(PV_LNG) + prm(PV_LNB) + bonus) * g

    @pl.when(c == nc - 1)
    def _():
        so_ref[...] = s_ref[...]


def _rwkv(proj, col_r, col_lora, last_rkv, last_lora, pvec, mu_lora, w_lora, s0_blocks):
    B, T, _ = proj.shape
    W = pvec.shape[1]
    C = CHUNK
    assert T % C == 0 and C == HEAD and HEAD == 1 << HEAD_SHIFT
    n_groups = W // GROUP
    rb = C // V7X_SUBLANES

    def cur(col, width):
        return pl.BlockSpec((None, C, width), lambda b, c, col=col: (b, c, col))

    def prev(col, width):
        return pl.BlockSpec((None, V7X_SUBLANES, width),
                            lambda b, c, col=col: (b, jnp.maximum(c * rb - 1, 0), col))

    def last(col, width):
        return pl.BlockSpec((None, 1, width), lambda b, c, col=col: (b, 0, col))

    const = lambda shape: pl.BlockSpec(shape, lambda b, c: (0,) * len(shape))
    state_spec = pl.BlockSpec((None, n_groups, GROUP, GROUP), lambda b, c: (b, 0, 0, 0))
    return pl.pallas_call(
        functools.partial(_rwkv_body, T // C),
        grid=(B, T // C),
        in_specs=[cur(col_r, W), cur(col_r + 1, W), cur(col_r + 2, W), cur(col_lora, LORA_PAD),
                  prev(col_r, W), prev(col_r + 1, W), prev(col_r + 2, W), prev(col_lora, LORA_PAD),
                  last(0, W), last(1, W), last(2, W), last(0, LORA_PAD),
                  const((PV_ROWS, W)), const((1, LORA_PAD)), const((LORA_PAD, 3 * W)), state_spec],
        out_specs=[pl.BlockSpec((None, C, W), lambda b, c: (b, c, 0)), state_spec],
        out_shape=[jax.ShapeDtypeStruct((B, T, W), f32),
                   jax.ShapeDtypeStruct((B, n_groups, GROUP, GROUP), f32)],
        scratch_shapes=[pltpu.VMEM((n_groups, GROUP, GROUP), f32)],
        compiler_params=_cparams(("parallel", "arbitrary")),
        name="rwkv7_chunked",
    )(proj, proj, proj, proj, proj, proj, proj, proj,
      last_rkv, last_rkv, last_rkv, last_lora, pvec, mu_lora, w_lora, s0_blocks)


COL_GA, COL_GB = 0, 1
COL_R = 4
COL_LORA = 14
MAIN_COLS = 7680


def _rwkv_weights(mu_shift, w0, w2, a0, a2, g2, k_k, k_a, r_k, lnx_g, lnx_b):
    W = w0.shape[0]
    rows = [mu_shift[:W], mu_shift[W:2 * W], mu_shift[2 * W:3 * W], w0, a0, k_k, k_a,
            r_k.reshape(W), lnx_g, lnx_b]
    pvec = jnp.zeros((PV_ROWS, W), f32).at[:len(rows)].set(jnp.stack(rows).astype(f32))
    n_lora = W_LORA + A_LORA + G_LORA
    mu_lora = jnp.zeros((1, LORA_PAD), f32).at[0, :n_lora].set(mu_shift[3 * W:])
    w_lora = jnp.zeros((LORA_PAD, 3 * W), f32)
    w_lora = w_lora.at[:W_LORA, :W].set(w2)
    w_lora = w_lora.at[W_LORA:W_LORA + A_LORA, W:2 * W].set(a2)
    w_lora = w_lora.at[W_LORA + A_LORA:n_lora, 2 * W:].set(g2)
    return pvec, mu_lora, w_lora.astype(bf16)


def _rwkv_stage(proj, shift0, s0, rw):
    pvec, mu_lora, w_lora = rw
    B = proj.shape[0]
    W = pvec.shape[1]
    H = W // HEAD
    n_groups = H // HEADS_PER_GROUP
    last_rkv = shift0[:, None, :3 * W].astype(f32)
    last_lora = jnp.zeros((B, 1, LORA_PAD), f32).at[:, 0, :shift0.shape[1] - 3 * W].set(shift0[:, 3 * W:])
    eye = jnp.eye(HEADS_PER_GROUP, dtype=f32)
    s0_blocks = jnp.einsum('bghij,hk->bghikj', s0.reshape(B, n_groups, HEADS_PER_GROUP, HEAD, HEAD), eye)
    s0_blocks = s0_blocks.reshape(B, n_groups, GROUP, GROUP)
    o, s_blocks = _rwkv(proj, COL_R, COL_LORA, last_rkv, last_lora, pvec, mu_lora, w_lora, s0_blocks)
    s5 = s_blocks.reshape(B, n_groups, HEADS_PER_GROUP, HEAD, HEADS_PER_GROUP, HEAD)
    s_new = jnp.einsum('bghikj,hk->bghij', s5, eye).reshape(B, H, HEAD, HEAD)
    return o, s_new


S5_BLOCK_IN = 128
S5_BLOCK_STATE = 512


def _s5_prep_body(lr_ref, li_ref, ldt_ref, br_ref, bi_ref, lbr_ref, lbi_ref, bbr_ref, bbi_ref):
    lr, li = lr_ref[...], li_ref[...]
    dt = jnp.exp(ldt_ref[...])
    mag = jnp.exp(lr * dt)
    er, ei = mag * jnp.cos(li * dt), mag * jnp.sin(li * dt)
    lbr_ref[...] = er
    lbi_ref[...] = ei
    den = lr * lr + li * li
    cr = ((er - 1.0) * lr + ei * li) / den
    ci = (ei * lr - (er - 1.0) * li) / den
    br, bi = br_ref[...], bi_ref[...]
    bbr_ref[...] = cr * br - ci * bi
    bbi_ref[...] = cr * bi + ci * br


def _s5_weights(lam_re, lam_im, log_dt, b_re, b_im, c_re, c_im, d_skip, w_glu, b_glu):
    G, S = lam_re.shape
    P = b_re.shape[-1]
    n = G * S
    col = lambda t: t.reshape(n, 1).astype(f32)
    ldt = jnp.broadcast_to(log_dt[:, None], (G, S))
    shapes = [jax.ShapeDtypeStruct((n, 1), f32)] * 2 + [jax.ShapeDtypeStruct((n, P), f32)] * 2
    lbr, lbi, bbr, bbi = pl.pallas_call(_s5_prep_body, out_shape=shapes, name="s5_discretise")(
        col(lam_re), col(lam_im), col(ldt), b_re.reshape(n, P).astype(f32), b_im.reshape(n, P).astype(f32))
    gpb = S5_BLOCK_IN // P
    nb = G // gpb
    eye = jnp.eye(gpb, dtype=f32)

    def in_block(bb):
        t = bb.reshape(nb, gpb, S, P)
        return jnp.einsum('qgsp,gh->qgphs', t, eye).reshape(nb, gpb * P, gpb * S)

    def out_block(cc):
        t = cc.astype(f32).reshape(nb, gpb, P, S)
        return jnp.einsum('qgps,gh->qgshp', t, eye).reshape(nb, gpb * S, gpb * P)

    wb = jnp.concatenate([in_block(bbr), in_block(bbi)], axis=2).astype(bf16)
    wc = jnp.concatenate([out_block(c_re), -out_block(c_im)], axis=1).astype(bf16)
    W = G * P
    return (lbr.reshape(1, n), lbi.reshape(1, n), wb, wc, d_skip.reshape(1, W).astype(f32),
            w_glu.astype(bf16), b_glu.reshape(1, W).astype(f32))


def _s5_body(n_steps, u_ref, x0r_ref, x0i_ref, lbr_ref, lbi_ref, wb_ref, wc_ref, d_ref, wg_ref, bg_ref,
             o_ref, xr_out, xi_out, bur_ref, bui_ref, xr_ref, xi_ref, y_ref):
    i = pl.program_id(1)
    Tc, Bs, W = u_ref.shape
    rows = Tc * Bs
    nb = wb_ref.shape[0]
    ns = S5_BLOCK_STATE

    @pl.when(i == 0)
    def _():
        xr_ref[...] = x0r_ref[...]
        xi_ref[...] = x0i_ref[...]

    u = u_ref[...].reshape(rows, W)
    for q in range(nb):
        bu = jnp.dot(u[:, q * S5_BLOCK_IN:(q + 1) * S5_BLOCK_IN].astype(bf16), wb_ref[q],
                     preferred_element_type=f32)
        bur_ref[:, q * ns:(q + 1) * ns] = bu[:, :ns]
        bui_ref[:, q * ns:(q + 1) * ns] = bu[:, ns:]

    for q in range(nb):
        sl = slice(q * ns, (q + 1) * ns)
        lr = jnp.broadcast_to(lbr_ref[:, sl], (Bs, ns))
        li = jnp.broadcast_to(lbi_ref[:, sl], (Bs, ns))

        def step(t, carry):
            xr, xi = carry
            rs = pl.ds(pl.multiple_of(t * Bs, Bs), Bs)
            nr = lr * xr - li * xi + bur_ref[rs, sl]
            ni = lr * xi + li * xr + bui_ref[rs, sl]
            bur_ref[rs, sl] = nr
            bui_ref[rs, sl] = ni
            return nr, ni

        xr, xi = lax.fori_loop(0, Tc, step, (xr_ref[:, sl], xi_ref[:, sl]), unroll=4)
        xr_ref[:, sl] = xr
        xi_ref[:, sl] = xi
        y_ref[:, q * S5_BLOCK_IN:(q + 1) * S5_BLOCK_IN] = (
            jnp.dot(bur_ref[:, sl].astype(bf16), wc_ref[q, :ns, :], preferred_element_type=f32)
            + jnp.dot(bui_ref[:, sl].astype(bf16), wc_ref[q, ns:, :], preferred_element_type=f32))

    y = y_ref[...] + d_ref[...] * u
    y = 0.5 * y * (1.0 + jnp.tanh(math.sqrt(2.0 / math.pi) * (y + 0.044715 * (y * y * y))))
    z = jnp.dot(y.astype(bf16), wg_ref[...], preferred_element_type=f32) + bg_ref[...]
    o_ref[...] = (y * jax.nn.sigmoid(z)).reshape(Tc, Bs, W)

    @pl.when(i == n_steps - 1)
    def _():
        xr_out[...] = xr_ref[...]
        xi_out[...] = xi_ref[...]


def _s5_stage(u_tb, x0_re, x0_im, sw):
    lbr, lbi, wb, wc, d_skip, w_glu, b_glu = sw
    T, B, W = u_tb.shape
    G, S = x0_re.shape[1:]
    n = G * S
    Bs = V7X_SUBLANES
    Tc = min(64, T)
    assert B % Bs == 0 and T % Tc == 0
    rows = Tc * Bs
    const = lambda shape: pl.BlockSpec(shape, lambda j, i: (0,) * len(shape))
    state_spec = pl.BlockSpec((Bs, n), lambda j, i: (j, 0))
    o, xr, xi = pl.pallas_call(
        functools.partial(_s5_body, T // Tc),
        grid=(B // Bs, T // Tc),
        in_specs=[pl.BlockSpec((Tc, Bs, W), lambda j, i: (i, j, 0)), state_spec, state_spec,
                  const((1, n)), const((1, n)), const(wb.shape), const(wc.shape),
                  const((1, W)), const((W, W)), const((1, W))],
        out_specs=[pl.BlockSpec((Tc, Bs, W), lambda j, i: (i, j, 0)), state_spec, state_spec],
        out_shape=[jax.ShapeDtypeStruct((T, B, W), f32),
                   jax.ShapeDtypeStruct((B, n), f32), jax.ShapeDtypeStruct((B, n), f32)],
        scratch_shapes=[pltpu.VMEM((rows, n), f32), pltpu.VMEM((rows, n), f32),
                        pltpu.VMEM((Bs, n), f32), pltpu.VMEM((Bs, n), f32), pltpu.VMEM((rows, W), f32)],
        compiler_params=_cparams(("parallel", "arbitrary")),
        name="s5_scan",
    )(u_tb, x0_re.reshape(B, n).astype(f32), x0_im.reshape(B, n).astype(f32),
      lbr, lbi, wb, wc, d_skip, w_glu, b_glu)
    return o, xr.reshape(B, G, S), xi.reshape(B, G, S)


PACK_ROWS = V7X_SUBLANES
PACK_SPAN = 2 * V7X_LANES
HIGH_HALF = 0xFFFF0000


def _pack_rows(x, out_ref):
    n, D = x.shape
    assert D == PACK_ROWS * PACK_SPAN
    bits = lax.bitcast_convert_type(x.astype(bf16).astype(f32), u32)
    for c in range(PACK_ROWS):
        lo = bits[:, c * PACK_SPAN:c * PACK_SPAN + V7X_LANES]
        hi = bits[:, c * PACK_SPAN + V7X_LANES:(c + 1) * PACK_SPAN]
        out_ref[pl.ds(c, n, stride=PACK_ROWS), :] = (lo >> 16) | (hi & jnp.uint32(HIGH_HALF))


def _unpack_chunk(ref, c, n):
    words = ref[pl.ds(c, n, stride=PACK_ROWS), :]
    lo = lax.bitcast_convert_type(words << 16, f32)
    hi = lax.bitcast_convert_type(words & jnp.uint32(HIGH_HALF), f32)
    return lo, hi


def _merge_body(x_ref, oa_ref, ob_ref, ga_ref, gb_ref, wa_ref, wb_ref, wo_ref, gf_ref, wr_ref, br_ref,
                x1_ref, h2_ref, lg_ref):
    ya = jnp.dot(oa_ref[...].astype(bf16), wa_ref[...], preferred_element_type=f32)
    yb = jnp.dot(ob_ref[...].astype(bf16), wb_ref[...], preferred_element_type=f32)
    merged = jax.nn.sigmoid(ga_ref[...]) * ya + jax.nn.sigmoid(gb_ref[...]) * yb
    x1 = x_ref[...] + jnp.dot(merged.astype(bf16), wo_ref[...], preferred_element_type=f32)
    x1_ref[...] = x1
    ms = jnp.mean(x1 * x1, axis=-1, keepdims=True)
    h2 = x1 * lax.rsqrt(ms + RMS_EPS) * gf_ref[...]
    _pack_rows(h2, h2_ref)
    lg_ref[...] = _dot_nt(wr_ref[...], h2) + br_ref[...]


def _resident(shape, nargs):
    return pl.BlockSpec(shape, lambda *_: (0,) * len(shape), pipeline_mode=pl.Buffered(1))


def _merge_stage(x, o_a, o_b, ob_time_major, proj, mw, tm):
    w_up_a, w_up_b, w_out, g_ffn, w_rt, b_r = mw
    B, T, D = x.shape
    W = o_a.shape[-1]
    E = w_rt.shape[0]
    assert T % tm == 0
    nt = T // tm
    tok = lambda width, col=0: pl.BlockSpec((None, tm, width), lambda b, i, col=col: (b, i, col))
    if ob_time_major:
        o_b = o_b.reshape(T, B * W)
        ob_spec = pl.BlockSpec((tm, W), lambda b, i: (i, b))
    else:
        ob_spec = tok(W)
    return pl.pallas_call(
        _merge_body,
        grid=(B, nt),
        in_specs=[tok(D), tok(W), ob_spec, tok(D, COL_GA), tok(D, COL_GB),
                  _resident((W, D), 2), _resident((W, D), 2), _resident((D, D), 2),
                  _resident((1, D), 2), _resident((E, D), 2), _resident((E, 1), 2)],
        out_specs=[tok(D), pl.BlockSpec((tm * PACK_ROWS, V7X_LANES), lambda b, i: (b * nt + i, 0)),
                   pl.BlockSpec((E, tm), lambda b, i: (0, b * nt + i))],
        out_shape=[jax.ShapeDtypeStruct((B, T, D), f32),
                   jax.ShapeDtypeStruct((B * T * PACK_ROWS, V7X_LANES), u32),
                   jax.ShapeDtypeStruct((E, B * T), f32)],
        compiler_params=_cparams(("parallel", "parallel")),
        name="merge_outproj",
    )(x, o_a, o_b, proj, proj, w_up_a, w_up_b, w_out, g_ffn, w_rt, b_r)


ROUTE_TILE = 512


def _route_body(lg_ref, c0_ref, idx_ref, gate_ref, pos_ref, cnt_ref, carry_ref):
    i = pl.program_id(0)
    E, tt = lg_ref.shape

    @pl.when(i == 0)
    def _():
        carry_ref[...] = c0_ref[...]

    l = lg_ref[...]
    e_iota = lax.broadcasted_iota(i32, (E, tt), 0)
    vals, sels = [], []
    for r in range(TOP_K):
        m = jnp.max(l, axis=0, keepdims=True)
        idx = jnp.min(jnp.where(l == m, e_iota, E), axis=0, keepdims=True)
        sel = e_iota == idx
        l = jnp.where(sel, -jnp.inf, l)
        idx_ref[r:r + 1, :] = idx
        vals.append(m)
        sels.append(sel)
    ex = [jnp.exp(v - vals[0]) for v in vals]
    tot = ex[0] + ex[1] + ex[2] + ex[3]
    for r in range(TOP_K):
        gate_ref[r:r + 1, :] = ex[r] / tot
    member = sels[0] | sels[1] | sels[2] | sels[3]
    mb = jnp.where(member, 1.0, 0.0)
    ti = lax.broadcasted_iota(i32, (tt, tt), 0)
    tj = lax.broadcasted_iota(i32, (tt, tt), 1)
    before = (ti < tj).astype(bf16)
    rank = carry_ref[:, 0:1] + jnp.dot(mb.astype(bf16), before, preferred_element_type=f32)
    for r in range(TOP_K):
        pos_ref[r:r + 1, :] = jnp.sum(jnp.where(sels[r], rank, 0.0), axis=0, keepdims=True).astype(i32)
    carry_ref[...] = carry_ref[...] + jnp.sum(mb, axis=1, keepdims=True)
    cnt_ref[...] = carry_ref[...]


def _route(logits_t, count0):
    E, N = logits_t.shape
    tt = min(ROUTE_TILE, N)
    assert N % tt == 0
    tokspec = pl.BlockSpec((TOP_K, tt), lambda i: (0, i))
    cspec = pl.BlockSpec((E, V7X_LANES), lambda i: (0, 0))
    return pl.pallas_call(
        _route_body,
        grid=(N // tt,),
        in_specs=[pl.BlockSpec((E, tt), lambda i: (0, i)), cspec],
        out_specs=[tokspec, tokspec, tokspec, cspec],
        out_shape=[jax.ShapeDtypeStruct((TOP_K, N), i32), jax.ShapeDtypeStruct((TOP_K, N), f32),
                   jax.ShapeDtypeStruct((TOP_K, N), i32), jax.ShapeDtypeStruct((E, V7X_LANES), f32)],
        scratch_shapes=[pltpu.VMEM((E, V7X_LANES), f32)],
        compiler_params=_cparams(("arbitrary",)),
        name="moe_route",
    )(logits_t, count0)


def _dest_body(idx_ref, pos_ref, start_ref, dest_ref):
    E = start_ref.shape[0]
    tt = idx_ref.shape[1]
    e_iota = lax.broadcasted_iota(i32, (E, tt), 0)
    start = start_ref[:, 0:1]
    for r in range(TOP_K):
        base = jnp.sum(jnp.where(e_iota == idx_ref[r:r + 1, :], start, 0.0), axis=0, keepdims=True)
        dest_ref[r:r + 1, :] = base.astype(i32) + pos_ref[r:r + 1, :]


def _dest(idx, pos, pad_start):
    N = idx.shape[1]
    E = pad_start.shape[0]
    tt = min(ROUTE_TILE, N)
    tokspec = pl.BlockSpec((TOP_K, tt), lambda i: (0, i))
    start = jnp.broadcast_to(pad_start.astype(f32)[:, None], (E, V7X_LANES))
    return pl.pallas_call(
        _dest_body,
        grid=(N // tt,),
        in_specs=[tokspec, tokspec, pl.BlockSpec((E, V7X_LANES), lambda i: (0, 0))],
        out_specs=tokspec,
        out_shape=jax.ShapeDtypeStruct((TOP_K, N), i32),
        compiler_params=_cparams(("parallel",)),
        name="moe_dest",
    )(idx, pos, start)


ZERO_ROWS = 256


def _scatter_body(zero_fill, block_rows, zoff_ref, zflag_ref, dest_ref, h_ref, *rest):
    if zero_fill:
        xs_ref, zeros_ref, sem = rest
    else:
        _, xs_ref, zeros_ref, sem = rest
    i = pl.program_id(0)
    tt = dest_ref.shape[1]

    def packed(ref, row, n):
        return ref.at[pl.ds(pl.multiple_of(row * PACK_ROWS, PACK_ROWS), n * PACK_ROWS)]

    def zero_copy(e, piece):
        return pltpu.make_async_copy(
            zeros_ref, packed(xs_ref, zoff_ref[e] + piece * ZERO_ROWS, ZERO_ROWS), sem.at[0])

    if zero_fill:
        @pl.when(i == 0)
        def _():
            zeros_ref[...] = jnp.zeros_like(zeros_ref)
            n_e = zoff_ref.shape[0]
            for phase in ("start", "wait"):
                def per_expert(e, carry, phase=phase):
                    @pl.when(zflag_ref[e] > 0)
                    def _():
                        for piece in range(block_rows // ZERO_ROWS):
                            cp = zero_copy(e, piece)
                            cp.start() if phase == "start" else cp.wait()
                    return carry
                lax.fori_loop(0, n_e, per_expert, 0)

    def row_copy(t, r):
        return pltpu.make_async_copy(packed(h_ref, i * tt + t, 1), packed(xs_ref, dest_ref[r, t], 1),
                                     sem.at[1])

    def start(t, carry):
        for r in range(TOP_K):
            row_copy(t, r).start()
        return carry

    def wait(t, carry):
        for r in range(TOP_K):
            row_copy(t, r).wait()
        return carry

    lax.fori_loop(0, tt, start, 0)
    lax.fori_loop(0, tt, wait, 0)


def _scatter(h2, dest, zoff, zflag, n_rows, block_rows, xs_prev):
    N = h2.shape[0] // PACK_ROWS
    tt = min(ROUTE_TILE, N)
    zero_fill = xs_prev is None
    any_spec = pl.BlockSpec(memory_space=pl.ANY)
    in_specs = [pl.BlockSpec((TOP_K, tt), lambda i, *_: (0, i), memory_space=pltpu.SMEM), any_spec]
    args = [dest, h2]
    aliases = {}
    if not zero_fill:
        in_specs.append(any_spec)
        args.append(xs_prev)
        aliases = {4: 0}
    return pl.pallas_call(
        functools.partial(_scatter_body, zero_fill, block_rows),
        grid_spec=pltpu.PrefetchScalarGridSpec(
            num_scalar_prefetch=2, grid=(N // tt,), in_specs=in_specs, out_specs=any_spec,
            scratch_shapes=[pltpu.VMEM((ZERO_ROWS * PACK_ROWS, V7X_LANES), u32),
                            pltpu.SemaphoreType.DMA((2,))]),
        out_shape=jax.ShapeDtypeStruct((n_rows * PACK_ROWS, V7X_LANES), u32),
        input_output_aliases=aliases,
        compiler_params=_cparams(("arbitrary",)),
        name="moe_scatter",
    )(zoff, zflag, *args)


def _expert_body(nf, bexp_ref, nused_ref, x_ref, wg_ref, wl_ref, bgl_ref, bll_ref, wd_ref, bd_ref,
                 o_ref, xb_ref, acc_ref):
    b = pl.program_id(0)
    j = pl.program_id(1)
    active = b < nused_ref[0]
    tm = xb_ref.shape[0]

    @pl.when(j == 0)
    def _():
        for c in range(PACK_ROWS):
            lo, hi = _unpack_chunk(x_ref, c, tm)
            xb_ref[:, c * PACK_SPAN:c * PACK_SPAN + V7X_LANES] = lo.astype(bf16)
            xb_ref[:, c * PACK_SPAN + V7X_LANES:(c + 1) * PACK_SPAN] = hi.astype(bf16)
        acc_ref[...] = jnp.where(active, jnp.broadcast_to(bd_ref[...], acc_ref.shape), 0.0)

    @pl.when(active)
    def _():
        xb = xb_ref[...]
        glu = jnp.dot(xb, wg_ref[...].astype(bf16), preferred_element_type=f32) + bgl_ref[...]
        lin = jnp.dot(xb, wl_ref[...].astype(bf16), preferred_element_type=f32) + bll_ref[...]
        glu = jnp.minimum(glu, SWIGLU_LIMIT)
        lin = jnp.clip(lin, -SWIGLU_LIMIT, SWIGLU_LIMIT)
        act = glu * jax.nn.sigmoid(SWIGLU_ALPHA * glu) * (lin + 1.0)
        acc_ref[...] += jnp.dot(act.astype(bf16), wd_ref[...].astype(bf16), preferred_element_type=f32)

    @pl.when(j == nf - 1)
    def _():
        _pack_rows(acc_ref[...], o_ref)


def _experts(xs, bexp, nused, w_gate_up, b_gate_up, w_down, b_down, tm, tf):
    n_rows = xs.shape[0] // PACK_ROWS
    E, D, F2 = w_gate_up.shape
    F = F2 // 2
    assert n_rows % tm == 0 and F % tf == 0
    nf = F // tf
    nb = n_rows // tm

    def blk(b, nused):
        return jnp.minimum(b, nused[0] - 1)

    def ftile(b, j, nused):
        return jnp.where(b < nused[0], j, nf - 1)

    b_gu = b_gate_up.reshape(E, 1, F2)
    b_d = b_down.reshape(E, 1, D)
    return pl.pallas_call(
        functools.partial(_expert_body, nf),
        grid_spec=pltpu.PrefetchScalarGridSpec(
            num_scalar_prefetch=2, grid=(nb, nf),
            in_specs=[
                pl.BlockSpec((tm * PACK_ROWS, V7X_LANES), lambda b, j, be, nu: (blk(b, nu), 0)),
                pl.BlockSpec((None, D, tf), lambda b, j, be, nu: (be[blk(b, nu)], 0, ftile(b, j, nu))),
                pl.BlockSpec((None, D, tf), lambda b, j, be, nu: (be[blk(b, nu)], 0, nf + ftile(b, j, nu))),
                pl.BlockSpec((None, 1, tf), lambda b, j, be, nu: (be[blk(b, nu)], 0, ftile(b, j, nu))),
                pl.BlockSpec((None, 1, tf), lambda b, j, be, nu: (be[blk(b, nu)], 0, nf + ftile(b, j, nu))),
                pl.BlockSpec((None, tf, D), lambda b, j, be, nu: (be[blk(b, nu)], ftile(b, j, nu), 0)),
                pl.BlockSpec((None, 1, D), lambda b, j, be, nu: (be[blk(b, nu)], 0, 0)),
            ],
            out_specs=pl.BlockSpec((tm * PACK_ROWS, V7X_LANES), lambda b, j, be, nu: (b, 0)),
            scratch_shapes=[pltpu.VMEM((tm, D), bf16), pltpu.VMEM((tm, D), f32)]),
        out_shape=jax.ShapeDtypeStruct((n_rows * PACK_ROWS, V7X_LANES), u32),
        compiler_params=_cparams(("arbitrary", "arbitrary")),
        name="moe_experts",
    )(bexp, nused, xs, w_gate_up, w_gate_up, b_gu, b_gu, w_down, b_d)


COMBINE_TILE = 256


def _combine_body(n, dcur_ref, dnext_ref, x1_ref, gate_ref, gf_ref, ys_ref, y_ref, buf_ref, sem):
    i = pl.program_id(0)
    tt = x1_ref.shape[0]

    def packed(row):
        return pl.ds(pl.multiple_of(row * PACK_ROWS, PACK_ROWS), PACK_ROWS)

    def row_copy(d_ref, slot, t, r):
        return pltpu.make_async_copy(ys_ref.at[packed(d_ref[r, t])], buf_ref.at[slot, r, packed(t)],
                                     sem.at[slot])

    def issue(d_ref, slot):
        def body(t, carry):
            for r in range(TOP_K):
                row_copy(d_ref, slot, t, r).start()
            return carry
        lax.fori_loop(0, tt, body, 0)

    @pl.when(i == 0)
    def _():
        issue(dcur_ref, 0)

    @pl.when(i + 1 < n)
    def _():
        issue(dnext_ref, (i + 1) % 2)

    slot = i % 2

    def wait(t, carry):
        for r in range(TOP_K):
            row_copy(dcur_ref, slot, t, r).wait()
        return carry
    lax.fori_loop(0, tt, wait, 0)

    gates = gate_ref[...]
    parts = []
    for c in range(PACK_ROWS):
        lo_sum = x1_ref[:, c * PACK_SPAN:c * PACK_SPAN + V7X_LANES]
        hi_sum = x1_ref[:, c * PACK_SPAN + V7X_LANES:(c + 1) * PACK_SPAN]
        for r in range(TOP_K):
            lo, hi = _unpack_chunk(buf_ref.at[slot, r], c, tt)
            lo_sum = lo_sum + gates[:, r:r + 1] * lo
            hi_sum = hi_sum + gates[:, r:r + 1] * hi
        parts += [lo_sum, hi_sum]
    x = jnp.concatenate(parts, axis=1)
    ms = jnp.mean(x * x, axis=-1, keepdims=True)
    y_ref[...] = x * lax.rsqrt(ms + RMS_EPS) * gf_ref[...]


def _combine(x1, gates_tm, dest, ys, g_final):
    N, D = x1.shape
    tt = min(COMBINE_TILE, N)
    n = N // tt
    smem = lambda f: pl.BlockSpec((TOP_K, tt), f, memory_space=pltpu.SMEM)
    return pl.pallas_call(
        functools.partial(_combine_body, n),
        grid=(n,),
        in_specs=[smem(lambda i: (0, i)), smem(lambda i: (0, jnp.minimum(i + 1, n - 1))),
                  pl.BlockSpec((tt, D), lambda i: (i, 0)), pl.BlockSpec((tt, TOP_K), lambda i: (i, 0)),
                  pl.BlockSpec((1, D), lambda i: (0, 0)), pl.BlockSpec(memory_space=pl.ANY)],
        out_specs=pl.BlockSpec((tt, D), lambda i: (i, 0)),
        out_shape=jax.ShapeDtypeStruct((N, D), f32),
        scratch_shapes=[pltpu.VMEM((2, TOP_K, tt * PACK_ROWS, V7X_LANES), u32),
                        pltpu.SemaphoreType.DMA((2,))],
        compiler_params=_cparams(("arbitrary",)),
        name="moe_combine",
    )(dest, dest, x1, gates_tm, g_final.reshape(1, D), ys)


def _moe(h2_groups, logit_groups, x1_groups, ew, g_final, tm, tf):
    w_gate_up, b_gate_up, w_down, b_down = ew
    E = w_gate_up.shape[0]
    routed = []
    count = jnp.zeros((E, V7X_LANES), f32)
    for lg in logit_groups:
        idx, gate, pos, count = _route(lg, count)
        routed.append((idx, gate, pos))
    total = sum(lg.shape[1] for lg in logit_groups) * TOP_K
    n_blocks = -(-(total + E * (tm - 1)) // tm)
    n_rows = n_blocks * tm
    counts = count[:, 0].astype(i32)
    padded = (counts + tm - 1) // tm * tm
    pad_end = jnp.cumsum(padded)
    pad_start = pad_end - padded
    n_used = pad_end[-1] // tm
    blocks = jnp.minimum(jnp.arange(n_blocks, dtype=i32), n_used - 1)
    bexp = jnp.minimum(jnp.sum(blocks[:, None] * tm >= pad_end[None, :], axis=1), E - 1).astype(i32)
    all_blocks = jnp.arange(n_blocks, dtype=i32)
    zoff = jnp.concatenate([jnp.maximum(pad_end - tm, 0).astype(i32), all_blocks * tm])
    zflag = jnp.concatenate([counts > 0, all_blocks >= n_used]).astype(i32)
    xs = None
    dests = []
    for h2, (idx, gate, pos) in zip(h2_groups, routed):
        dest = _dest(idx, pos, pad_start)
        dests.append(dest)
        xs = _scatter(h2, dest, zoff, zflag, n_rows, tm, xs)
    ys = _experts(xs, bexp, n_used.reshape(1).astype(i32), w_gate_up, b_gate_up, w_down, b_down, tm, tf)
    return [_combine(x1, gate.T, dest, ys, g_final)
            for x1, (idx, gate, pos), dest in zip(x1_groups, routed, dests)]


PROJ_TILE_M = 512
PROJ_TILE_N = 1920
MERGE_TILE_M = 256
EXPERT_TILE_M = 1024
EXPERT_TILE_F = 256


def _mixers(x, state, norm_g, w_main, w_u, rw, sw, mw):
    B, T, D = x.shape
    W = w_u.shape[1]
    shift_w = 3 * W + W_LORA + A_LORA + G_LORA
    G, S = sw[0].shape[1] // 64, 64
    if state is None:
        s_rwkv0 = jnp.zeros((B, W // HEAD, HEAD, HEAD), f32)
        s5_re0 = jnp.zeros((B, W // 16, 64), f32)
        s5_im0 = jnp.zeros((B, W // 16, 64), f32)
        shift0 = jnp.zeros((B, shift_w), f32)
    else:
        s_rwkv0, s5_re0, s5_im0, shift0 = state
    per_batch = T % PROJ_TILE_M == 0
    xf = x if per_batch else x.reshape(1, B * T, D)
    tm = PROJ_TILE_M if per_batch else min(PROJ_TILE_M, B * T)
    proj = _norm_proj(xf, norm_g, w_main, tm, PROJ_TILE_N, False).reshape(B, T, MAIN_COLS)
    if per_batch:
        u_tb = _norm_proj(xf, norm_g, w_u, tm, W, True)
    else:
        u_tb = jnp.swapaxes(_norm_proj(xf, norm_g, w_u, tm, W, False).reshape(B, T, W), 0, 1)
    o_a, s_rwkv = _rwkv_stage(proj, shift0, s_rwkv0, rw)
    o_b, s5_re, s5_im = _s5_stage(u_tb, s5_re0, s5_im0, sw)
    if per_batch:
        x1, h2, logits_t = _merge_stage(x, o_a, o_b, True, proj, mw, MERGE_TILE_M)
    else:
        flat = lambda t: t.reshape(1, B * T, t.shape[-1])
        x1, h2, logits_t = _merge_stage(flat(x), flat(o_a), flat(jnp.swapaxes(o_b, 0, 1)), False,
                                        flat(proj), mw, min(MERGE_TILE_M, B * T))
    last = proj[:, -1, :]
    n_lora = W_LORA + A_LORA + G_LORA
    shift = jnp.concatenate([last[:, COL_R * W:(COL_R + 3) * W],
                             last[:, COL_LORA * LORA_PAD:COL_LORA * LORA_PAD + n_lora]], axis=1)
    return (x1.reshape(B * T, D), h2, logits_t), (s_rwkv, s5_re, s5_im, shift)


def kernel(x_prompt, x_sample, state_rwkv, state_s5_re, state_s5_im, state_shift, norm_mix_g, w_in, mu_shift, w0, w2, a0, a2, g2, k_k, k_a, r_k, lnx_g, lnx_b, lam_re, lam_im, log_dt, b_re, b_im, c_re, c_im, d_skip, w_glu, b_glu, w_up_a, w_up_b, w_out, norm_ffn_g, w_router, b_router, w_gate_up, b_gate_up, w_down, b_down, norm_final_g):
    assert w_in.shape[0] == 1, "single-layer trunk"
    D = x_prompt.shape[-1]
    W = w0.shape[-1]
    n_lora = W_LORA + A_LORA + G_LORA
    shift_w = 3 * W + n_lora
    w = w_in[0]
    ga0 = shift_w + W
    w_main = jnp.concatenate(
        [w[:, ga0:ga0 + 2 * D], w[:, :3 * W], w[:, 3 * W:shift_w],
         jnp.zeros((D, LORA_PAD - n_lora), w.dtype)], axis=1).astype(bf16)
    assert w_main.shape[1] == MAIN_COLS
    w_u = w[:, shift_w:shift_w + W].astype(bf16)
    rw = _rwkv_weights(mu_shift[0], w0[0], w2[0], a0[0], a2[0], g2[0], k_k[0], k_a[0], r_k[0],
                       lnx_g[0], lnx_b[0])
    sw = _s5_weights(lam_re[0], lam_im[0], log_dt[0], b_re[0], b_im[0], c_re[0], c_im[0],
                     d_skip[0], w_glu[0], b_glu[0])
    E = w_router.shape[-1]
    mw = (w_up_a[0].astype(bf16), w_up_b[0].astype(bf16), w_out[0].astype(bf16),
          norm_ffn_g[0].reshape(1, D).astype(f32), w_router[0].T.astype(bf16),
          b_router[0].reshape(E, 1).astype(f32))
    ew = (w_gate_up[0], b_gate_up[0], w_down[0], b_down[0])

    state_s = (state_rwkv[0], state_s5_re[0], state_s5_im[0], state_shift[0])
    tok_p, st_p = _mixers(x_prompt, None, norm_mix_g[0], w_main, w_u, rw, sw, mw)
    tok_s, st_s = _mixers(x_sample, state_s, norm_mix_g[0], w_main, w_u, rw, sw, mw)
    y_p, y_s = _moe([tok_p[1], tok_s[1]], [tok_p[2], tok_s[2]], [tok_p[0], tok_s[0]], ew, norm_final_g,
                    EXPERT_TILE_M, EXPERT_TILE_F)
    lead = lambda t: t[None]
    return (y_p.reshape(x_prompt.shape), y_s.reshape(x_sample.shape),
            lead(st_p[0]), lead(st_p[1]), lead(st_p[2]), lead(st_p[3]),
            lead(st_s[0]), lead(st_s[1]), lead(st_s[2]), lead(st_s[3]))
```

```python
import functools
import math

import jax
import jax.numpy as jnp
from jax import lax
from jax.experimental import pallas as pl
from jax.experimental.pallas import tpu as pltpu

f32 = jnp.float32
bf16 = jnp.bfloat16
i32 = jnp.int32
u32 = jnp.uint32

V7X_LANES = 128
V7X_SUBLANES = 8
V7X_VMEM_BYTES = 64 * 1024 * 1024
VMEM_LIMIT = 56 * 1024 * 1024

RMS_EPS = 1e-5
GN_EPS = 64e-5
HEAD = 64
HEAD_SHIFT = 6
HEADS_PER_GROUP = 4
GROUP = HEAD * HEADS_PER_GROUP
CHUNK = 64
W_LORA, A_LORA, G_LORA = 64, 64, 160
LORA_PAD = 512
TOP_K = 4
SWIGLU_LIMIT = 7.0
SWIGLU_ALPHA = 1.702


def _cparams(sem):
    return pltpu.CompilerParams(dimension_semantics=sem, vmem_limit_bytes=VMEM_LIMIT)


def _dot(a, b):
    return jnp.dot(a.astype(bf16), b.astype(bf16), preferred_element_type=f32)


def _dot_nt(a, b):
    return lax.dot_general(a.astype(bf16), b.astype(bf16), (((1,), (1,)), ((), ())),
                           preferred_element_type=f32)


def _dot_tn(a, b):
    return lax.dot_general(a.astype(bf16), b.astype(bf16), (((0,), (0,)), ((), ())),
                           preferred_element_type=f32)


def _split(x):
    hi = x.astype(bf16)
    lo = (x - hi.astype(f32)).astype(bf16)
    return hi, lo


def _dot_exact_rhs(a, b_exact):
    hi, lo = _split(a)
    return (jnp.dot(hi, b_exact, preferred_element_type=f32)
            + jnp.dot(lo, b_exact, preferred_element_type=f32))


def _dot_exact_lhs(a_exact, b):
    hi, lo = _split(b)
    return (jnp.dot(a_exact, hi, preferred_element_type=f32)
            + jnp.dot(a_exact, lo, preferred_element_type=f32))


def _norm_proj_body(x_ref, g_ref, w_ref, o_ref, h_ref):
    @pl.when(pl.program_id(2) == 0)
    def _():
        x = x_ref[...]
        ms = jnp.mean(x * x, axis=-1, keepdims=True)
        h_ref[...] = (x * lax.rsqrt(ms + RMS_EPS) * g_ref[...]).astype(bf16)

    o_ref[...] = jnp.dot(h_ref[...], w_ref[...], preferred_element_type=f32)


def _norm_proj(x, gain, w, tm, tn, time_major_out):
    B, T, D = x.shape
    N = w.shape[1]
    assert T % tm == 0 and N % tn == 0
    if time_major_out:
        out_shape = jax.ShapeDtypeStruct((T, B * N), f32)
        out_spec = pl.BlockSpec((tm, tn), lambda b, i, n: (i, b * (N // tn) + n))
    else:
        out_shape = jax.ShapeDtypeStruct((B, T, N), f32)
        out_spec = pl.BlockSpec((None, tm, tn), lambda b, i, n: (b, i, n))
    out = pl.pallas_call(
        _norm_proj_body,
        grid=(B, T // tm, N // tn),
        in_specs=[pl.BlockSpec((None, tm, D), lambda b, i, n: (b, i, 0)),
                  pl.BlockSpec((1, D), lambda b, i, n: (0, 0)),
                  pl.BlockSpec((D, tn), lambda b, i, n: (0, n))],
        out_specs=out_spec,
        out_shape=out_shape,
        scratch_shapes=[pltpu.VMEM((tm, D), bf16)],
        compiler_params=_cparams(("parallel", "parallel", "arbitrary")),
        name="norm_proj",
    )(x, gain.reshape(1, D), w)
    return out.reshape(T, B, N) if time_major_out else out


PV_MU_R, PV_MU_K, PV_MU_V, PV_W0, PV_A0, PV_KK, PV_KA, PV_RK, PV_LNG, PV_LNB = range(10)
PV_ROWS = 16


def _rwkv_body(nc, r_ref, k_ref, v_ref, l_ref, pr_ref, pk_ref, pv_ref, plo_ref,
               lr_ref, lk_ref, lv_ref, llo_ref, pvec_ref, mul_ref, wl_ref, s0_ref,
               o_ref, so_ref, s_ref):
    c = pl.program_id(1)
    C = r_ref.shape[0]
    W = r_ref.shape[1]
    n_groups = W // GROUP

    @pl.when(c == 0)
    def _():
        s_ref[...] = s0_ref[...]

    row = lax.broadcasted_iota(i32, (C, 1), 0)

    def shifted(x_ref, prev8_ref, last_ref, mu):
        x = x_ref[...]
        prev_row = jnp.where(c > 0, prev8_ref[V7X_SUBLANES - 1:V7X_SUBLANES, :], last_ref[...])
        prev = jnp.where(row == 0, prev_row, pltpu.roll(x, shift=1, axis=0))
        return x + (prev - x) * mu

    pvec = pvec_ref[...]
    prm = lambda i: pvec[i:i + 1, :]
    r = shifted(r_ref, pr_ref, lr_ref, prm(PV_MU_R))
    k = shifted(k_ref, pk_ref, lk_ref, prm(PV_MU_K))
    v = shifted(v_ref, pv_ref, lv_ref, prm(PV_MU_V))
    xl = shifted(l_ref, plo_ref, llo_ref, mul_ref[...])

    lane_l = lax.broadcasted_iota(i32, (1, xl.shape[1]), 1)
    act = jnp.where(lane_l < W_LORA, jnp.tanh(xl),
                    jnp.where(lane_l < W_LORA + A_LORA, xl, jax.nn.sigmoid(xl)))
    lo = jnp.dot(act.astype(bf16), wl_ref[...], preferred_element_type=f32)
    z = -(prm(PV_W0) + lo[:, :W])
    w = -(jnp.maximum(z, 0.0) + jnp.log1p(jnp.exp(-jnp.abs(z)))) - 0.5
    logd = -jnp.exp(w)
    a = jax.nn.sigmoid(prm(PV_A0) + lo[:, W:2 * W])
    g = lo[:, 2 * W:]

    gi = lax.broadcasted_iota(i32, (GROUP, GROUP), 0)
    gj = lax.broadcasted_iota(i32, (GROUP, GROUP), 1)
    same_head = (gi >> HEAD_SHIFT) == (gj >> HEAD_SHIFT)
    head_ones = jnp.where(same_head, 1.0, 0.0).astype(bf16)
    ti, tj = gi & (HEAD - 1), gj & (HEAD - 1)
    strict = same_head & (ti > tj)
    incl = same_head & (ti >= tj)
    ci = lax.broadcasted_iota(i32, (C, C), 0)
    cj = lax.broadcasted_iota(i32, (C, C), 1)
    tri = jnp.where(ci >= cj, 1.0, 0.0).astype(bf16)
    lane_head = lax.broadcasted_iota(i32, (1, GROUP), 1) >> HEAD_SHIFT
    eye_quad = jnp.where(lax.broadcasted_iota(i32, (C, GROUP), 0)
                         == (lax.broadcasted_iota(i32, (C, GROUP), 1) & (HEAD - 1)), 1.0, 0.0)

    def head_sum(x, exact):
        parts = []
        for gq in range(n_groups):
            xg = x[:, gq * GROUP:(gq + 1) * GROUP]
            parts.append(_dot_exact_rhs(xg, head_ones) if exact
                         else jnp.dot(xg.astype(bf16), head_ones, preferred_element_type=f32))
        return jnp.concatenate(parts, axis=1)

    kk = k * prm(PV_KK)
    kk = kk / jnp.maximum(jnp.sqrt(head_sum(kk * kk, True)), 1e-12)
    k2 = k * (1.0 + (a - 1.0) * prm(PV_KA))
    cl = _dot_exact_lhs(tri, logd)
    p_incl = jnp.exp(cl)
    p_excl = jnp.exp(cl - logd)
    p_inv = jnp.exp(-cl)
    rt = r * p_incl
    at = -kk * p_excl
    bt = kk * a * p_inv
    kt = k2 * p_inv
    p_end = p_incl[C - 1:C, :]
    bonus = head_sum(r * k2 * prm(PV_RK), False) * v

    def stack(x):
        return jnp.concatenate([jnp.where(lane_head == h, x, 0.0) for h in range(HEADS_PER_GROUP)], axis=0)

    def tile(x):
        return jnp.concatenate([x] * HEADS_PER_GROUP, axis=0)

    def fold(x):
        out = x[0:C]
        for h in range(1, HEADS_PER_GROUP):
            out = out + x[h * C:(h + 1) * C]
        return out

    def block(x):
        return jnp.where(same_head, tile(x), 0.0)

    groups = range(n_groups)
    each = lambda f, *lists: [f(*xs) for xs in zip(*lists)]
    sls = [slice(gq * GROUP, (gq + 1) * GROUP) for gq in groups]
    ag, rg, bg, kg, vg = ([x[:, sl] for sl in sls] for x in (at, rt, bt, kt, v))
    s_old = [s_ref[gq] for gq in groups]
    sa, sr, sv = each(stack, ag), each(stack, rg), each(stack, vg)
    tb, tk = each(tile, bg), each(tile, kg)
    ab = each(lambda x, y: jnp.where(strict, _dot_nt(x, y), 0.0), sa, tb)
    ak = each(lambda x, y: jnp.where(strict, _dot_nt(x, y), 0.0), sa, tk)
    rb = each(lambda x, y: jnp.where(incl, _dot_nt(x, y), 0.0), sr, tb)
    rk = each(lambda x, y: jnp.where(incl, _dot_nt(x, y), 0.0), sr, tk)
    q4, qb = each(fold, ab), ab
    t4 = each(lambda q: eye_quad + q, q4)
    for _ in range(max(1, (C - 1).bit_length() - 1)):
        q4 = each(_dot, q4, qb)
        qb = each(block, q4)
        t4 = each(lambda t, q: t + _dot(t, q), t4, qb)
    w4 = each(_dot, t4, sa)
    akv = each(lambda x, y: _dot(fold(x), y), ak, sv)
    uv = each(lambda t, x: _dot(t, stack(x)), t4, akv)
    orv = each(lambda x, y: _dot(fold(x), y), rk, sv)
    u4 = each(lambda w_, s_, uv_: _dot_nt(w_, s_) + uv_, w4, s_old, uv)
    rs = each(_dot_nt, rg, s_old)
    upd = each(lambda u_, v_, b_, k_: _dot_tn(jnp.concatenate([u_, v_], axis=0),
                                              jnp.concatenate([b_, k_], axis=0)), u4, vg, bg, kg)
    o4 = each(lambda rs_, rb_, u_, orv_: rs_ + _dot(fold(rb_), stack(u_)) + orv_, rs, rb, u4, orv)
    for gq in groups:
        s_ref[gq] = (s_old[gq] + jnp.where(same_head, upd[gq], 0.0)) * p_end[:, sls[gq]]
    o = jnp.concatenate(o4, axis=1)

    inv_n = 1.0 / HEAD
    mean = head_sum(o, False) * inv_n
    d = o - mean
    var = head_sum(d * d, False) * inv_n
    on = d * lax.rsqrt(var + GN_EPS)
    o_ref[...] = (on * prm(PV_LNG) + prm(PV_LNB) + bonus) * g

    @pl.when(c == nc - 1)
    def _():
        so_ref[...] = s_ref[...]


def _rwkv(proj, col_r, col_lora, last_rkv, last_lora, pvec, mu_lora, w_lora, s0_blocks):
    B, T, _ = proj.shape
    W = pvec.shape[1]
    C = CHUNK
    assert T % C == 0 and C == HEAD and HEAD == 1 << HEAD_SHIFT
    n_groups = W // GROUP
    rb = C // V7X_SUBLANES

    def cur(col, width):
        return pl.BlockSpec((None, C, width), lambda b, c, col=col: (b, c, col))

    def prev(col, width):
        return pl.BlockSpec((None, V7X_SUBLANES, width),
                            lambda b, c, col=col: (b, jnp.maximum(c * rb - 1, 0), col))

    def last(col, width):
        return pl.BlockSpec((None, 1, width), lambda b, c, col=col: (b, 0, col))

    const = lambda shape: pl.BlockSpec(shape, lambda b, c: (0,) * len(shape))
    state_spec = pl.BlockSpec((None, n_groups, GROUP, GROUP), lambda b, c: (b, 0, 0, 0))
    return pl.pallas_call(
        functools.partial(_rwkv_body, T // C),
        grid=(B, T // C),
        in_specs=[cur(col_r, W), cur(col_r + 1, W), cur(col_r + 2, W), cur(col_lora, LORA_PAD),
                  prev(col_r, W), prev(col_r + 1, W), prev(col_r + 2, W), prev(col_lora, LORA_PAD),
                  last(0, W), last(1, W), last(2, W), last(0, LORA_PAD),
                  const((PV_ROWS, W)), const((1, LORA_PAD)), const((LORA_PAD, 3 * W)), state_spec],
        out_specs=[pl.BlockSpec((None, C, W), lambda b, c: (b, c, 0)), state_spec],
        out_shape=[jax.ShapeDtypeStruct((B, T, W), f32),
                   jax.ShapeDtypeStruct((B, n_groups, GROUP, GROUP), f32)],
        scratch_shapes=[pltpu.VMEM((n_groups, GROUP, GROUP), f32)],
        compiler_params=_cparams(("parallel", "arbitrary")),
        name="rwkv7_chunked",
    )(proj, proj, proj, proj, proj, proj, proj, proj,
      last_rkv, last_rkv, last_rkv, last_lora, pvec, mu_lora, w_lora, s0_blocks)


COL_GA, COL_GB = 0, 1
COL_R = 4
COL_LORA = 14
MAIN_COLS = 7680


def _rwkv_weights(mu_shift, w0, w2, a0, a2, g2, k_k, k_a, r_k, lnx_g, lnx_b):
    W = w0.shape[0]
    rows = [mu_shift[:W], mu_shift[W:2 * W], mu_shift[2 * W:3 * W], w0, a0, k_k, k_a,
            r_k.reshape(W), lnx_g, lnx_b]
    pvec = jnp.zeros((PV_ROWS, W), f32).at[:len(rows)].set(jnp.stack(rows).astype(f32))
    n_lora = W_LORA + A_LORA + G_LORA
    mu_lora = jnp.zeros((1, LORA_PAD), f32).at[0, :n_lora].set(mu_shift[3 * W:])
    w_lora = jnp.zeros((LORA_PAD, 3 * W), f32)
    w_lora = w_lora.at[:W_LORA, :W].set(w2)
    w_lora = w_lora.at[W_LORA:W_LORA + A_LORA, W:2 * W].set(a2)
    w_lora = w_lora.at[W_LORA + A_LORA:n_lora, 2 * W:].set(g2)
    return pvec, mu_lora, w_lora.astype(bf16)


def _rwkv_stage(proj, shift0, s0, rw):
    pvec, mu_lora, w_lora = rw
    B = proj.shape[0]
    W = pvec.shape[1]
    H = W // HEAD
    n_groups = H // HEADS_PER_GROUP
    last_rkv = shift0[:, None, :3 * W].astype(f32)
    last_lora = jnp.zeros((B, 1, LORA_PAD), f32).at[:, 0, :shift0.shape[1] - 3 * W].set(shift0[:, 3 * W:])
    eye = jnp.eye(HEADS_PER_GROUP, dtype=f32)
    s0_blocks = jnp.einsum('bghij,hk->bghikj', s0.reshape(B, n_groups, HEADS_PER_GROUP, HEAD, HEAD), eye)
    s0_blocks = s0_blocks.reshape(B, n_groups, GROUP, GROUP)
    o, s_blocks = _rwkv(proj, COL_R, COL_LORA, last_rkv, last_lora, pvec, mu_lora, w_lora, s0_blocks)
    s5 = s_blocks.reshape(B, n_groups, HEADS_PER_GROUP, HEAD, HEADS_PER_GROUP, HEAD)
    s_new = jnp.einsum('bghikj,hk->bghij', s5, eye).reshape(B, H, HEAD, HEAD)
    return o, s_new


S5_BLOCK_IN = 128
S5_BLOCK_STATE = 512


def _s5_prep_body(lr_ref, li_ref, ldt_ref, br_ref, bi_ref, lbr_ref, lbi_ref, bbr_ref, bbi_ref):
    lr, li = lr_ref[...], li_ref[...]
    dt = jnp.exp(ldt_ref[...])
    mag = jnp.exp(lr * dt)
    er, ei = mag * jnp.cos(li * dt), mag * jnp.sin(li * dt)
    lbr_ref[...] = er
    lbi_ref[...] = ei
    den = lr * lr + li * li
    cr = ((er - 1.0) * lr + ei * li) / den
    ci = (ei * lr - (er - 1.0) * li) / den
    br, bi = br_ref[...], bi_ref[...]
    bbr_ref[...] = cr * br - ci * bi
    bbi_ref[...] = cr * bi + ci * br


def _s5_weights(lam_re, lam_im, log_dt, b_re, b_im, c_re, c_im, d_skip, w_glu, b_glu):
    G, S = lam_re.shape
    P = b_re.shape[-1]
    n = G * S
    col = lambda t: t.reshape(n, 1).astype(f32)
    ldt = jnp.broadcast_to(log_dt[:, None], (G, S))
    shapes = [jax.ShapeDtypeStruct((n, 1), f32)] * 2 + [jax.ShapeDtypeStruct((n, P), f32)] * 2
    lbr, lbi, bbr, bbi = pl.pallas_call(_s5_prep_body, out_shape=shapes, name="s5_discretise")(
        col(lam_re), col(lam_im), col(ldt), b_re.reshape(n, P).astype(f32), b_im.reshape(n, P).astype(f32))
    gpb = S5_BLOCK_IN // P
    nb = G // gpb
    eye = jnp.eye(gpb, dtype=f32)

    def in_block(bb):
        t = bb.reshape(nb, gpb, S, P)
        return jnp.einsum('qgsp,gh->qgphs', t, eye).reshape(nb, gpb * P, gpb * S)

    def out_block(cc):
        t = cc.astype(f32).reshape(nb, gpb, P, S)
        return jnp.einsum('qgps,gh->qgshp', t, eye).reshape(nb, gpb * S, gpb * P)

    wb = jnp.concatenate([in_block(bbr), in_block(bbi)], axis=2).astype(bf16)
    wc = jnp.concatenate([out_block(c_re), -out_block(c_im)], axis=1).astype(bf16)
    W = G * P
    return (lbr.reshape(1, n), lbi.reshape(1, n), wb, wc, d_skip.reshape(1, W).astype(f32),
            w_glu.astype(bf16), b_glu.reshape(1, W).astype(f32))


def _s5_body(n_steps, u_ref, x0r_ref, x0i_ref, lbr_ref, lbi_ref, wb_ref, wc_ref, d_ref, wg_ref, bg_ref,
             o_ref, xr_out, xi_out, bur_ref, bui_ref, xr_ref, xi_ref, y_ref):
    i = pl.program_id(1)
    Tc, Bs, W = u_ref.shape
    rows = Tc * Bs
    nb = wb_ref.shape[0]
    ns = S5_BLOCK_STATE

    @pl.when(i == 0)
    def _():
        xr_ref[...] = x0r_ref[...]
        xi_ref[...] = x0i_ref[...]

    u = u_ref[...].reshape(rows, W)
    for q in range(nb):
        bu = jnp.dot(u[:, q * S5_BLOCK_IN:(q + 1) * S5_BLOCK_IN].astype(bf16), wb_ref[q],
                     preferred_element_type=f32)
        bur_ref[:, q * ns:(q + 1) * ns] = bu[:, :ns]
        bui_ref[:, q * ns:(q + 1) * ns] = bu[:, ns:]

    for q in range(nb):
        sl = slice(q * ns, (q + 1) * ns)
        lr = jnp.broadcast_to(lbr_ref[:, sl], (Bs, ns))
        li = jnp.broadcast_to(lbi_ref[:, sl], (Bs, ns))

        def step(t, carry):
            xr, xi = carry
            rs = pl.ds(pl.multiple_of(t * Bs, Bs), Bs)
            nr = lr * xr - li * xi + bur_ref[rs, sl]
            ni = lr * xi + li * xr + bui_ref[rs, sl]
            bur_ref[rs, sl] = nr
            bui_ref[rs, sl] = ni
            return nr, ni

        xr, xi = lax.fori_loop(0, Tc, step, (xr_ref[:, sl], xi_ref[:, sl]), unroll=4)
        xr_ref[:, sl] = xr
        xi_ref[:, sl] = xi
        y_ref[:, q * S5_BLOCK_IN:(q + 1) * S5_BLOCK_IN] = (
            jnp.dot(bur_ref[:, sl].astype(bf16), wc_ref[q, :ns, :], preferred_element_type=f32)
            + jnp.dot(bui_ref[:, sl].astype(bf16), wc_ref[q, ns:, :], preferred_element_type=f32))

    y = y_ref[...] + d_ref[...] * u
    y = 0.5 * y * (1.0 + jnp.tanh(math.sqrt(2.0 / math.pi) * (y + 0.044715 * (y * y * y))))
    z = jnp.dot(y.astype(bf16), wg_ref[...], preferred_element_type=f32) + bg_ref[...]
    o_ref[...] = (y * jax.nn.sigmoid(z)).reshape(Tc, Bs, W)

    @pl.when(i == n_steps - 1)
    def _():
        xr_out[...] = xr_ref[...]
        xi_out[...] = xi_ref[...]


def _s5_stage(u_tb, x0_re, x0_im, sw):
    lbr, lbi, wb, wc, d_skip, w_glu, b_glu = sw
    T, B, W = u_tb.shape
    G, S = x0_re.shape[1:]
    n = G * S
    Bs = V7X_SUBLANES
    Tc = min(64, T)
    assert B % Bs == 0 and T % Tc == 0
    rows = Tc * Bs
    const = lambda shape: pl.BlockSpec(shape, lambda j, i: (0,) * len(shape))
    state_spec = pl.BlockSpec((Bs, n), lambda j, i: (j, 0))
    o, xr, xi = pl.pallas_call(
        functools.partial(_s5_body, T // Tc),
        grid=(B // Bs, T // Tc),
        in_specs=[pl.BlockSpec((Tc, Bs, W), lambda j, i: (i, j, 0)), state_spec, state_spec,
                  const((1, n)), const((1, n)), const(wb.shape), const(wc.shape),
                  const((1, W)), const((W, W)), const((1, W))],
        out_specs=[pl.BlockSpec((Tc, Bs, W), lambda j, i: (i, j, 0)), state_spec, state_spec],
        out_shape=[jax.ShapeDtypeStruct((T, B, W), f32),
                   jax.ShapeDtypeStruct((B, n), f32), jax.ShapeDtypeStruct((B, n), f32)],
        scratch_shapes=[pltpu.VMEM((rows, n), f32), pltpu.VMEM((rows, n), f32),
                        pltpu.VMEM((Bs, n), f32), pltpu.VMEM((Bs, n), f32), pltpu.VMEM((rows, W), f32)],
        compiler_params=_cparams(("parallel", "arbitrary")),
        name="s5_scan",
    )(u_tb, x0_re.reshape(B, n).astype(f32), x0_im.reshape(B, n).astype(f32),
      lbr, lbi, wb, wc, d_skip, w_glu, b_glu)
    return o, xr.reshape(B, G, S), xi.reshape(B, G, S)


PACK_ROWS = V7X_SUBLANES
PACK_SPAN = 2 * V7X_LANES
HIGH_HALF = 0xFFFF0000


def _pack_rows(x, out_ref):
    n, D = x.shape
    assert D == PACK_ROWS * PACK_SPAN
    bits = lax.bitcast_convert_type(x.astype(bf16).astype(f32), u32)
    for c in range(PACK_ROWS):
        lo = bits[:, c * PACK_SPAN:c * PACK_SPAN + V7X_LANES]
        hi = bits[:, c * PACK_SPAN + V7X_LANES:(c + 1) * PACK_SPAN]
        out_ref[pl.ds(c, n, stride=PACK_ROWS), :] = (lo >> 16) | (hi & jnp.uint32(HIGH_HALF))


def _unpack_chunk(ref, c, n):
    words = ref[pl.ds(c, n, stride=PACK_ROWS), :]
    lo = lax.bitcast_convert_type(words << 16, f32)
    hi = lax.bitcast_convert_type(words & jnp.uint32(HIGH_HALF), f32)
    return lo, hi


def _merge_body(x_ref, oa_ref, ob_ref, ga_ref, gb_ref, wa_ref, wb_ref, wo_ref, gf_ref, wr_ref, br_ref,
                x1_ref, h2_ref, lg_ref):
    ya = jnp.dot(oa_ref[...].astype(bf16), wa_ref[...], preferred_element_type=f32)
    yb = jnp.dot(ob_ref[...].astype(bf16), wb_ref[...], preferred_element_type=f32)
    merged = jax.nn.sigmoid(ga_ref[...]) * ya + jax.nn.sigmoid(gb_ref[...]) * yb
    x1 = x_ref[...] + jnp.dot(merged.astype(bf16), wo_ref[...], preferred_element_type=f32)
    x1_ref[...] = x1
    ms = jnp.mean(x1 * x1, axis=-1, keepdims=True)
    h2 = x1 * lax.rsqrt(ms + RMS_EPS) * gf_ref[...]
    _pack_rows(h2, h2_ref)
    lg_ref[...] = _dot_nt(wr_ref[...], h2) + br_ref[...]


def _resident(shape, nargs):
    return pl.BlockSpec(shape, lambda *_: (0,) * len(shape), pipeline_mode=pl.Buffered(1))


def _merge_stage(x, o_a, o_b, ob_time_major, proj, mw, tm):
    w_up_a, w_up_b, w_out, g_ffn, w_rt, b_r = mw
    B, T, D = x.shape
    W = o_a.shape[-1]
    E = w_rt.shape[0]
    assert T % tm == 0
    nt = T // tm
    tok = lambda width, col=0: pl.BlockSpec((None, tm, width), lambda b, i, col=col: (b, i, col))
    if ob_time_major:
        o_b = o_b.reshape(T, B * W)
        ob_spec = pl.BlockSpec((tm, W), lambda b, i: (i, b))
    else:
        ob_spec = tok(W)
    return pl.pallas_call(
        _merge_body,
        grid=(B, nt),
        in_specs=[tok(D), tok(W), ob_spec, tok(D, COL_GA), tok(D, COL_GB),
                  _resident((W, D), 2), _resident((W, D), 2), _resident((D, D), 2),
                  _resident((1, D), 2), _resident((E, D), 2), _resident((E, 1), 2)],
        out_specs=[tok(D), pl.BlockSpec((tm * PACK_ROWS, V7X_LANES), lambda b, i: (b * nt + i, 0)),
                   pl.BlockSpec((E, tm), lambda b, i: (0, b * nt + i))],
        out_shape=[jax.ShapeDtypeStruct((B, T, D), f32),
                   jax.ShapeDtypeStruct((B * T * PACK_ROWS, V7X_LANES), u32),
                   jax.ShapeDtypeStruct((E, B * T), f32)],
        compiler_params=_cparams(("parallel", "parallel")),
        name="merge_outproj",
    )(x, o_a, o_b, proj, proj, w_up_a, w_up_b, w_out, g_ffn, w_rt, b_r)


ROUTE_TILE = 512


def _route_body(lg_ref, c0_ref, idx_ref, gate_ref, pos_ref, cnt_ref, carry_ref):
    i = pl.program_id(0)
    E, tt = lg_ref.shape

    @pl.when(i == 0)
    def _():
        carry_ref[...] = c0_ref[...]

    l = lg_ref[...]
    e_iota = lax.broadcasted_iota(i32, (E, tt), 0)
    vals, sels = [], []
    for r in range(TOP_K):
        m = jnp.max(l, axis=0, keepdims=True)
        idx = jnp.min(jnp.where(l == m, e_iota, E), axis=0, keepdims=True)
        sel = e_iota == idx
        l = jnp.where(sel, -jnp.inf, l)
        idx_ref[r:r + 1, :] = idx
        vals.append(m)
        sels.append(sel)
    ex = [jnp.exp(v - vals[0]) for v in vals]
    tot = ex[0] + ex[1] + ex[2] + ex[3]
    for r in range(TOP_K):
        gate_ref[r:r + 1, :] = ex[r] / tot
    member = sels[0] | sels[1] | sels[2] | sels[3]
    mb = jnp.where(member, 1.0, 0.0)
    ti = lax.broadcasted_iota(i32, (tt, tt), 0)
    tj = lax.broadcasted_iota(i32, (tt, tt), 1)
    before = (ti < tj).astype(bf16)
    rank = carry_ref[:, 0:1] + jnp.dot(mb.astype(bf16), before, preferred_element_type=f32)
    for r in range(TOP_K):
        pos_ref[r:r + 1, :] = jnp.sum(jnp.where(sels[r], rank, 0.0), axis=0, keepdims=True).astype(i32)
    carry_ref[...] = carry_ref[...] + jnp.sum(mb, axis=1, keepdims=True)
    cnt_ref[...] = carry_ref[...]


def _route(logits_t, count0):
    E, N = logits_t.shape
    tt = min(ROUTE_TILE, N)
    assert N % tt == 0
    tokspec = pl.BlockSpec((TOP_K, tt), lambda i: (0, i))
    cspec = pl.BlockSpec((E, V7X_LANES), lambda i: (0, 0))
    return pl.pallas_call(
        _route_body,
        grid=(N // tt,),
        in_specs=[pl.BlockSpec((E, tt), lambda i: (0, i)), cspec],
        out_specs=[tokspec, tokspec, tokspec, cspec],
        out_shape=[jax.ShapeDtypeStruct((TOP_K, N), i32), jax.ShapeDtypeStruct((TOP_K, N), f32),
                   jax.ShapeDtypeStruct((TOP_K, N), i32), jax.ShapeDtypeStruct((E, V7X_LANES), f32)],
        scratch_shapes=[pltpu.VMEM((E, V7X_LANES), f32)],
        compiler_params=_cparams(("arbitrary",)),
        name="moe_route",
    )(logits_t, count0)


def _dest_body(idx_ref, pos_ref, start_ref, dest_ref):
    E = start_ref.shape[0]
    tt = idx_ref.shape[1]
    e_iota = lax.broadcasted_iota(i32, (E, tt), 0)
    start = start_ref[:, 0:1]
    for r in range(TOP_K):
        base = jnp.sum(jnp.where(e_iota == idx_ref[r:r + 1, :], start, 0.0), axis=0, keepdims=True)
        dest_ref[r:r + 1, :] = base.astype(i32) + pos_ref[r:r + 1, :]


def _dest(idx, pos, pad_start):
    N = idx.shape[1]
    E = pad_start.shape[0]
    tt = min(ROUTE_TILE, N)
    tokspec = pl.BlockSpec((TOP_K, tt), lambda i: (0, i))
    start = jnp.broadcast_to(pad_start.astype(f32)[:, None], (E, V7X_LANES))
    return pl.pallas_call(
        _dest_body,
        grid=(N // tt,),
        in_specs=[tokspec, tokspec, pl.BlockSpec((E, V7X_LANES), lambda i: (0, 0))],
        out_specs=tokspec,
        out_shape=jax.ShapeDtypeStruct((TOP_K, N), i32),
        compiler_params=_cparams(("parallel",)),
        name="moe_dest",
    )(idx, pos, start)


ZERO_ROWS = 256
DMA_PRIORITIES = 2
DMA_LOOP_UNROLL = 8


def _scatter_body(zero_fill, block_rows, zoff_ref, zflag_ref, dest_ref, h_ref, *rest):
    if zero_fill:
        xs_ref, zeros_ref, sem = rest
    else:
        _, xs_ref, zeros_ref, sem = rest
    i = pl.program_id(0)
    tt = dest_ref.shape[1]

    def packed(ref, row, n):
        return ref.at[pl.ds(pl.multiple_of(row * PACK_ROWS, PACK_ROWS), n * PACK_ROWS)]

    def zero_copy(e, piece):
        return pltpu.make_async_copy(
            zeros_ref, packed(xs_ref, zoff_ref[e] + piece * ZERO_ROWS, ZERO_ROWS), sem.at[0])

    if zero_fill:
        @pl.when(i == 0)
        def _():
            zeros_ref[...] = jnp.zeros_like(zeros_ref)
            n_e = zoff_ref.shape[0]
            for phase in ("start", "wait"):
                def per_expert(e, carry, phase=phase):
                    @pl.when(zflag_ref[e] > 0)
                    def _():
                        for piece in range(block_rows // ZERO_ROWS):
                            cp = zero_copy(e, piece)
                            cp.start() if phase == "start" else cp.wait()
                    return carry
                lax.fori_loop(0, n_e, per_expert, 0)

    def row_copy(t, r):
        return pltpu.make_async_copy(packed(h_ref, t, 1), packed(xs_ref, dest_ref[r, t], 1), sem.at[1])

    def start(t, carry):
        for r in range(TOP_K):
            row_copy(t, r).start(priority=r % DMA_PRIORITIES)
        return carry

    def wait(t, carry):
        for r in range(TOP_K):
            row_copy(t, r).wait()
        return carry

    lax.fori_loop(0, tt, start, 0, unroll=DMA_LOOP_UNROLL)
    lax.fori_loop(0, tt, wait, 0, unroll=DMA_LOOP_UNROLL)


def _scatter(h2, dest, zoff, zflag, n_rows, block_rows, xs_prev):
    N = h2.shape[0] // PACK_ROWS
    tt = min(ROUTE_TILE, N)
    zero_fill = xs_prev is None
    any_spec = pl.BlockSpec(memory_space=pl.ANY)
    in_specs = [pl.BlockSpec((TOP_K, tt), lambda i, *_: (0, i), memory_space=pltpu.SMEM),
                pl.BlockSpec((tt * PACK_ROWS, V7X_LANES), lambda i, *_: (i, 0))]
    args = [dest, h2]
    aliases = {}
    if not zero_fill:
        in_specs.append(any_spec)
        args.append(xs_prev)
        aliases = {4: 0}
    return pl.pallas_call(
        functools.partial(_scatter_body, zero_fill, block_rows),
        grid_spec=pltpu.PrefetchScalarGridSpec(
            num_scalar_prefetch=2, grid=(N // tt,), in_specs=in_specs, out_specs=any_spec,
            scratch_shapes=[pltpu.VMEM((ZERO_ROWS * PACK_ROWS, V7X_LANES), u32),
                            pltpu.SemaphoreType.DMA((2,))]),
        out_shape=jax.ShapeDtypeStruct((n_rows * PACK_ROWS, V7X_LANES), u32),
        input_output_aliases=aliases,
        compiler_params=_cparams(("arbitrary",)),
        name="moe_scatter",
    )(zoff, zflag, *args)


def _expert_body(nf, bexp_ref, nused_ref, x_ref, wg_ref, wl_ref, bgl_ref, bll_ref, wd_ref, bd_ref,
                 o_ref, xb_ref, acc_ref):
    b = pl.program_id(0)
    j = pl.program_id(1)
    active = b < nused_ref[0]
    tm = xb_ref.shape[0]

    @pl.when(j == 0)
    def _():
        for c in range(PACK_ROWS):
            lo, hi = _unpack_chunk(x_ref, c, tm)
            xb_ref[:, c * PACK_SPAN:c * PACK_SPAN + V7X_LANES] = lo.astype(bf16)
            xb_ref[:, c * PACK_SPAN + V7X_LANES:(c + 1) * PACK_SPAN] = hi.astype(bf16)
        acc_ref[...] = jnp.where(active, jnp.broadcast_to(bd_ref[...], acc_ref.shape), 0.0)

    @pl.when(active)
    def _():
        xb = xb_ref[...]
        glu = jnp.dot(xb, wg_ref[...].astype(bf16), preferred_element_type=f32) + bgl_ref[...]
        lin = jnp.dot(xb, wl_ref[...].astype(bf16), preferred_element_type=f32) + bll_ref[...]
        glu = jnp.minimum(glu, SWIGLU_LIMIT)
        lin = jnp.clip(lin, -SWIGLU_LIMIT, SWIGLU_LIMIT)
        act = glu * jax.nn.sigmoid(SWIGLU_ALPHA * glu) * (lin + 1.0)
        acc_ref[...] += jnp.dot(act.astype(bf16), wd_ref[...].astype(bf16), preferred_element_type=f32)

    @pl.when(j == nf - 1)
    def _():
        _pack_rows(acc_ref[...], o_ref)


def _experts(xs, bexp, nused, w_gate_up, b_gate_up, w_down, b_down, tm, tf):
    n_rows = xs.shape[0] // PACK_ROWS
    E, D, F2 = w_gate_up.shape
    F = F2 // 2
    assert n_rows % tm == 0 and F % tf == 0
    nf = F // tf
    nb = n_rows // tm

    def blk(b, nused):
        return jnp.minimum(b, nused[0] - 1)

    def ftile(b, j, nused):
        return jnp.where(b < nused[0], j, nf - 1)

    b_gu = b_gate_up.reshape(E, 1, F2)
    b_d = b_down.reshape(E, 1, D)
    return pl.pallas_call(
        functools.partial(_expert_body, nf),
        grid_spec=pltpu.PrefetchScalarGridSpec(
            num_scalar_prefetch=2, grid=(nb, nf),
            in_specs=[
                pl.BlockSpec((tm * PACK_ROWS, V7X_LANES), lambda b, j, be, nu: (blk(b, nu), 0)),
                pl.BlockSpec((None, D, tf), lambda b, j, be, nu: (be[blk(b, nu)], 0, ftile(b, j, nu))),
                pl.BlockSpec((None, D, tf), lambda b, j, be, nu: (be[blk(b, nu)], 0, nf + ftile(b, j, nu))),
                pl.BlockSpec((None, 1, tf), lambda b, j, be, nu: (be[blk(b, nu)], 0, ftile(b, j, nu))),
                pl.BlockSpec((None, 1, tf), lambda b, j, be, nu: (be[blk(b, nu)], 0, nf + ftile(b, j, nu))),
                pl.BlockSpec((None, tf, D), lambda b, j, be, nu: (be[blk(b, nu)], ftile(b, j, nu), 0)),
                pl.BlockSpec((None, 1, D), lambda b, j, be, nu: (be[blk(b, nu)], 0, 0)),
            ],
            out_specs=pl.BlockSpec((tm * PACK_ROWS, V7X_LANES), lambda b, j, be, nu: (b, 0)),
            scratch_shapes=[pltpu.VMEM((tm, D), bf16), pltpu.VMEM((tm, D), f32)]),
        out_shape=jax.ShapeDtypeStruct((n_rows * PACK_ROWS, V7X_LANES), u32),
        compiler_params=_cparams(("arbitrary", "arbitrary")),
        name="moe_experts",
    )(bexp, nused, xs, w_gate_up, w_gate_up, b_gu, b_gu, w_down, b_d)


COMBINE_TILE = 256


def _combine_body(n, dcur_ref, dnext_ref, x1_ref, gate_ref, gf_ref, ys_ref, y_ref, buf_ref, sem):
    i = pl.program_id(0)
    tt = x1_ref.shape[0]

    def packed(row):
        return pl.ds(pl.multiple_of(row * PACK_ROWS, PACK_ROWS), PACK_ROWS)

    def row_copy(d_ref, slot, t, r):
        return pltpu.make_async_copy(ys_ref.at[packed(d_ref[r, t])], buf_ref.at[slot, r, packed(t)],
                                     sem.at[slot])

    def issue(d_ref, slot):
        def body(t, carry):
            for r in range(TOP_K):
                row_copy(d_ref, slot, t, r).start(priority=r % DMA_PRIORITIES)
            return carry
        lax.fori_loop(0, tt, body, 0, unroll=DMA_LOOP_UNROLL)

    @pl.when(i == 0)
    def _():
        issue(dcur_ref, 0)

    @pl.when(i + 1 < n)
    def _():
        issue(dnext_ref, (i + 1) % 2)

    slot = i % 2

    def wait(t, carry):
        for r in range(TOP_K):
            row_copy(dcur_ref, slot, t, r).wait()
        return carry
    lax.fori_loop(0, tt, wait, 0, unroll=DMA_LOOP_UNROLL)

    gates = gate_ref[...]
    parts = []
    for c in range(PACK_ROWS):
        lo_sum = x1_ref[:, c * PACK_SPAN:c * PACK_SPAN + V7X_LANES]
        hi_sum = x1_ref[:, c * PACK_SPAN + V7X_LANES:(c + 1) * PACK_SPAN]
        for r in range(TOP_K):
            lo, hi = _unpack_chunk(buf_ref.at[slot, r], c, tt)
            lo_sum = lo_sum + gates[:, r:r + 1] * lo
            hi_sum = hi_sum + gates[:, r:r + 1] * hi
        parts += [lo_sum, hi_sum]
    x = jnp.concatenate(parts, axis=1)
    ms = jnp.mean(x * x, axis=-1, keepdims=True)
    y_ref[...] = x * lax.rsqrt(ms + RMS_EPS) * gf_ref[...]


def _combine(x1, gates_tm, dest, ys, g_final):
    N, D = x1.shape
    tt = min(COMBINE_TILE, N)
    n = N // tt
    smem = lambda f: pl.BlockSpec((TOP_K, tt), f, memory_space=pltpu.SMEM)
    return pl.pallas_call(
        functools.partial(_combine_body, n),
        grid=(n,),
        in_specs=[smem(lambda i: (0, i)), smem(lambda i: (0, jnp.minimum(i + 1, n - 1))),
                  pl.BlockSpec((tt, D), lambda i: (i, 0)), pl.BlockSpec((tt, TOP_K), lambda i: (i, 0)),
                  pl.BlockSpec((1, D), lambda i: (0, 0)), pl.BlockSpec(memory_space=pl.ANY)],
        out_specs=pl.BlockSpec((tt, D), lambda i: (i, 0)),
        out_shape=jax.ShapeDtypeStruct((N, D), f32),
        scratch_shapes=[pltpu.VMEM((2, TOP_K, tt * PACK_ROWS, V7X_LANES), u32),
                        pltpu.SemaphoreType.DMA((2,))],
        compiler_params=_cparams(("arbitrary",)),
        name="moe_combine",
    )(dest, dest, x1, gates_tm, g_final.reshape(1, D), ys)


def _moe(h2_groups, logit_groups, x1_groups, ew, g_final, tm, tf):
    w_gate_up, b_gate_up, w_down, b_down = ew
    E = w_gate_up.shape[0]
    routed = []
    count = jnp.zeros((E, V7X_LANES), f32)
    for lg in logit_groups:
        idx, gate, pos, count = _route(lg, count)
        routed.append((idx, gate, pos))
    total = sum(lg.shape[1] for lg in logit_groups) * TOP_K
    n_blocks = -(-(total + E * (tm - 1)) // tm)
    n_rows = n_blocks * tm
    counts = count[:, 0].astype(i32)
    padded = (counts + tm - 1) // tm * tm
    pad_end = jnp.cumsum(padded)
    pad_start = pad_end - padded
    n_used = pad_end[-1] // tm
    blocks = jnp.minimum(jnp.arange(n_blocks, dtype=i32), n_used - 1)
    bexp = jnp.minimum(jnp.sum(blocks[:, None] * tm >= pad_end[None, :], axis=1), E - 1).astype(i32)
    all_blocks = jnp.arange(n_blocks, dtype=i32)
    zoff = jnp.concatenate([jnp.maximum(pad_end - tm, 0).astype(i32), all_blocks * tm])
    zflag = jnp.concatenate([counts > 0, all_blocks >= n_used]).astype(i32)
    xs = None
    dests = []
    for h2, (idx, gate, pos) in zip(h2_groups, routed):
        dest = _dest(idx, pos, pad_start)
        dests.append(dest)
        xs = _scatter(h2, dest, zoff, zflag, n_rows, tm, xs)
    ys = _experts(xs, bexp, n_used.reshape(1).astype(i32), w_gate_up, b_gate_up, w_down, b_down, tm, tf)
    return [_combine(x1, gate.T, dest, ys, g_final)
            for x1, (idx, gate, pos), dest in zip(x1_groups, routed, dests)]


PROJ_TILE_M = 512
PROJ_TILE_N = 1920
MERGE_TILE_M = 256
EXPERT_TILE_M = 1024
EXPERT_TILE_F = 256


def _mixers(x, state, norm_g, w_main, w_u, rw, sw, mw):
    B, T, D = x.shape
    W = w_u.shape[1]
    shift_w = 3 * W + W_LORA + A_LORA + G_LORA
    G, S = sw[0].shape[1] // 64, 64
    if state is None:
        s_rwkv0 = jnp.zeros((B, W // HEAD, HEAD, HEAD), f32)
        s5_re0 = jnp.zeros((B, W // 16, 64), f32)
        s5_im0 = jnp.zeros((B, W // 16, 64), f32)
        shift0 = jnp.zeros((B, shift_w), f32)
    else:
        s_rwkv0, s5_re0, s5_im0, shift0 = state
    per_batch = T % PROJ_TILE_M == 0
    xf = x if per_batch else x.reshape(1, B * T, D)
    tm = PROJ_TILE_M if per_batch else min(PROJ_TILE_M, B * T)
    proj = _norm_proj(xf, norm_g, w_main, tm, PROJ_TILE_N, False).reshape(B, T, MAIN_COLS)
    if per_batch:
        u_tb = _norm_proj(xf, norm_g, w_u, tm, W, True)
    else:
        u_tb = jnp.swapaxes(_norm_proj(xf, norm_g, w_u, tm, W, False).reshape(B, T, W), 0, 1)
    o_a, s_rwkv = _rwkv_stage(proj, shift0, s_rwkv0, rw)
    o_b, s5_re, s5_im = _s5_stage(u_tb, s5_re0, s5_im0, sw)
    if per_batch:
        x1, h2, logits_t = _merge_stage(x, o_a, o_b, True, proj, mw, MERGE_TILE_M)
    else:
        flat = lambda t: t.reshape(1, B * T, t.shape[-1])
        x1, h2, logits_t = _merge_stage(flat(x), flat(o_a), flat(jnp.swapaxes(o_b, 0, 1)), False,
                                        flat(proj), mw, min(MERGE_TILE_M, B * T))
    last = proj[:, -1, :]
    n_lora = W_LORA + A_LORA + G_LORA
    shift = jnp.concatenate([last[:, COL_R * W:(COL_R + 3) * W],
                             last[:, COL_LORA * LORA_PAD:COL_LORA * LORA_PAD + n_lora]], axis=1)
    return (x1.reshape(B * T, D), h2, logits_t), (s_rwkv, s5_re, s5_im, shift)


def kernel(x_prompt, x_sample, state_rwkv, state_s5_re, state_s5_im, state_shift, norm_mix_g, w_in, mu_shift, w0, w2, a0, a2, g2, k_k, k_a, r_k, lnx_g, lnx_b, lam_re, lam_im, log_dt, b_re, b_im, c_re, c_im, d_skip, w_glu, b_glu, w_up_a, w_up_b, w_out, norm_ffn_g, w_router, b_router, w_gate_up, b_gate_up, w_down, b_down, norm_final_g):
    assert w_in.shape[0] == 1, "single-layer trunk"
    D = x_prompt.shape[-1]
    W = w0.shape[-1]
    n_lora = W_LORA + A_LORA + G_LORA
    shift_w = 3 * W + n_lora
    w = w_in[0]
    ga0 = shift_w + W
    w_main = jnp.concatenate(
        [w[:, ga0:ga0 + 2 * D], w[:, :3 * W], w[:, 3 * W:shift_w],
         jnp.zeros((D, LORA_PAD - n_lora), w.dtype)], axis=1).astype(bf16)
    assert w_main.shape[1] == MAIN_COLS
    w_u = w[:, shift_w:shift_w + W].astype(bf16)
    rw = _rwkv_weights(mu_shift[0], w0[0], w2[0], a0[0], a2[0], g2[0], k_k[0], k_a[0], r_k[0],
                       lnx_g[0], lnx_b[0])
    sw = _s5_weights(lam_re[0], lam_im[0], log_dt[0], b_re[0], b_im[0], c_re[0], c_im[0],
                     d_skip[0], w_glu[0], b_glu[0])
    E = w_router.shape[-1]
    mw = (w_up_a[0].astype(bf16), w_up_b[0].astype(bf16), w_out[0].astype(bf16),
          norm_ffn_g[0].reshape(1, D).astype(f32), w_router[0].T.astype(bf16),
          b_router[0].reshape(E, 1).astype(f32))
    ew = (w_gate_up[0], b_gate_up[0], w_down[0], b_down[0])

    state_s = (state_rwkv[0], state_s5_re[0], state_s5_im[0], state_shift[0])
    tok_p, st_p = _mixers(x_prompt, None, norm_mix_g[0], w_main, w_u, rw, sw, mw)
    tok_s, st_s = _mixers(x_sample, state_s, norm_mix_g[0], w_main, w_u, rw, sw, mw)
    y_p, y_s = _moe([tok_p[1], tok_s[1]], [tok_p[2], tok_s[2]], [tok_p[0], tok_s[0]], ew, norm_final_g,
                    EXPERT_TILE_M, EXPERT_TILE_F)
    lead = lambda t: t[None]
    return (y_p.reshape(x_prompt.shape), y_s.reshape(x_sample.shape),
            lead(st_p[0]), lead(st_p[1]), lead(st_p[2]), lead(st_p[3]),
            lead(st_s[0]), lead(st_s[1]), lead(st_s[2]), lead(st_s[3]))
```

```python
import functools
import math

import jax
import jax.numpy as jnp
from jax import lax
from jax.experimental import pallas as pl
from jax.experimental.pallas import tpu as pltpu

f32 = jnp.float32
bf16 = jnp.bfloat16
i32 = jnp.int32
u32 = jnp.uint32

V7X_LANES = 128
V7X_SUBLANES = 8
V7X_VMEM_BYTES = 64 * 1024 * 1024
VMEM_LIMIT = 56 * 1024 * 1024

RMS_EPS = 1e-5
GN_EPS = 64e-5
HEAD = 64
HEAD_SHIFT = 6
HEADS_PER_GROUP = 4
GROUP = HEAD * HEADS_PER_GROUP
CHUNK = 64
W_LORA, A_LORA, G_LORA = 64, 64, 160
LORA_PAD = 512
TOP_K = 4
SWIGLU_LIMIT = 7.0
SWIGLU_ALPHA = 1.702


def _cparams(sem):
    return pltpu.CompilerParams(dimension_semantics=sem, vmem_limit_bytes=VMEM_LIMIT)


def _dot(a, b):
    return jnp.dot(a.astype(bf16), b.astype(bf16), preferred_element_type=f32)


def _dot_nt(a, b):
    return lax.dot_general(a.astype(bf16), b.astype(bf16), (((1,), (1,)), ((), ())),
                           preferred_element_type=f32)


def _dot_tn(a, b):
    return lax.dot_general(a.astype(bf16), b.astype(bf16), (((0,), (0,)), ((), ())),
                           preferred_element_type=f32)


def _split(x):
    hi = x.astype(bf16)
    lo = (x - hi.astype(f32)).astype(bf16)
    return hi, lo


def _dot_exact_rhs(a, b_exact):
    hi, lo = _split(a)
    return (jnp.dot(hi, b_exact, preferred_element_type=f32)
            + jnp.dot(lo, b_exact, preferred_element_type=f32))


def _dot_exact_lhs(a_exact, b):
    hi, lo = _split(b)
    return (jnp.dot(a_exact, hi, preferred_element_type=f32)
            + jnp.dot(a_exact, lo, preferred_element_type=f32))


def _norm_proj_body(x_ref, g_ref, w_ref, o_ref, h_ref):
    @pl.when(pl.program_id(2) == 0)
    def _():
        x = x_ref[...]
        ms = jnp.mean(x * x, axis=-1, keepdims=True)
        h_ref[...] = (x * lax.rsqrt(ms + RMS_EPS) * g_ref[...]).astype(bf16)

    o_ref[...] = jnp.dot(h_ref[...], w_ref[...], preferred_element_type=f32).astype(o_ref.dtype)


def _norm_proj(x, gain, w, tm, tn, time_major_out, out_dtype):
    B, T, D = x.shape
    N = w.shape[1]
    assert T % tm == 0 and N % tn == 0
    if time_major_out:
        out_shape = jax.ShapeDtypeStruct((T, B * N), out_dtype)
        out_spec = pl.BlockSpec((tm, tn), lambda b, i, n: (i, b * (N // tn) + n))
    else:
        out_shape = jax.ShapeDtypeStruct((B, T, N), out_dtype)
        out_spec = pl.BlockSpec((None, tm, tn), lambda b, i, n: (b, i, n))
    out = pl.pallas_call(
        _norm_proj_body,
        grid=(B, T // tm, N // tn),
        in_specs=[pl.BlockSpec((None, tm, D), lambda b, i, n: (b, i, 0)),
                  pl.BlockSpec((1, D), lambda b, i, n: (0, 0)),
                  pl.BlockSpec((D, tn), lambda b, i, n: (0, n))],
        out_specs=out_spec,
        out_shape=out_shape,
        scratch_shapes=[pltpu.VMEM((tm, D), bf16)],
        compiler_params=_cparams(("parallel", "parallel", "arbitrary")),
        name="norm_proj",
    )(x, gain.reshape(1, D), w)
    return out.reshape(T, B, N) if time_major_out else out


PV_MU_R, PV_MU_K, PV_MU_V, PV_W0, PV_A0, PV_KK, PV_KA, PV_RK, PV_LNG, PV_LNB = range(10)
PV_ROWS = 16
PREV_ROWS = 16
RWKV_CHUNKS_PER_STEP = 2


def _rwkv_body(nc, r_ref, k_ref, v_ref, l_ref, pr_ref, pk_ref, pv_ref, plo_ref,
               lr_ref, lk_ref, lv_ref, llo_ref, pvec_ref, mul_ref, wl_ref, s0_ref,
               o_ref, so_ref, s_ref):
    c = pl.program_id(1)
    C = CHUNK
    R, W = r_ref.shape
    n_chunks = R // C
    n_groups = W // GROUP

    @pl.when(c == 0)
    def _():
        s_ref[...] = s0_ref[...]

    row = lax.broadcasted_iota(i32, (R, 1), 0)

    def shifted(x_ref, prev_ref, last_ref, mu):
        x = x_ref[...].astype(f32)
        prev_row = jnp.where(c > 0, prev_ref[PREV_ROWS - 1:PREV_ROWS, :].astype(f32), last_ref[...])
        prev = jnp.where(row == 0, prev_row, pltpu.roll(x, shift=1, axis=0))
        return x + (prev - x) * mu

    pvec = pvec_ref[...]
    prm = lambda i: pvec[i:i + 1, :]
    r = shifted(r_ref, pr_ref, lr_ref, prm(PV_MU_R))
    k = shifted(k_ref, pk_ref, lk_ref, prm(PV_MU_K))
    v = shifted(v_ref, pv_ref, lv_ref, prm(PV_MU_V))
    xl = shifted(l_ref, plo_ref, llo_ref, mul_ref[...])

    lane_l = lax.broadcasted_iota(i32, (1, xl.shape[1]), 1)
    act = jnp.where(lane_l < W_LORA, jnp.tanh(xl),
                    jnp.where(lane_l < W_LORA + A_LORA, xl, jax.nn.sigmoid(xl)))
    lo = jnp.dot(act.astype(bf16), wl_ref[...], preferred_element_type=f32)
    z = -(prm(PV_W0) + lo[:, :W])
    w = -(jnp.maximum(z, 0.0) + jnp.log1p(jnp.exp(-jnp.abs(z)))) - 0.5
    logd = -jnp.exp(w)
    a = jax.nn.sigmoid(prm(PV_A0) + lo[:, W:2 * W])
    g = lo[:, 2 * W:]

    gi = lax.broadcasted_iota(i32, (GROUP, GROUP), 0)
    gj = lax.broadcasted_iota(i32, (GROUP, GROUP), 1)
    same_head = (gi >> HEAD_SHIFT) == (gj >> HEAD_SHIFT)
    head_ones = jnp.where(same_head, 1.0, 0.0).astype(bf16)
    ti, tj = gi & (HEAD - 1), gj & (HEAD - 1)
    strict = same_head & (ti > tj)
    incl = same_head & (ti >= tj)
    ci = lax.broadcasted_iota(i32, (R, R), 0)
    cj = lax.broadcasted_iota(i32, (R, R), 1)
    tri = jnp.where((ci >= cj) & ((ci >> HEAD_SHIFT) == (cj >> HEAD_SHIFT)), 1.0, 0.0).astype(bf16)
    lane_head = lax.broadcasted_iota(i32, (1, GROUP), 1) >> HEAD_SHIFT
    eye_quad = jnp.where(lax.broadcasted_iota(i32, (C, GROUP), 0)
                         == (lax.broadcasted_iota(i32, (C, GROUP), 1) & (HEAD - 1)), 1.0, 0.0)

    def head_sum(x, exact):
        parts = []
        for gq in range(n_groups):
            xg = x[:, gq * GROUP:(gq + 1) * GROUP]
            parts.append(_dot_exact_rhs(xg, head_ones) if exact
                         else jnp.dot(xg.astype(bf16), head_ones, preferred_element_type=f32))
        return jnp.concatenate(parts, axis=1)

    kk = k * prm(PV_KK)
    kk = kk / jnp.maximum(jnp.sqrt(head_sum(kk * kk, True)), 1e-12)
    k2 = k * (1.0 + (a - 1.0) * prm(PV_KA))
    cl = _dot_exact_lhs(tri, logd)
    p_incl = jnp.exp(cl)
    p_excl = jnp.exp(cl - logd)
    p_inv = jnp.exp(-cl)
    rt = r * p_incl
    at = -kk * p_excl
    bt = kk * a * p_inv
    kt = k2 * p_inv
    bonus = head_sum(r * k2 * prm(PV_RK), False) * v

    def stack(x):
        return jnp.concatenate([jnp.where(lane_head == h, x, 0.0) for h in range(HEADS_PER_GROUP)], axis=0)

    def tile(x):
        return jnp.concatenate([x] * HEADS_PER_GROUP, axis=0)

    def fold(x):
        out = x[0:C]
        for h in range(1, HEADS_PER_GROUP):
            out = out + x[h * C:(h + 1) * C]
        return out

    def block(x):
        return jnp.where(same_head, tile(x), 0.0)

    groups = range(n_groups)
    each = lambda f, *lists: [f(*xs) for xs in zip(*lists)]
    lanes = [slice(gq * GROUP, (gq + 1) * GROUP) for gq in groups]
    tiles = [(slice(ch * C, (ch + 1) * C), sl) for ch in range(n_chunks) for sl in lanes]
    ag, rg, bg, kg, vg = ([x[rows, sl] for rows, sl in tiles] for x in (at, rt, bt, kt, v))
    sa, sr, sv = each(stack, ag), each(stack, rg), each(stack, vg)
    tb, tk = each(tile, bg), each(tile, kg)
    ab = each(lambda x, y: jnp.where(strict, _dot_nt(x, y), 0.0), sa, tb)
    ak = each(lambda x, y: jnp.where(strict, _dot_nt(x, y), 0.0), sa, tk)
    rb = each(lambda x, y: jnp.where(incl, _dot_nt(x, y), 0.0), sr, tb)
    rk = each(lambda x, y: jnp.where(incl, _dot_nt(x, y), 0.0), sr, tk)
    q4, qb = each(fold, ab), ab
    t4 = each(lambda q: eye_quad + q, q4)
    for _ in range(max(1, (C - 1).bit_length() - 1)):
        q4 = each(_dot, q4, qb)
        qb = each(block, q4)
        t4 = each(lambda t, q: t + _dot(t, q), t4, qb)
    w4 = each(_dot, t4, sa)
    akv = each(lambda x, y: _dot(fold(x), y), ak, sv)
    uv = each(lambda t, x: _dot(t, stack(x)), t4, akv)
    orv = each(lambda x, y: _dot(fold(x), y), rk, sv)
    state = [s_ref[gq] for gq in groups]
    o_rows = []
    for ch in range(n_chunks):
        of = lambda xs: xs[ch * n_groups:(ch + 1) * n_groups]
        u4 = each(lambda w_, s_, uv_: _dot_nt(w_, s_) + uv_, of(w4), state, of(uv))
        rs = each(_dot_nt, of(rg), state)
        upd = each(lambda u_, v_, b_, k_: _dot_tn(jnp.concatenate([u_, v_], axis=0),
                                                  jnp.concatenate([b_, k_], axis=0)),
                   u4, of(vg), of(bg), of(kg))
        o4 = each(lambda rs_, rb_, u_, orv_: rs_ + _dot(fold(rb_), stack(u_)) + orv_,
                  rs, of(rb), u4, of(orv))
        p_end = p_incl[(ch + 1) * C - 1:(ch + 1) * C, :]
        state = [(s + jnp.where(same_head, d, 0.0)) * p_end[:, sl] for s, d, sl in zip(state, upd, lanes)]
        o_rows.append(jnp.concatenate(o4, axis=1))
    for gq in groups:
        s_ref[gq] = state[gq]
    o = jnp.concatenate(o_rows, axis=0)

    inv_n = 1.0 / HEAD
    mean = head_sum(o, False) * inv_n
    d = o - mean
    var = head_sum(d * d, False) * inv_n
    on = d * lax.rsqrt(var + GN_EPS)
    o_ref[...] = (on * prm(PV_LNG) + prm(PV_LNB) + bonus) * g

    @pl.when(c == nc - 1)
    def _():
        so_ref[...] = s_ref[...]


def _rwkv(proj, col_r, col_lora, last_rkv, last_lora, pvec, mu_lora, w_lora, s0_blocks):
    B, T, _ = proj.shape
    W = pvec.shape[1]
    C = CHUNK
    assert T % C == 0 and C == HEAD and HEAD == 1 << HEAD_SHIFT
    n_groups = W // GROUP
    R = C * RWKV_CHUNKS_PER_STEP if T % (C * RWKV_CHUNKS_PER_STEP) == 0 else C
    rb = R // PREV_ROWS

    def cur(col, width):
        return pl.BlockSpec((None, R, width), lambda b, c, col=col: (b, c, col))

    def prev(col, width):
        return pl.BlockSpec((None, PREV_ROWS, width),
                            lambda b, c, col=col: (b, jnp.maximum(c * rb - 1, 0), col))

    def last(col, width):
        return pl.BlockSpec((None, 1, width), lambda b, c, col=col: (b, 0, col))

    const = lambda shape: pl.BlockSpec(shape, lambda b, c: (0,) * len(shape))
    state_spec = pl.BlockSpec((None, n_groups, GROUP, GROUP), lambda b, c: (b, 0, 0, 0))
    return pl.pallas_call(
        functools.partial(_rwkv_body, T // R),
        grid=(B, T // R),
        in_specs=[cur(col_r, W), cur(col_r + 1, W), cur(col_r + 2, W), cur(col_lora, LORA_PAD),
                  prev(col_r, W), prev(col_r + 1, W), prev(col_r + 2, W), prev(col_lora, LORA_PAD),
                  last(0, W), last(1, W), last(2, W), last(0, LORA_PAD),
                  const((PV_ROWS, W)), const((1, LORA_PAD)), const((LORA_PAD, 3 * W)), state_spec],
        out_specs=[pl.BlockSpec((None, R, W), lambda b, c: (b, c, 0)), state_spec],
        out_shape=[jax.ShapeDtypeStruct((B, T, W), f32),
                   jax.ShapeDtypeStruct((B, n_groups, GROUP, GROUP), f32)],
        scratch_shapes=[pltpu.VMEM((n_groups, GROUP, GROUP), f32)],
        compiler_params=_cparams(("parallel", "arbitrary")),
        name="rwkv7_chunked",
    )(proj, proj, proj, proj, proj, proj, proj, proj,
      last_rkv, last_rkv, last_rkv, last_lora, pvec, mu_lora, w_lora, s0_blocks)


COL_GA, COL_GB = 0, 1
COL_R = 4
COL_LORA = 14
MAIN_COLS = 7680


def _rwkv_weights(mu_shift, w0, w2, a0, a2, g2, k_k, k_a, r_k, lnx_g, lnx_b):
    W = w0.shape[0]
    rows = [mu_shift[:W], mu_shift[W:2 * W], mu_shift[2 * W:3 * W], w0, a0, k_k, k_a,
            r_k.reshape(W), lnx_g, lnx_b]
    pvec = jnp.zeros((PV_ROWS, W), f32).at[:len(rows)].set(jnp.stack(rows).astype(f32))
    n_lora = W_LORA + A_LORA + G_LORA
    mu_lora = jnp.zeros((1, LORA_PAD), f32).at[0, :n_lora].set(mu_shift[3 * W:])
    w_lora = jnp.zeros((LORA_PAD, 3 * W), f32)
    w_lora = w_lora.at[:W_LORA, :W].set(w2)
    w_lora = w_lora.at[W_LORA:W_LORA + A_LORA, W:2 * W].set(a2)
    w_lora = w_lora.at[W_LORA + A_LORA:n_lora, 2 * W:].set(g2)
    return pvec, mu_lora, w_lora.astype(bf16)


def _rwkv_stage(proj, shift0, s0, rw):
    pvec, mu_lora, w_lora = rw
    B = proj.shape[0]
    W = pvec.shape[1]
    H = W // HEAD
    n_groups = H // HEADS_PER_GROUP
    last_rkv = shift0[:, None, :3 * W].astype(f32)
    last_lora = jnp.zeros((B, 1, LORA_PAD), f32).at[:, 0, :shift0.shape[1] - 3 * W].set(shift0[:, 3 * W:])
    eye = jnp.eye(HEADS_PER_GROUP, dtype=f32)
    s0_blocks = jnp.einsum('bghij,hk->bghikj', s0.reshape(B, n_groups, HEADS_PER_GROUP, HEAD, HEAD), eye)
    s0_blocks = s0_blocks.reshape(B, n_groups, GROUP, GROUP)
    o, s_blocks = _rwkv(proj, COL_R, COL_LORA, last_rkv, last_lora, pvec, mu_lora, w_lora, s0_blocks)
    s5 = s_blocks.reshape(B, n_groups, HEADS_PER_GROUP, HEAD, HEADS_PER_GROUP, HEAD)
    s_new = jnp.einsum('bghikj,hk->bghij', s5, eye).reshape(B, H, HEAD, HEAD)
    return o, s_new


S5_BLOCK_IN = 128
S5_BLOCK_STATE = 512


def _s5_prep_body(lr_ref, li_ref, ldt_ref, br_ref, bi_ref, lbr_ref, lbi_ref, bbr_ref, bbi_ref):
    lr, li = lr_ref[...], li_ref[...]
    dt = jnp.exp(ldt_ref[...])
    mag = jnp.exp(lr * dt)
    er, ei = mag * jnp.cos(li * dt), mag * jnp.sin(li * dt)
    lbr_ref[...] = er
    lbi_ref[...] = ei
    den = lr * lr + li * li
    cr = ((er - 1.0) * lr + ei * li) / den
    ci = (ei * lr - (er - 1.0) * li) / den
    br, bi = br_ref[...], bi_ref[...]
    bbr_ref[...] = cr * br - ci * bi
    bbi_ref[...] = cr * bi + ci * br


def _s5_weights(lam_re, lam_im, log_dt, b_re, b_im, c_re, c_im, d_skip, w_glu, b_glu):
    G, S = lam_re.shape
    P = b_re.shape[-1]
    n = G * S
    col = lambda t: t.reshape(n, 1).astype(f32)
    ldt = jnp.broadcast_to(log_dt[:, None], (G, S))
    shapes = [jax.ShapeDtypeStruct((n, 1), f32)] * 2 + [jax.ShapeDtypeStruct((n, P), f32)] * 2
    lbr, lbi, bbr, bbi = pl.pallas_call(_s5_prep_body, out_shape=shapes, name="s5_discretise")(
        col(lam_re), col(lam_im), col(ldt), b_re.reshape(n, P).astype(f32), b_im.reshape(n, P).astype(f32))
    gpb = S5_BLOCK_IN // P
    nb = G // gpb
    eye = jnp.eye(gpb, dtype=f32)

    def in_block(bb):
        t = bb.reshape(nb, gpb, S, P)
        return jnp.einsum('qgsp,gh->qgphs', t, eye).reshape(nb, gpb * P, gpb * S)

    def out_block(cc):
        t = cc.astype(f32).reshape(nb, gpb, P, S)
        return jnp.einsum('qgps,gh->qgshp', t, eye).reshape(nb, gpb * S, gpb * P)

    wb = jnp.concatenate([in_block(bbr), in_block(bbi)], axis=2).astype(bf16)
    wc = jnp.concatenate([out_block(c_re), -out_block(c_im)], axis=1).astype(bf16)
    W = G * P
    return (lbr.reshape(1, n), lbi.reshape(1, n), wb, wc, d_skip.reshape(1, W).astype(f32),
            w_glu.astype(bf16), b_glu.reshape(1, W).astype(f32))


def _s5_body(n_steps, u_ref, x0r_ref, x0i_ref, lbr_ref, lbi_ref, wb_ref, wc_ref, d_ref, wg_ref, bg_ref,
             o_ref, xr_out, xi_out, bur_ref, bui_ref, xr_ref, xi_ref, y_ref):
    i = pl.program_id(1)
    Tc, Bs, W = u_ref.shape
    rows = Tc * Bs
    nb = wb_ref.shape[0]
    ns = S5_BLOCK_STATE

    @pl.when(i == 0)
    def _():
        xr_ref[...] = x0r_ref[...]
        xi_ref[...] = x0i_ref[...]

    u = u_ref[...].reshape(rows, W)
    for q in range(nb):
        bu = jnp.dot(u[:, q * S5_BLOCK_IN:(q + 1) * S5_BLOCK_IN].astype(bf16), wb_ref[q],
                     preferred_element_type=f32)
        bur_ref[:, q * ns:(q + 1) * ns] = bu[:, :ns]
        bui_ref[:, q * ns:(q + 1) * ns] = bu[:, ns:]

    for q in range(nb):
        sl = slice(q * ns, (q + 1) * ns)
        lr = jnp.broadcast_to(lbr_ref[:, sl], (Bs, ns))
        li = jnp.broadcast_to(lbi_ref[:, sl], (Bs, ns))

        def step(t, carry):
            xr, xi = carry
            rs = pl.ds(pl.multiple_of(t * Bs, Bs), Bs)
            nr = lr * xr - li * xi + bur_ref[rs, sl]
            ni = lr * xi + li * xr + bui_ref[rs, sl]
            bur_ref[rs, sl] = nr
            bui_ref[rs, sl] = ni
            return nr, ni

        xr, xi = lax.fori_loop(0, Tc, step, (xr_ref[:, sl], xi_ref[:, sl]), unroll=4)
        xr_ref[:, sl] = xr
        xi_ref[:, sl] = xi
        y_ref[:, q * S5_BLOCK_IN:(q + 1) * S5_BLOCK_IN] = (
            jnp.dot(bur_ref[:, sl].astype(bf16), wc_ref[q, :ns, :], preferred_element_type=f32)
            + jnp.dot(bui_ref[:, sl].astype(bf16), wc_ref[q, ns:, :], preferred_element_type=f32))

    y = y_ref[...] + d_ref[...] * u
    y = 0.5 * y * (1.0 + jnp.tanh(math.sqrt(2.0 / math.pi) * (y + 0.044715 * (y * y * y))))
    z = jnp.dot(y.astype(bf16), wg_ref[...], preferred_element_type=f32) + bg_ref[...]
    o_ref[...] = (y * jax.nn.sigmoid(z)).reshape(Tc, Bs, W)

    @pl.when(i == n_steps - 1)
    def _():
        xr_out[...] = xr_ref[...]
        xi_out[...] = xi_ref[...]


def _s5_stage(u_tb, x0_re, x0_im, sw):
    lbr, lbi, wb, wc, d_skip, w_glu, b_glu = sw
    T, B, W = u_tb.shape
    G, S = x0_re.shape[1:]
    n = G * S
    Bs = V7X_SUBLANES
    Tc = min(64, T)
    assert B % Bs == 0 and T % Tc == 0
    rows = Tc * Bs
    const = lambda shape: pl.BlockSpec(shape, lambda j, i: (0,) * len(shape))
    state_spec = pl.BlockSpec((Bs, n), lambda j, i: (j, 0))
    o, xr, xi = pl.pallas_call(
        functools.partial(_s5_body, T // Tc),
        grid=(B // Bs, T // Tc),
        in_specs=[pl.BlockSpec((Tc, Bs, W), lambda j, i: (i, j, 0)), state_spec, state_spec,
                  const((1, n)), const((1, n)), const(wb.shape), const(wc.shape),
                  const((1, W)), const((W, W)), const((1, W))],
        out_specs=[pl.BlockSpec((Tc, Bs, W), lambda j, i: (i, j, 0)), state_spec, state_spec],
        out_shape=[jax.ShapeDtypeStruct((T, B, W), f32),
                   jax.ShapeDtypeStruct((B, n), f32), jax.ShapeDtypeStruct((B, n), f32)],
        scratch_shapes=[pltpu.VMEM((rows, n), f32), pltpu.VMEM((rows, n), f32),
                        pltpu.VMEM((Bs, n), f32), pltpu.VMEM((Bs, n), f32), pltpu.VMEM((rows, W), f32)],
        compiler_params=_cparams(("parallel", "arbitrary")),
        name="s5_scan",
    )(u_tb, x0_re.reshape(B, n).astype(f32), x0_im.reshape(B, n).astype(f32),
      lbr, lbi, wb, wc, d_skip, w_glu, b_glu)
    return o, xr.reshape(B, G, S), xi.reshape(B, G, S)


PACK_ROWS = V7X_SUBLANES
PACK_SPAN = 2 * V7X_LANES
HIGH_HALF = 0xFFFF0000


def _pack_rows(x, out_ref):
    n, D = x.shape
    assert D == PACK_ROWS * PACK_SPAN
    bits = lax.bitcast_convert_type(x.astype(bf16).astype(f32), u32)
    for c in range(PACK_ROWS):
        lo = bits[:, c * PACK_SPAN:c * PACK_SPAN + V7X_LANES]
        hi = bits[:, c * PACK_SPAN + V7X_LANES:(c + 1) * PACK_SPAN]
        out_ref[pl.ds(c, n, stride=PACK_ROWS), :] = (lo >> 16) | (hi & jnp.uint32(HIGH_HALF))


def _unpack_chunk(ref, c, n):
    words = ref[pl.ds(c, n, stride=PACK_ROWS), :]
    lo = lax.bitcast_convert_type(words << 16, f32)
    hi = lax.bitcast_convert_type(words & jnp.uint32(HIGH_HALF), f32)
    return lo, hi


def _merge_body(x_ref, oa_ref, ob_ref, ga_ref, gb_ref, wa_ref, wb_ref, wo_ref, gf_ref, wr_ref, br_ref,
                x1_ref, h2_ref, lg_ref):
    ya = jnp.dot(oa_ref[...].astype(bf16), wa_ref[...], preferred_element_type=f32)
    yb = jnp.dot(ob_ref[...].astype(bf16), wb_ref[...], preferred_element_type=f32)
    merged = (jax.nn.sigmoid(ga_ref[...].astype(f32)) * ya
              + jax.nn.sigmoid(gb_ref[...].astype(f32)) * yb)
    x1 = x_ref[...] + jnp.dot(merged.astype(bf16), wo_ref[...], preferred_element_type=f32)
    x1_ref[...] = x1
    ms = jnp.mean(x1 * x1, axis=-1, keepdims=True)
    h2 = x1 * lax.rsqrt(ms + RMS_EPS) * gf_ref[...]
    _pack_rows(h2, h2_ref)
    lg_ref[...] = _dot_nt(wr_ref[...], h2) + br_ref[...]


def _resident(shape, nargs):
    return pl.BlockSpec(shape, lambda *_: (0,) * len(shape), pipeline_mode=pl.Buffered(1))


def _merge_stage(x, o_a, o_b, ob_time_major, proj, mw, tm):
    w_up_a, w_up_b, w_out, g_ffn, w_rt, b_r = mw
    B, T, D = x.shape
    W = o_a.shape[-1]
    E = w_rt.shape[0]
    assert T % tm == 0
    nt = T // tm
    tok = lambda width, col=0: pl.BlockSpec((None, tm, width), lambda b, i, col=col: (b, i, col))
    if ob_time_major:
        o_b = o_b.reshape(T, B * W)
        ob_spec = pl.BlockSpec((tm, W), lambda b, i: (i, b))
    else:
        ob_spec = tok(W)
    return pl.pallas_call(
        _merge_body,
        grid=(B, nt),
        in_specs=[tok(D), tok(W), ob_spec, tok(D, COL_GA), tok(D, COL_GB),
                  _resident((W, D), 2), _resident((W, D), 2), _resident((D, D), 2),
                  _resident((1, D), 2), _resident((E, D), 2), _resident((E, 1), 2)],
        out_specs=[tok(D), pl.BlockSpec((tm * PACK_ROWS, V7X_LANES), lambda b, i: (b * nt + i, 0)),
                   pl.BlockSpec((E, tm), lambda b, i: (0, b * nt + i))],
        out_shape=[jax.ShapeDtypeStruct((B, T, D), f32),
                   jax.ShapeDtypeStruct((B * T * PACK_ROWS, V7X_LANES), u32),
                   jax.ShapeDtypeStruct((E, B * T), f32)],
        compiler_params=_cparams(("parallel", "parallel")),
        name="merge_outproj",
    )(x, o_a, o_b, proj, proj, w_up_a, w_up_b, w_out, g_ffn, w_rt, b_r)


ROUTE_TILE = 512


def _route_body(lg_ref, c0_ref, idx_ref, gate_ref, pos_ref, cnt_ref, carry_ref):
    i = pl.program_id(0)
    E, tt = lg_ref.shape

    @pl.when(i == 0)
    def _():
        carry_ref[...] = c0_ref[...]

    l = lg_ref[...]
    e_iota = lax.broadcasted_iota(i32, (E, tt), 0)
    vals, sels = [], []
    for r in range(TOP_K):
        m = jnp.max(l, axis=0, keepdims=True)
        idx = jnp.min(jnp.where(l == m, e_iota, E), axis=0, keepdims=True)
        sel = e_iota == idx
        l = jnp.where(sel, -jnp.inf, l)
        idx_ref[r:r + 1, :] = idx
        vals.append(m)
        sels.append(sel)
    ex = [jnp.exp(v - vals[0]) for v in vals]
    tot = ex[0] + ex[1] + ex[2] + ex[3]
    for r in range(TOP_K):
        gate_ref[r:r + 1, :] = ex[r] / tot
    member = sels[0] | sels[1] | sels[2] | sels[3]
    mb = jnp.where(member, 1.0, 0.0)
    ti = lax.broadcasted_iota(i32, (tt, tt), 0)
    tj = lax.broadcasted_iota(i32, (tt, tt), 1)
    before = (ti < tj).astype(bf16)
    rank = carry_ref[:, 0:1] + jnp.dot(mb.astype(bf16), before, preferred_element_type=f32)
    for r in range(TOP_K):
        pos_ref[r:r + 1, :] = jnp.sum(jnp.where(sels[r], rank, 0.0), axis=0, keepdims=True).astype(i32)
    carry_ref[...] = carry_ref[...] + jnp.sum(mb, axis=1, keepdims=True)
    cnt_ref[...] = carry_ref[...]


def _route(logits_t, count0):
    E, N = logits_t.shape
    tt = min(ROUTE_TILE, N)
    assert N % tt == 0
    tokspec = pl.BlockSpec((TOP_K, tt), lambda i: (0, i))
    cspec = pl.BlockSpec((E, V7X_LANES), lambda i: (0, 0))
    return pl.pallas_call(
        _route_body,
        grid=(N // tt,),
        in_specs=[pl.BlockSpec((E, tt), lambda i: (0, i)), cspec],
        out_specs=[tokspec, tokspec, tokspec, cspec],
        out_shape=[jax.ShapeDtypeStruct((TOP_K, N), i32), jax.ShapeDtypeStruct((TOP_K, N), f32),
                   jax.ShapeDtypeStruct((TOP_K, N), i32), jax.ShapeDtypeStruct((E, V7X_LANES), f32)],
        scratch_shapes=[pltpu.VMEM((E, V7X_LANES), f32)],
        compiler_params=_cparams(("arbitrary",)),
        name="moe_route",
    )(logits_t, count0)


def _dest_body(idx_ref, pos_ref, start_ref, dest_ref):
    E = start_ref.shape[0]
    tt = idx_ref.shape[1]
    e_iota = lax.broadcasted_iota(i32, (E, tt), 0)
    start = start_ref[:, 0:1]
    for r in range(TOP_K):
        base = jnp.sum(jnp.where(e_iota == idx_ref[r:r + 1, :], start, 0.0), axis=0, keepdims=True)
        dest_ref[r:r + 1, :] = base.astype(i32) + pos_ref[r:r + 1, :]


def _dest(idx, pos, pad_start):
    N = idx.shape[1]
    E = pad_start.shape[0]
    tt = min(ROUTE_TILE, N)
    tokspec = pl.BlockSpec((TOP_K, tt), lambda i: (0, i))
    start = jnp.broadcast_to(pad_start.astype(f32)[:, None], (E, V7X_LANES))
    return pl.pallas_call(
        _dest_body,
        grid=(N // tt,),
        in_specs=[tokspec, tokspec, pl.BlockSpec((E, V7X_LANES), lambda i: (0, 0))],
        out_specs=tokspec,
        out_shape=jax.ShapeDtypeStruct((TOP_K, N), i32),
        compiler_params=_cparams(("parallel",)),
        name="moe_dest",
    )(idx, pos, start)


ZERO_ROWS = 256
DMA_PRIORITIES = 2
DMA_LOOP_UNROLL = 8


def _scatter_body(zero_fill, block_rows, zoff_ref, zflag_ref, dest_ref, h_ref, *rest):
    if zero_fill:
        xs_ref, zeros_ref, sem = rest
    else:
        _, xs_ref, zeros_ref, sem = rest
    i = pl.program_id(0)
    tt = dest_ref.shape[1]

    def packed(ref, row, n):
        return ref.at[pl.ds(pl.multiple_of(row * PACK_ROWS, PACK_ROWS), n * PACK_ROWS)]

    def zero_copy(e, piece):
        return pltpu.make_async_copy(
            zeros_ref, packed(xs_ref, zoff_ref[e] + piece * ZERO_ROWS, ZERO_ROWS), sem.at[0])

    if zero_fill:
        @pl.when(i == 0)
        def _():
            zeros_ref[...] = jnp.zeros_like(zeros_ref)
            n_e = zoff_ref.shape[0]
            for phase in ("start", "wait"):
                def per_expert(e, carry, phase=phase):
                    @pl.when(zflag_ref[e] > 0)
                    def _():
                        for piece in range(block_rows // ZERO_ROWS):
                            cp = zero_copy(e, piece)
                            cp.start() if phase == "start" else cp.wait()
                    return carry
                lax.fori_loop(0, n_e, per_expert, 0)

    def row_copy(t, r):
        return pltpu.make_async_copy(packed(h_ref, t, 1), packed(xs_ref, dest_ref[r, t], 1), sem.at[1])

    def start(t, carry):
        for r in range(TOP_K):
            row_copy(t, r).start(priority=r % DMA_PRIORITIES)
        return carry

    def wait(t, carry):
        for r in range(TOP_K):
            row_copy(t, r).wait()
        return carry

    lax.fori_loop(0, tt, start, 0, unroll=DMA_LOOP_UNROLL)
    lax.fori_loop(0, tt, wait, 0, unroll=DMA_LOOP_UNROLL)


def _scatter(h2, dest, zoff, zflag, n_rows, block_rows, xs_prev):
    N = h2.shape[0] // PACK_ROWS
    tt = min(ROUTE_TILE, N)
    zero_fill = xs_prev is None
    any_spec = pl.BlockSpec(memory_space=pl.ANY)
    in_specs = [pl.BlockSpec((TOP_K, tt), lambda i, *_: (0, i), memory_space=pltpu.SMEM),
                pl.BlockSpec((tt * PACK_ROWS, V7X_LANES), lambda i, *_: (i, 0))]
    args = [dest, h2]
    aliases = {}
    if not zero_fill:
        in_specs.append(any_spec)
        args.append(xs_prev)
        aliases = {4: 0}
    return pl.pallas_call(
        functools.partial(_scatter_body, zero_fill, block_rows),
        grid_spec=pltpu.PrefetchScalarGridSpec(
            num_scalar_prefetch=2, grid=(N // tt,), in_specs=in_specs, out_specs=any_spec,
            scratch_shapes=[pltpu.VMEM((ZERO_ROWS * PACK_ROWS, V7X_LANES), u32),
                            pltpu.SemaphoreType.DMA((2,))]),
        out_shape=jax.ShapeDtypeStruct((n_rows * PACK_ROWS, V7X_LANES), u32),
        input_output_aliases=aliases,
        compiler_params=_cparams(("arbitrary",)),
        name="moe_scatter",
    )(zoff, zflag, *args)


def _expert_body(nf, n_sub, bexp_ref, nused_ref, nvalid_ref, x_ref, wg_ref, wl_ref, bgl_ref, bll_ref,
                 wd_ref, bd_ref, o_ref, xb_ref, acc_ref):
    b = pl.program_id(0)
    j = pl.program_id(1)
    n_valid = nvalid_ref[b]
    active = n_valid > 0
    tm = xb_ref.shape[0]
    sub = tm // n_sub

    @pl.when(j == 0)
    def _():
        for c in range(PACK_ROWS):
            lo, hi = _unpack_chunk(x_ref, c, tm)
            xb_ref[:, c * PACK_SPAN:c * PACK_SPAN + V7X_LANES] = lo.astype(bf16)
            xb_ref[:, c * PACK_SPAN + V7X_LANES:(c + 1) * PACK_SPAN] = hi.astype(bf16)
        acc_ref[...] = jnp.where(active, jnp.broadcast_to(bd_ref[...], acc_ref.shape), 0.0)

    for n_rows, cond in ((tm, n_valid > sub), (sub, active & (n_valid <= sub))):
        @pl.when(cond)
        def _():
            rows = slice(0, n_rows)
            xb = xb_ref[rows, :]
            glu = jnp.dot(xb, wg_ref[...].astype(bf16), preferred_element_type=f32) + bgl_ref[...]
            lin = jnp.dot(xb, wl_ref[...].astype(bf16), preferred_element_type=f32) + bll_ref[...]
            glu = jnp.minimum(glu, SWIGLU_LIMIT)
            lin = jnp.clip(lin, -SWIGLU_LIMIT, SWIGLU_LIMIT)
            act = glu * jax.nn.sigmoid(SWIGLU_ALPHA * glu) * (lin + 1.0)
            acc_ref[rows, :] += jnp.dot(act.astype(bf16), wd_ref[...].astype(bf16),
                                        preferred_element_type=f32)

    @pl.when(j == nf - 1)
    def _():
        _pack_rows(acc_ref[...], o_ref)


def _experts(xs, bexp, nused, nvalid, w_gate_up, b_gate_up, w_down, b_down, tm, tf):
    n_rows = xs.shape[0] // PACK_ROWS
    E, D, F2 = w_gate_up.shape
    F = F2 // 2
    assert n_rows % tm == 0 and F % tf == 0
    nf = F // tf
    nb = n_rows // tm

    def blk(b, nused):
        return jnp.minimum(b, nused[0] - 1)

    def ftile(b, j, nused):
        return jnp.where(b < nused[0], j, nf - 1)

    b_gu = b_gate_up.reshape(E, 1, F2)
    b_d = b_down.reshape(E, 1, D)
    return pl.pallas_call(
        functools.partial(_expert_body, nf, EXPERT_SUB_BLOCKS),
        grid_spec=pltpu.PrefetchScalarGridSpec(
            num_scalar_prefetch=3, grid=(nb, nf),
            in_specs=[
                pl.BlockSpec((tm * PACK_ROWS, V7X_LANES), lambda b, j, be, nu, nv: (blk(b, nu), 0)),
                pl.BlockSpec((None, D, tf), lambda b, j, be, nu, nv: (be[blk(b, nu)], 0, ftile(b, j, nu))),
                pl.BlockSpec((None, D, tf),
                             lambda b, j, be, nu, nv: (be[blk(b, nu)], 0, nf + ftile(b, j, nu))),
                pl.BlockSpec((None, 1, tf), lambda b, j, be, nu, nv: (be[blk(b, nu)], 0, ftile(b, j, nu))),
                pl.BlockSpec((None, 1, tf),
                             lambda b, j, be, nu, nv: (be[blk(b, nu)], 0, nf + ftile(b, j, nu))),
                pl.BlockSpec((None, tf, D), lambda b, j, be, nu, nv: (be[blk(b, nu)], ftile(b, j, nu), 0)),
                pl.BlockSpec((None, 1, D), lambda b, j, be, nu, nv: (be[blk(b, nu)], 0, 0)),
            ],
            out_specs=pl.BlockSpec((tm * PACK_ROWS, V7X_LANES), lambda b, j, be, nu, nv: (b, 0)),
            scratch_shapes=[pltpu.VMEM((tm, D), bf16), pltpu.VMEM((tm, D), f32)]),
        out_shape=jax.ShapeDtypeStruct((n_rows * PACK_ROWS, V7X_LANES), u32),
        compiler_params=_cparams(("arbitrary", "arbitrary")),
        name="moe_experts",
    )(bexp, nused, nvalid, xs, w_gate_up, w_gate_up, b_gu, b_gu, w_down, b_d)


COMBINE_TILE = 256


def _combine_body(n, dcur_ref, dnext_ref, x1_ref, gate_ref, gf_ref, ys_ref, y_ref, buf_ref, sem):
    i = pl.program_id(0)
    tt = x1_ref.shape[0]

    def packed(row):
        return pl.ds(pl.multiple_of(row * PACK_ROWS, PACK_ROWS), PACK_ROWS)

    def row_copy(d_ref, slot, t, r):
        return pltpu.make_async_copy(ys_ref.at[packed(d_ref[r, t])], buf_ref.at[slot, r, packed(t)],
                                     sem.at[slot])

    def issue(d_ref, slot):
        def body(t, carry):
            for r in range(TOP_K):
                row_copy(d_ref, slot, t, r).start(priority=r % DMA_PRIORITIES)
            return carry
        lax.fori_loop(0, tt, body, 0, unroll=DMA_LOOP_UNROLL)

    @pl.when(i == 0)
    def _():
        issue(dcur_ref, 0)

    @pl.when(i + 1 < n)
    def _():
        issue(dnext_ref, (i + 1) % 2)

    slot = i % 2

    def wait(t, carry):
        for r in range(TOP_K):
            row_copy(dcur_ref, slot, t, r).wait()
        return carry
    lax.fori_loop(0, tt, wait, 0, unroll=DMA_LOOP_UNROLL)

    gates = gate_ref[...]
    parts = []
    for c in range(PACK_ROWS):
        lo_sum = x1_ref[:, c * PACK_SPAN:c * PACK_SPAN + V7X_LANES]
        hi_sum = x1_ref[:, c * PACK_SPAN + V7X_LANES:(c + 1) * PACK_SPAN]
        for r in range(TOP_K):
            lo, hi = _unpack_chunk(buf_ref.at[slot, r], c, tt)
            lo_sum = lo_sum + gates[:, r:r + 1] * lo
            hi_sum = hi_sum + gates[:, r:r + 1] * hi
        parts += [lo_sum, hi_sum]
    x = jnp.concatenate(parts, axis=1)
    ms = jnp.mean(x * x, axis=-1, keepdims=True)
    y_ref[...] = x * lax.rsqrt(ms + RMS_EPS) * gf_ref[...]


def _combine(x1, gates_tm, dest, ys, g_final):
    N, D = x1.shape
    tt = min(COMBINE_TILE, N)
    n = N // tt
    smem = lambda f: pl.BlockSpec((TOP_K, tt), f, memory_space=pltpu.SMEM)
    return pl.pallas_call(
        functools.partial(_combine_body, n),
        grid=(n,),
        in_specs=[smem(lambda i: (0, i)), smem(lambda i: (0, jnp.minimum(i + 1, n - 1))),
                  pl.BlockSpec((tt, D), lambda i: (i, 0)), pl.BlockSpec((tt, TOP_K), lambda i: (i, 0)),
                  pl.BlockSpec((1, D), lambda i: (0, 0)), pl.BlockSpec(memory_space=pl.ANY)],
        out_specs=pl.BlockSpec((tt, D), lambda i: (i, 0)),
        out_shape=jax.ShapeDtypeStruct((N, D), f32),
        scratch_shapes=[pltpu.VMEM((2, TOP_K, tt * PACK_ROWS, V7X_LANES), u32),
                        pltpu.SemaphoreType.DMA((2,))],
        compiler_params=_cparams(("arbitrary",)),
        name="moe_combine",
    )(dest, dest, x1, gates_tm, g_final.reshape(1, D), ys)


def _moe(h2_groups, logit_groups, x1_groups, ew, g_final, tm, tf):
    w_gate_up, b_gate_up, w_down, b_down = ew
    E = w_gate_up.shape[0]
    routed = []
    count = jnp.zeros((E, V7X_LANES), f32)
    for lg in logit_groups:
        idx, gate, pos, count = _route(lg, count)
        routed.append((idx, gate, pos))
    total = sum(lg.shape[1] for lg in logit_groups) * TOP_K
    n_blocks = -(-(total + E * (tm - 1)) // tm)
    n_rows = n_blocks * tm
    counts = count[:, 0].astype(i32)
    padded = (counts + tm - 1) // tm * tm
    pad_end = jnp.cumsum(padded)
    pad_start = pad_end - padded
    n_used = pad_end[-1] // tm
    blocks = jnp.minimum(jnp.arange(n_blocks, dtype=i32), n_used - 1)
    bexp = jnp.minimum(jnp.sum(blocks[:, None] * tm >= pad_end[None, :], axis=1), E - 1).astype(i32)
    all_blocks = jnp.arange(n_blocks, dtype=i32)
    zoff = jnp.concatenate([jnp.maximum(pad_end - tm, 0).astype(i32), all_blocks * tm])
    zflag = jnp.concatenate([counts > 0, all_blocks >= n_used]).astype(i32)
    xs = None
    dests = []
    for h2, (idx, gate, pos) in zip(h2_groups, routed):
        dest = _dest(idx, pos, pad_start)
        dests.append(dest)
        xs = _scatter(h2, dest, zoff, zflag, n_rows, tm, xs)
    row0 = all_blocks * tm
    nvalid = jnp.clip(pad_start[bexp] + counts[bexp] - row0, 0, tm)
    nvalid = jnp.where(all_blocks < n_used, nvalid, 0).astype(i32)
    ys = _experts(xs, bexp, n_used.reshape(1).astype(i32), nvalid, w_gate_up, b_gate_up, w_down, b_down,
                  tm, tf)
    return [_combine(x1, gate.T, dest, ys, g_final)
            for x1, (idx, gate, pos), dest in zip(x1_groups, routed, dests)]


PROJ_TILE_M = 1024
PROJ_TILE_N = 1920
MERGE_TILE_M = 256
EXPERT_TILE_M = 1024
EXPERT_TILE_F = 256
EXPERT_SUB_BLOCKS = 2


def _mixers(x, state, norm_g, w_main, w_u, rw, sw, mw):
    B, T, D = x.shape
    W = w_u.shape[1]
    shift_w = 3 * W + W_LORA + A_LORA + G_LORA
    G, S = sw[0].shape[1] // 64, 64
    if state is None:
        s_rwkv0 = jnp.zeros((B, W // HEAD, HEAD, HEAD), f32)
        s5_re0 = jnp.zeros((B, W // 16, 64), f32)
        s5_im0 = jnp.zeros((B, W // 16, 64), f32)
        shift0 = jnp.zeros((B, shift_w), f32)
    else:
        s_rwkv0, s5_re0, s5_im0, shift0 = state
    per_batch = T % PROJ_TILE_M == 0
    xf = x if per_batch else x.reshape(1, B * T, D)
    tm = PROJ_TILE_M if per_batch else min(PROJ_TILE_M, B * T)
    proj = _norm_proj(xf, norm_g, w_main, tm, PROJ_TILE_N, False, bf16).reshape(B, T, MAIN_COLS)
    if per_batch:
        u_tb = _norm_proj(xf, norm_g, w_u, tm, W, True, f32)
    else:
        u_tb = jnp.swapaxes(_norm_proj(xf, norm_g, w_u, tm, W, False, f32).reshape(B, T, W), 0, 1)
    o_a, s_rwkv = _rwkv_stage(proj, shift0, s_rwkv0, rw)
    o_b, s5_re, s5_im = _s5_stage(u_tb, s5_re0, s5_im0, sw)
    if per_batch:
        x1, h2, logits_t = _merge_stage(x, o_a, o_b, True, proj, mw, MERGE_TILE_M)
    else:
        flat = lambda t: t.reshape(1, B * T, t.shape[-1])
        x1, h2, logits_t = _merge_stage(flat(x), flat(o_a), flat(jnp.swapaxes(o_b, 0, 1)), False,
                                        flat(proj), mw, min(MERGE_TILE_M, B * T))
    last = proj[:, -1, :].astype(f32)
    n_lora = W_LORA + A_LORA + G_LORA
    shift = jnp.concatenate([last[:, COL_R * W:(COL_R + 3) * W],
                             last[:, COL_LORA * LORA_PAD:COL_LORA * LORA_PAD + n_lora]], axis=1)
    return (x1.reshape(B * T, D), h2, logits_t), (s_rwkv, s5_re, s5_im, shift)


def kernel(x_prompt, x_sample, state_rwkv, state_s5_re, state_s5_im, state_shift, norm_mix_g, w_in, mu_shift, w0, w2, a0, a2, g2, k_k, k_a, r_k, lnx_g, lnx_b, lam_re, lam_im, log_dt, b_re, b_im, c_re, c_im, d_skip, w_glu, b_glu, w_up_a, w_up_b, w_out, norm_ffn_g, w_router, b_router, w_gate_up, b_gate_up, w_down, b_down, norm_final_g):
    assert w_in.shape[0] == 1, "single-layer trunk"
    D = x_prompt.shape[-1]
    W = w0.shape[-1]
    n_lora = W_LORA + A_LORA + G_LORA
    shift_w = 3 * W + n_lora
    w = w_in[0]
    ga0 = shift_w + W
    w_main = jnp.concatenate(
        [w[:, ga0:ga0 + 2 * D], w[:, :3 * W], w[:, 3 * W:shift_w],
         jnp.zeros((D, LORA_PAD - n_lora), w.dtype)], axis=1).astype(bf16)
    assert w_main.shape[1] == MAIN_COLS
    w_u = w[:, shift_w:shift_w + W].astype(bf16)
    rw = _rwkv_weights(mu_shift[0], w0[0], w2[0], a0[0], a2[0], g2[0], k_k[0], k_a[0], r_k[0],
                       lnx_g[0], lnx_b[0])
    sw = _s5_weights(lam_re[0], lam_im[0], log_dt[0], b_re[0], b_im[0], c_re[0], c_im[0],
                     d_skip[0], w_glu[0], b_glu[0])
    E = w_router.shape[-1]
    mw = (w_up_a[0].astype(bf16), w_up_b[0].astype(bf16), w_out[0].astype(bf16),
          norm_ffn_g[0].reshape(1, D).astype(f32), w_router[0].T.astype(bf16),
          b_router[0].reshape(E, 1).astype(f32))
    ew = (w_gate_up[0], b_gate_up[0], w_down[0], b_down[0])

    state_s = (state_rwkv[0], state_s5_re[0], state_s5_im[0], state_shift[0])
    tok_p, st_p = _mixers(x_prompt, None, norm_mix_g[0], w_main, w_u, rw, sw, mw)
    tok_s, st_s = _mixers(x_sample, state_s, norm_mix_g[0], w_main, w_u, rw, sw, mw)
    y_p, y_s = _moe([tok_p[1], tok_s[1]], [tok_p[2], tok_s[2]], [tok_p[0], tok_s[0]], ew, norm_final_g,
                    EXPERT_TILE_M, EXPERT_TILE_F)
    lead = lambda t: t[None]
    return (y_p.reshape(x_prompt.shape), y_s.reshape(x_sample.shape),
            lead(st_p[0]), lead(st_p[1]), lead(st_p[2]), lead(st_p[3]),
            lead(st_s[0]), lead(st_s[1]), lead(st_s[2]), lead(st_s[3]))
```

```python
import functools
import math

import jax
import jax.numpy as jnp
from jax import lax
from jax.experimental import pallas as pl
from jax.experimental.pallas import tpu as pltpu

f32 = jnp.float32
bf16 = jnp.bfloat16
i32 = jnp.int32
u32 = jnp.uint32

V7X_LANES = 128
V7X_SUBLANES = 8
V7X_VMEM_BYTES = 64 * 1024 * 1024
VMEM_LIMIT = 56 * 1024 * 1024

RMS_EPS = 1e-5
GN_EPS = 64e-5
HEAD = 64
HEAD_SHIFT = 6
HEADS_PER_GROUP = 4
GROUP = HEAD * HEADS_PER_GROUP
CHUNK = 64
W_LORA, A_LORA, G_LORA = 64, 64, 160
LORA_PAD = 512
TOP_K = 4
SWIGLU_LIMIT = 7.0
SWIGLU_ALPHA = 1.702


def _cparams(sem):
    return pltpu.CompilerParams(dimension_semantics=sem, vmem_limit_bytes=VMEM_LIMIT)


def _dot(a, b):
    return jnp.dot(a.astype(bf16), b.astype(bf16), preferred_element_type=f32)


def _dot_nt(a, b):
    return lax.dot_general(a.astype(bf16), b.astype(bf16), (((1,), (1,)), ((), ())),
                           preferred_element_type=f32)


def _dot_tn(a, b):
    return lax.dot_general(a.astype(bf16), b.astype(bf16), (((0,), (0,)), ((), ())),
                           preferred_element_type=f32)


def _split(x):
    hi = x.astype(bf16)
    lo = (x - hi.astype(f32)).astype(bf16)
    return hi, lo


def _dot_exact_rhs(a, b_exact):
    hi, lo = _split(a)
    return (jnp.dot(hi, b_exact, preferred_element_type=f32)
            + jnp.dot(lo, b_exact, preferred_element_type=f32))


def _dot_exact_lhs(a_exact, b):
    hi, lo = _split(b)
    return (jnp.dot(a_exact, hi, preferred_element_type=f32)
            + jnp.dot(a_exact, lo, preferred_element_type=f32))


def _norm_proj_body(x_ref, g_ref, w_ref, o_ref, h_ref):
    @pl.when(pl.program_id(2) == 0)
    def _():
        x = x_ref[...]
        ms = jnp.mean(x * x, axis=-1, keepdims=True)
        h_ref[...] = (x * lax.rsqrt(ms + RMS_EPS) * g_ref[...]).astype(bf16)

    o_ref[...] = jnp.dot(h_ref[...], w_ref[...], preferred_element_type=f32).astype(o_ref.dtype)


def _norm_proj(x, gain, w, tm, tn, time_major_out, out_dtype):
    B, T, D = x.shape
    N = w.shape[1]
    assert T % tm == 0 and N % tn == 0
    if time_major_out:
        out_shape = jax.ShapeDtypeStruct((T, B * N), out_dtype)
        out_spec = pl.BlockSpec((tm, tn), lambda b, i, n: (i, b * (N // tn) + n))
    else:
        out_shape = jax.ShapeDtypeStruct((B, T, N), out_dtype)
        out_spec = pl.BlockSpec((None, tm, tn), lambda b, i, n: (b, i, n))
    out = pl.pallas_call(
        _norm_proj_body,
        grid=(B, T // tm, N // tn),
        in_specs=[pl.BlockSpec((None, tm, D), lambda b, i, n: (b, i, 0)),
                  pl.BlockSpec((1, D), lambda b, i, n: (0, 0)),
                  pl.BlockSpec((D, tn), lambda b, i, n: (0, n))],
        out_specs=out_spec,
        out_shape=out_shape,
        scratch_shapes=[pltpu.VMEM((tm, D), bf16)],
        compiler_params=_cparams(("parallel", "parallel", "arbitrary")),
        name="norm_proj",
    )(x, gain.reshape(1, D), w)
    return out.reshape(T, B, N) if time_major_out else out


PV_MU_R, PV_MU_K, PV_MU_V, PV_W0, PV_A0, PV_KK, PV_KA, PV_RK, PV_LNG, PV_LNB = range(10)
PV_ROWS = 16
PREV_ROWS = 16
RWKV_CHUNKS_PER_STEP = 4


def _rwkv_body(nc, r_ref, k_ref, v_ref, l_ref, pr_ref, pk_ref, pv_ref, plo_ref,
               lr_ref, lk_ref, lv_ref, llo_ref, pvec_ref, mul_ref, wl_ref, s0_ref,
               o_ref, so_ref, s_ref):
    c = pl.program_id(1)
    C = CHUNK
    R, W = r_ref.shape
    n_chunks = R // C
    n_groups = W // GROUP

    @pl.when(c == 0)
    def _():
        s_ref[...] = s0_ref[...]

    row = lax.broadcasted_iota(i32, (R, 1), 0)

    def shifted(x_ref, prev_ref, last_ref, mu):
        x = x_ref[...].astype(f32)
        prev_row = jnp.where(c > 0, prev_ref[PREV_ROWS - 1:PREV_ROWS, :].astype(f32), last_ref[...])
        prev = jnp.where(row == 0, prev_row, pltpu.roll(x, shift=1, axis=0))
        return x + (prev - x) * mu

    pvec = pvec_ref[...]
    prm = lambda i: pvec[i:i + 1, :]
    r = shifted(r_ref, pr_ref, lr_ref, prm(PV_MU_R))
    k = shifted(k_ref, pk_ref, lk_ref, prm(PV_MU_K))
    v = shifted(v_ref, pv_ref, lv_ref, prm(PV_MU_V))
    xl = shifted(l_ref, plo_ref, llo_ref, mul_ref[...])

    lane_l = lax.broadcasted_iota(i32, (1, xl.shape[1]), 1)
    act = jnp.where(lane_l < W_LORA, jnp.tanh(xl),
                    jnp.where(lane_l < W_LORA + A_LORA, xl, jax.nn.sigmoid(xl)))
    lo = jnp.dot(act.astype(bf16), wl_ref[...], preferred_element_type=f32)
    z = -(prm(PV_W0) + lo[:, :W])
    w = -(jnp.maximum(z, 0.0) + jnp.log(1.0 + jnp.exp(-jnp.abs(z)))) - 0.5
    logd = -jnp.exp(w)
    a = jax.nn.sigmoid(prm(PV_A0) + lo[:, W:2 * W])
    g = lo[:, 2 * W:]

    gi = lax.broadcasted_iota(i32, (GROUP, GROUP), 0)
    gj = lax.broadcasted_iota(i32, (GROUP, GROUP), 1)
    same_head = (gi >> HEAD_SHIFT) == (gj >> HEAD_SHIFT)
    head_ones = jnp.where(same_head, 1.0, 0.0).astype(bf16)
    qt = lax.broadcasted_iota(i32, (C, GROUP), 0)
    qs = lax.broadcasted_iota(i32, (C, GROUP), 1) & (HEAD - 1)
    strict_q, incl_q = qt > qs, qt >= qs
    ci = lax.broadcasted_iota(i32, (R, R), 0)
    cj = lax.broadcasted_iota(i32, (R, R), 1)
    tri = jnp.where((ci >= cj) & ((ci >> HEAD_SHIFT) == (cj >> HEAD_SHIFT)), 1.0, 0.0).astype(bf16)
    lane_head = lax.broadcasted_iota(i32, (1, GROUP), 1) >> HEAD_SHIFT
    eye_quad = jnp.where(qt == qs, 1.0, 0.0)

    def head_sum(x, exact):
        parts = []
        for gq in range(n_groups):
            xg = x[:, gq * GROUP:(gq + 1) * GROUP]
            parts.append(_dot_exact_rhs(xg, head_ones) if exact
                         else jnp.dot(xg.astype(bf16), head_ones, preferred_element_type=f32))
        return jnp.concatenate(parts, axis=1)

    kk = k * prm(PV_KK)
    kk = kk * lax.rsqrt(jnp.maximum(head_sum(kk * kk, True), 1e-24))
    k2 = k * (1.0 + (a - 1.0) * prm(PV_KA))
    cl = _dot_exact_lhs(tri, logd)
    p_incl = jnp.exp(cl)
    p_excl = jnp.exp(cl - logd)
    p_inv = jnp.exp(-cl)
    rt = r * p_incl
    at = -kk * p_excl
    bt = kk * a * p_inv
    kt = k2 * p_inv
    bonus = head_sum(r * k2 * prm(PV_RK), False) * v

    def stack(x):
        return jnp.concatenate([jnp.where(lane_head == h, x, 0.0) for h in range(HEADS_PER_GROUP)], axis=0)

    def tile(x):
        return jnp.concatenate([x] * HEADS_PER_GROUP, axis=0)

    def block(x):
        return jnp.where(same_head, tile(x), 0.0)

    groups = range(n_groups)
    each = lambda f, *lists: [f(*xs) for xs in zip(*lists)]
    lanes = [slice(gq * GROUP, (gq + 1) * GROUP) for gq in groups]
    tiles = [(slice(ch * C, (ch + 1) * C), sl) for ch in range(n_chunks) for sl in lanes]
    ag, rg, bg, kg, vg = ([x[rows, sl] for rows, sl in tiles] for x in (at, rt, bt, kt, v))
    sa, sv = each(stack, ag), each(stack, vg)
    bd_t = lambda y: jnp.where(same_head, tile(y).T, 0.0)
    tb, tk = each(bd_t, bg), each(bd_t, kg)
    ab = each(lambda x, y: jnp.where(strict_q, _dot(x, y), 0.0), ag, tb)
    ak = each(lambda x, y: jnp.where(strict_q, _dot(x, y), 0.0), ag, tk)
    rb = each(lambda x, y: jnp.where(incl_q, _dot(x, y), 0.0), rg, tb)
    rk = each(lambda x, y: jnp.where(incl_q, _dot(x, y), 0.0), rg, tk)
    q4, qb = ab, each(block, ab)
    t4 = each(lambda q: eye_quad + q, q4)
    for _ in range(max(1, (C - 1).bit_length() - 1)):
        q4 = each(_dot, q4, qb)
        qb = each(block, q4)
        t4 = each(lambda t, q: t + _dot(t, q), t4, qb)
    w4 = each(_dot, t4, sa)
    akv = each(_dot, ak, sv)
    uv = each(lambda t, x: _dot(t, stack(x)), t4, akv)
    orv = each(_dot, rk, sv)
    state = [s_ref[gq] for gq in groups]
    o_rows = []
    for ch in range(n_chunks):
        of = lambda xs: xs[ch * n_groups:(ch + 1) * n_groups]
        u4 = each(lambda w_, s_, uv_: _dot_nt(w_, s_) + uv_, of(w4), state, of(uv))
        rs = each(_dot_nt, of(rg), state)
        upd = each(lambda u_, v_, b_, k_: _dot_tn(jnp.concatenate([u_, v_], axis=0),
                                                  jnp.concatenate([b_, k_], axis=0)),
                   u4, of(vg), of(bg), of(kg))
        o4 = each(lambda rs_, rb_, u_, orv_: rs_ + _dot(rb_, stack(u_)) + orv_,
                  rs, of(rb), u4, of(orv))
        p_end = p_incl[(ch + 1) * C - 1:(ch + 1) * C, :]
        state = [(s + jnp.where(same_head, d, 0.0)) * p_end[:, sl] for s, d, sl in zip(state, upd, lanes)]
        o_rows.append(jnp.concatenate(o4, axis=1))
    for gq in groups:
        s_ref[gq] = state[gq]
    o = jnp.concatenate(o_rows, axis=0)

    inv_n = 1.0 / HEAD
    mean = head_sum(o, False) * inv_n
    d = o - mean
    var = head_sum(d * d, False) * inv_n
    on = d * lax.rsqrt(var + GN_EPS)
    o_ref[...] = (on * prm(PV_LNG) + prm(PV_LNB) + bonus) * g

    @pl.when(c == nc - 1)
    def _():
        so_ref[...] = s_ref[...]


def _rwkv(proj, col_r, col_lora, last_rkv, last_lora, pvec, mu_lora, w_lora, s0_blocks):
    B, T, _ = proj.shape
    W = pvec.shape[1]
    C = CHUNK
    assert T % C == 0 and C == HEAD and HEAD == 1 << HEAD_SHIFT
    n_groups = W // GROUP
    R = C * RWKV_CHUNKS_PER_STEP if T % (C * RWKV_CHUNKS_PER_STEP) == 0 else C
    rb = R // PREV_ROWS

    def cur(col, width):
        return pl.BlockSpec((None, R, width), lambda b, c, col=col: (b, c, col))

    def prev(col, width):
        return pl.BlockSpec((None, PREV_ROWS, width),
                            lambda b, c, col=col: (b, jnp.maximum(c * rb - 1, 0), col))

    def last(col, width):
        return pl.BlockSpec((None, 1, width), lambda b, c, col=col: (b, 0, col))

    const = lambda shape: pl.BlockSpec(shape, lambda b, c: (0,) * len(shape))
    state_spec = pl.BlockSpec((None, n_groups, GROUP, GROUP), lambda b, c: (b, 0, 0, 0))
    return pl.pallas_call(
        functools.partial(_rwkv_body, T // R),
        grid=(B, T // R),
        in_specs=[cur(col_r, W), cur(col_r + 1, W), cur(col_r + 2, W), cur(col_lora, LORA_PAD),
                  prev(col_r, W), prev(col_r + 1, W), prev(col_r + 2, W), prev(col_lora, LORA_PAD),
                  last(0, W), last(1, W), last(2, W), last(0, LORA_PAD),
                  const((PV_ROWS, W)), const((1, LORA_PAD)), const((LORA_PAD, 3 * W)), state_spec],
        out_specs=[pl.BlockSpec((None, R, W), lambda b, c: (b, c, 0)), state_spec],
        out_shape=[jax.ShapeDtypeStruct((B, T, W), f32),
                   jax.ShapeDtypeStruct((B, n_groups, GROUP, GROUP), f32)],
        scratch_shapes=[pltpu.VMEM((n_groups, GROUP, GROUP), f32)],
        compiler_params=_cparams(("parallel", "arbitrary")),
        name="rwkv7_chunked",
    )(proj, proj, proj, proj, proj, proj, proj, proj,
      last_rkv, last_rkv, last_rkv, last_lora, pvec, mu_lora, w_lora, s0_blocks)


COL_GA, COL_GB = 0, 1
COL_R = 4
COL_LORA = 14
MAIN_COLS = 7680


def _rwkv_weights(mu_shift, w0, w2, a0, a2, g2, k_k, k_a, r_k, lnx_g, lnx_b):
    W = w0.shape[0]
    rows = [mu_shift[:W], mu_shift[W:2 * W], mu_shift[2 * W:3 * W], w0, a0, k_k, k_a,
            r_k.reshape(W), lnx_g, lnx_b]
    pvec = jnp.zeros((PV_ROWS, W), f32).at[:len(rows)].set(jnp.stack(rows).astype(f32))
    n_lora = W_LORA + A_LORA + G_LORA
    mu_lora = jnp.zeros((1, LORA_PAD), f32).at[0, :n_lora].set(mu_shift[3 * W:])
    w_lora = jnp.zeros((LORA_PAD, 3 * W), f32)
    w_lora = w_lora.at[:W_LORA, :W].set(w2)
    w_lora = w_lora.at[W_LORA:W_LORA + A_LORA, W:2 * W].set(a2)
    w_lora = w_lora.at[W_LORA + A_LORA:n_lora, 2 * W:].set(g2)
    return pvec, mu_lora, w_lora.astype(bf16)


def _rwkv_stage(proj, shift0, s0, rw):
    pvec, mu_lora, w_lora = rw
    B = proj.shape[0]
    W = pvec.shape[1]
    H = W // HEAD
    n_groups = H // HEADS_PER_GROUP
    last_rkv = shift0[:, None, :3 * W].astype(f32)
    last_lora = jnp.zeros((B, 1, LORA_PAD), f32).at[:, 0, :shift0.shape[1] - 3 * W].set(shift0[:, 3 * W:])
    eye = jnp.eye(HEADS_PER_GROUP, dtype=f32)
    s0_blocks = jnp.einsum('bghij,hk->bghikj', s0.reshape(B, n_groups, HEADS_PER_GROUP, HEAD, HEAD), eye)
    s0_blocks = s0_blocks.reshape(B, n_groups, GROUP, GROUP)
    o, s_blocks = _rwkv(proj, COL_R, COL_LORA, last_rkv, last_lora, pvec, mu_lora, w_lora, s0_blocks)
    s5 = s_blocks.reshape(B, n_groups, HEADS_PER_GROUP, HEAD, HEADS_PER_GROUP, HEAD)
    s_new = jnp.einsum('bghikj,hk->bghij', s5, eye).reshape(B, H, HEAD, HEAD)
    return o, s_new


S5_BLOCK_IN = 128
S5_BLOCK_STATE = 512


def _s5_prep_body(lr_ref, li_ref, ldt_ref, br_ref, bi_ref, lbr_ref, lbi_ref, bbr_ref, bbi_ref):
    lr, li = lr_ref[...], li_ref[...]
    dt = jnp.exp(ldt_ref[...])
    mag = jnp.exp(lr * dt)
    er, ei = mag * jnp.cos(li * dt), mag * jnp.sin(li * dt)
    lbr_ref[...] = er
    lbi_ref[...] = ei
    den = lr * lr + li * li
    cr = ((er - 1.0) * lr + ei * li) / den
    ci = (ei * lr - (er - 1.0) * li) / den
    br, bi = br_ref[...], bi_ref[...]
    bbr_ref[...] = cr * br - ci * bi
    bbi_ref[...] = cr * bi + ci * br


def _s5_weights(lam_re, lam_im, log_dt, b_re, b_im, c_re, c_im, d_skip, w_glu, b_glu):
    G, S = lam_re.shape
    P = b_re.shape[-1]
    n = G * S
    col = lambda t: t.reshape(n, 1).astype(f32)
    ldt = jnp.broadcast_to(log_dt[:, None], (G, S))
    shapes = [jax.ShapeDtypeStruct((n, 1), f32)] * 2 + [jax.ShapeDtypeStruct((n, P), f32)] * 2
    lbr, lbi, bbr, bbi = pl.pallas_call(_s5_prep_body, out_shape=shapes, name="s5_discretise")(
        col(lam_re), col(lam_im), col(ldt), b_re.reshape(n, P).astype(f32), b_im.reshape(n, P).astype(f32))
    gpb = S5_BLOCK_IN // P
    nb = G // gpb
    eye = jnp.eye(gpb, dtype=f32)

    def in_block(bb):
        t = bb.reshape(nb, gpb, S, P)
        return jnp.einsum('qgsp,gh->qgphs', t, eye).reshape(nb, gpb * P, gpb * S)

    def out_block(cc):
        t = cc.astype(f32).reshape(nb, gpb, P, S)
        return jnp.einsum('qgps,gh->qgshp', t, eye).reshape(nb, gpb * S, gpb * P)

    wb = jnp.concatenate([in_block(bbr), in_block(bbi)], axis=2).astype(bf16)
    wc = jnp.concatenate([out_block(c_re), -out_block(c_im)], axis=1).astype(bf16)
    W = G * P
    return (lbr.reshape(1, n), lbi.reshape(1, n), wb, wc, d_skip.reshape(1, W).astype(f32),
            w_glu.astype(bf16), b_glu.reshape(1, W).astype(f32))


def _s5_body(n_steps, u_ref, x0r_ref, x0i_ref, lbr_ref, lbi_ref, wb_ref, wc_ref, d_ref, wg_ref, bg_ref,
             o_ref, xr_out, xi_out, bur_ref, bui_ref, xr_ref, xi_ref, y_ref):
    i = pl.program_id(1)
    Tc, Bs, W = u_ref.shape
    rows = Tc * Bs
    nb = wb_ref.shape[0]
    ns = S5_BLOCK_STATE

    @pl.when(i == 0)
    def _():
        xr_ref[...] = x0r_ref[...]
        xi_ref[...] = x0i_ref[...]

    u = u_ref[...].reshape(rows, W)
    for q in range(nb):
        bu = jnp.dot(u[:, q * S5_BLOCK_IN:(q + 1) * S5_BLOCK_IN].astype(bf16), wb_ref[q],
                     preferred_element_type=f32)
        bur_ref[:, q * ns:(q + 1) * ns] = bu[:, :ns]
        bui_ref[:, q * ns:(q + 1) * ns] = bu[:, ns:]

    for q in range(nb):
        sl = slice(q * ns, (q + 1) * ns)
        lr = jnp.broadcast_to(lbr_ref[:, sl], (Bs, ns))
        li = jnp.broadcast_to(lbi_ref[:, sl], (Bs, ns))

        def step(t, carry):
            xr, xi = carry
            rs = pl.ds(pl.multiple_of(t * Bs, Bs), Bs)
            nr = lr * xr - li * xi + bur_ref[rs, sl]
            ni = lr * xi + li * xr + bui_ref[rs, sl]
            bur_ref[rs, sl] = nr
            bui_ref[rs, sl] = ni
            return nr, ni

        xr, xi = lax.fori_loop(0, Tc, step, (xr_ref[:, sl], xi_ref[:, sl]), unroll=True)
        xr_ref[:, sl] = xr
        xi_ref[:, sl] = xi
        y_ref[:, q * S5_BLOCK_IN:(q + 1) * S5_BLOCK_IN] = (
            jnp.dot(bur_ref[:, sl].astype(bf16), wc_ref[q, :ns, :], preferred_element_type=f32)
            + jnp.dot(bui_ref[:, sl].astype(bf16), wc_ref[q, ns:, :], preferred_element_type=f32))

    y = y_ref[...] + d_ref[...] * u
    y = 0.5 * y * (1.0 + jnp.tanh(math.sqrt(2.0 / math.pi) * (y + 0.044715 * (y * y * y))))
    z = jnp.dot(y.astype(bf16), wg_ref[...], preferred_element_type=f32) + bg_ref[...]
    o_ref[...] = (y * jax.nn.sigmoid(z)).reshape(Tc, Bs, W)

    @pl.when(i == n_steps - 1)
    def _():
        xr_out[...] = xr_ref[...]
        xi_out[...] = xi_ref[...]


def _s5_stage(u_tb, x0_re, x0_im, sw):
    lbr, lbi, wb, wc, d_skip, w_glu, b_glu = sw
    T, B, W = u_tb.shape
    G, S = x0_re.shape[1:]
    n = G * S
    Bs = V7X_SUBLANES
    Tc = min(64, T)
    assert B % Bs == 0 and T % Tc == 0
    rows = Tc * Bs
    const = lambda shape: pl.BlockSpec(shape, lambda j, i: (0,) * len(shape))
    state_spec = pl.BlockSpec((Bs, n), lambda j, i: (j, 0))
    o, xr, xi = pl.pallas_call(
        functools.partial(_s5_body, T // Tc),
        grid=(B // Bs, T // Tc),
        in_specs=[pl.BlockSpec((Tc, Bs, W), lambda j, i: (i, j, 0)), state_spec, state_spec,
                  const((1, n)), const((1, n)), const(wb.shape), const(wc.shape),
                  const((1, W)), const((W, W)), const((1, W))],
        out_specs=[pl.BlockSpec((Tc, Bs, W), lambda j, i: (i, j, 0)), state_spec, state_spec],
        out_shape=[jax.ShapeDtypeStruct((T, B, W), f32),
                   jax.ShapeDtypeStruct((B, n), f32), jax.ShapeDtypeStruct((B, n), f32)],
        scratch_shapes=[pltpu.VMEM((rows, n), f32), pltpu.VMEM((rows, n), f32),
                        pltpu.VMEM((Bs, n), f32), pltpu.VMEM((Bs, n), f32), pltpu.VMEM((rows, W), f32)],
        compiler_params=_cparams(("parallel", "arbitrary")),
        name="s5_scan",
    )(u_tb, x0_re.reshape(B, n).astype(f32), x0_im.reshape(B, n).astype(f32),
      lbr, lbi, wb, wc, d_skip, w_glu, b_glu)
    return o, xr.reshape(B, G, S), xi.reshape(B, G, S)


PACK_ROWS = V7X_SUBLANES
PACK_SPAN = 2 * V7X_LANES
HIGH_HALF = 0xFFFF0000


def _pack_rows(x, out_ref):
    n, D = x.shape
    assert D == PACK_ROWS * PACK_SPAN
    bits = lax.bitcast_convert_type(x.astype(bf16).astype(f32), u32)
    for c in range(PACK_ROWS):
        lo = bits[:, c * PACK_SPAN:c * PACK_SPAN + V7X_LANES]
        hi = bits[:, c * PACK_SPAN + V7X_LANES:(c + 1) * PACK_SPAN]
        out_ref[pl.ds(c, n, stride=PACK_ROWS), :] = (lo >> 16) | (hi & jnp.uint32(HIGH_HALF))


def _unpack_chunk(ref, c, n):
    words = ref[pl.ds(c, n, stride=PACK_ROWS), :]
    lo = lax.bitcast_convert_type(words << 16, f32)
    hi = lax.bitcast_convert_type(words & jnp.uint32(HIGH_HALF), f32)
    return lo, hi


def _merge_body(x_ref, oa_ref, ob_ref, ga_ref, gb_ref, wa_ref, wb_ref, wo_ref, gf_ref, wr_ref, br_ref,
                x1_ref, h2_ref, lg_ref):
    ya = jnp.dot(oa_ref[...].astype(bf16), wa_ref[...], preferred_element_type=f32)
    yb = jnp.dot(ob_ref[...].astype(bf16), wb_ref[...], preferred_element_type=f32)
    merged = (jax.nn.sigmoid(ga_ref[...].astype(f32)) * ya
              + jax.nn.sigmoid(gb_ref[...].astype(f32)) * yb)
    x1 = x_ref[...] + jnp.dot(merged.astype(bf16), wo_ref[...], preferred_element_type=f32)
    x1_ref[...] = x1
    ms = jnp.mean(x1 * x1, axis=-1, keepdims=True)
    h2 = x1 * lax.rsqrt(ms + RMS_EPS) * gf_ref[...]
    _pack_rows(h2, h2_ref)
    lg_ref[...] = _dot_nt(wr_ref[...], h2) + br_ref[...]


def _resident(shape, nargs):
    return pl.BlockSpec(shape, lambda *_: (0,) * len(shape), pipeline_mode=pl.Buffered(1))


def _merge_stage(x, o_a, o_b, ob_time_major, proj, mw, tm):
    w_up_a, w_up_b, w_out, g_ffn, w_rt, b_r = mw
    B, T, D = x.shape
    W = o_a.shape[-1]
    E = w_rt.shape[0]
    assert T % tm == 0
    nt = T // tm
    tok = lambda width, col=0: pl.BlockSpec((None, tm, width), lambda b, i, col=col: (b, i, col))
    if ob_time_major:
        o_b = o_b.reshape(T, B * W)
        ob_spec = pl.BlockSpec((tm, W), lambda b, i: (i, b))
    else:
        ob_spec = tok(W)
    return pl.pallas_call(
        _merge_body,
        grid=(B, nt),
        in_specs=[tok(D), tok(W), ob_spec, tok(D, COL_GA), tok(D, COL_GB),
                  _resident((W, D), 2), _resident((W, D), 2), _resident((D, D), 2),
                  _resident((1, D), 2), _resident((E, D), 2), _resident((E, 1), 2)],
        out_specs=[tok(D), pl.BlockSpec((tm * PACK_ROWS, V7X_LANES), lambda b, i: (b * nt + i, 0)),
                   pl.BlockSpec((E, tm), lambda b, i: (0, b * nt + i))],
        out_shape=[jax.ShapeDtypeStruct((B, T, D), f32),
                   jax.ShapeDtypeStruct((B * T * PACK_ROWS, V7X_LANES), u32),
                   jax.ShapeDtypeStruct((E, B * T), f32)],
        compiler_params=_cparams(("parallel", "parallel")),
        name="merge_outproj",
    )(x, o_a, o_b, proj, proj, w_up_a, w_up_b, w_out, g_ffn, w_rt, b_r)


ROUTE_TILE = 512


def _route_body(lg_ref, c0_ref, idx_ref, gate_ref, pos_ref, cnt_ref, carry_ref):
    i = pl.program_id(0)
    E, tt = lg_ref.shape

    @pl.when(i == 0)
    def _():
        carry_ref[...] = c0_ref[...]

    l = lg_ref[...]
    e_iota = lax.broadcasted_iota(i32, (E, tt), 0)
    vals, sels = [], []
    for r in range(TOP_K):
        m = jnp.max(l, axis=0, keepdims=True)
        idx = jnp.min(jnp.where(l == m, e_iota, E), axis=0, keepdims=True)
        sel = e_iota == idx
        l = jnp.where(sel, -jnp.inf, l)
        idx_ref[r:r + 1, :] = idx
        vals.append(m)
        sels.append(sel)
    ex = [jnp.exp(v - vals[0]) for v in vals]
    tot = ex[0] + ex[1] + ex[2] + ex[3]
    for r in range(TOP_K):
        gate_ref[r:r + 1, :] = ex[r] / tot
    member = sels[0] | sels[1] | sels[2] | sels[3]
    mb = jnp.where(member, 1.0, 0.0)
    ti = lax.broadcasted_iota(i32, (tt, tt), 0)
    tj = lax.broadcasted_iota(i32, (tt, tt), 1)
    before = (ti < tj).astype(bf16)
    rank = carry_ref[:, 0:1] + jnp.dot(mb.astype(bf16), before, preferred_element_type=f32)
    for r in range(TOP_K):
        pos_ref[r:r + 1, :] = jnp.sum(jnp.where(sels[r], rank, 0.0), axis=0, keepdims=True).astype(i32)
    carry_ref[...] = carry_ref[...] + jnp.sum(mb, axis=1, keepdims=True)
    cnt_ref[...] = carry_ref[...]


def _route(logits_t, count0):
    E, N = logits_t.shape
    tt = min(ROUTE_TILE, N)
    assert N % tt == 0
    tokspec = pl.BlockSpec((TOP_K, tt), lambda i: (0, i))
    cspec = pl.BlockSpec((E, V7X_LANES), lambda i: (0, 0))
    return pl.pallas_call(
        _route_body,
        grid=(N // tt,),
        in_specs=[pl.BlockSpec((E, tt), lambda i: (0, i)), cspec],
        out_specs=[tokspec, tokspec, tokspec, cspec],
        out_shape=[jax.ShapeDtypeStruct((TOP_K, N), i32), jax.ShapeDtypeStruct((TOP_K, N), f32),
                   jax.ShapeDtypeStruct((TOP_K, N), i32), jax.ShapeDtypeStruct((E, V7X_LANES), f32)],
        scratch_shapes=[pltpu.VMEM((E, V7X_LANES), f32)],
        compiler_params=_cparams(("arbitrary",)),
        name="moe_route",
    )(logits_t, count0)


def _dest_body(idx_ref, pos_ref, start_ref, dest_ref):
    E = start_ref.shape[0]
    tt = idx_ref.shape[1]
    e_iota = lax.broadcasted_iota(i32, (E, tt), 0)
    start = start_ref[:, 0:1]
    for r in range(TOP_K):
        base = jnp.sum(jnp.where(e_iota == idx_ref[r:r + 1, :], start, 0.0), axis=0, keepdims=True)
        dest_ref[r:r + 1, :] = base.astype(i32) + pos_ref[r:r + 1, :]


def _dest(idx, pos, pad_start):
    N = idx.shape[1]
    E = pad_start.shape[0]
    tt = min(ROUTE_TILE, N)
    tokspec = pl.BlockSpec((TOP_K, tt), lambda i: (0, i))
    start = jnp.broadcast_to(pad_start.astype(f32)[:, None], (E, V7X_LANES))
    return pl.pallas_call(
        _dest_body,
        grid=(N // tt,),
        in_specs=[tokspec, tokspec, pl.BlockSpec((E, V7X_LANES), lambda i: (0, 0))],
        out_specs=tokspec,
        out_shape=jax.ShapeDtypeStruct((TOP_K, N), i32),
        compiler_params=_cparams(("parallel",)),
        name="moe_dest",
    )(idx, pos, start)


ZERO_ROWS = 256
DMA_PRIORITIES = 2
DMA_LOOP_UNROLL = 8


def _scatter_body(zero_fill, block_rows, zoff_ref, zflag_ref, dest_ref, h_ref, *rest):
    if zero_fill:
        xs_ref, zeros_ref, sem = rest
    else:
        _, xs_ref, zeros_ref, sem = rest
    i = pl.program_id(0)
    tt = dest_ref.shape[1]

    def packed(ref, row, n):
        return ref.at[pl.ds(pl.multiple_of(row * PACK_ROWS, PACK_ROWS), n * PACK_ROWS)]

    def zero_copy(e, piece):
        return pltpu.make_async_copy(
            zeros_ref, packed(xs_ref, zoff_ref[e] + piece * ZERO_ROWS, ZERO_ROWS), sem.at[0])

    if zero_fill:
        @pl.when(i == 0)
        def _():
            zeros_ref[...] = jnp.zeros_like(zeros_ref)
            n_e = zoff_ref.shape[0]
            for phase in ("start", "wait"):
                def per_expert(e, carry, phase=phase):
                    @pl.when(zflag_ref[e] > 0)
                    def _():
                        for piece in range(block_rows // ZERO_ROWS):
                            cp = zero_copy(e, piece)
                            cp.start() if phase == "start" else cp.wait()
                    return carry
                lax.fori_loop(0, n_e, per_expert, 0)

    def row_copy(t, r):
        return pltpu.make_async_copy(packed(h_ref, t, 1), packed(xs_ref, dest_ref[r, t], 1), sem.at[1])

    def start(t, carry):
        for r in range(TOP_K):
            row_copy(t, r).start(priority=r % DMA_PRIORITIES)
        return carry

    def wait(t, carry):
        for r in range(TOP_K):
            row_copy(t, r).wait()
        return carry

    lax.fori_loop(0, tt, start, 0, unroll=DMA_LOOP_UNROLL)
    lax.fori_loop(0, tt, wait, 0, unroll=DMA_LOOP_UNROLL)


def _scatter(h2, dest, zoff, zflag, n_rows, block_rows, xs_prev):
    N = h2.shape[0] // PACK_ROWS
    tt = min(ROUTE_TILE, N)
    zero_fill = xs_prev is None
    any_spec = pl.BlockSpec(memory_space=pl.ANY)
    in_specs = [pl.BlockSpec((TOP_K, tt), lambda i, *_: (0, i), memory_space=pltpu.SMEM),
                pl.BlockSpec((tt * PACK_ROWS, V7X_LANES), lambda i, *_: (i, 0))]
    args = [dest, h2]
    aliases = {}
    if not zero_fill:
        in_specs.append(any_spec)
        args.append(xs_prev)
        aliases = {4: 0}
    return pl.pallas_call(
        functools.partial(_scatter_body, zero_fill, block_rows),
        grid_spec=pltpu.PrefetchScalarGridSpec(
            num_scalar_prefetch=2, grid=(N // tt,), in_specs=in_specs, out_specs=any_spec,
            scratch_shapes=[pltpu.VMEM((ZERO_ROWS * PACK_ROWS, V7X_LANES), u32),
                            pltpu.SemaphoreType.DMA((2,))]),
        out_shape=jax.ShapeDtypeStruct((n_rows * PACK_ROWS, V7X_LANES), u32),
        input_output_aliases=aliases,
        compiler_params=_cparams(("arbitrary",)),
        name="moe_scatter",
    )(zoff, zflag, *args)


def _expert_body(nf, n_sub, bexp_ref, nused_ref, nvalid_ref, x_ref, wg_ref, wl_ref, bgl_ref, bll_ref,
                 wd_ref, bd_ref, o_ref, xb_ref, acc_ref):
    b = pl.program_id(0)
    j = pl.program_id(1)
    n_valid = nvalid_ref[b]
    active = n_valid > 0
    tm = xb_ref.shape[0]
    sub = tm // n_sub

    @pl.when(j == 0)
    def _():
        for c in range(PACK_ROWS):
            lo, hi = _unpack_chunk(x_ref, c, tm)
            xb_ref[:, c * PACK_SPAN:c * PACK_SPAN + V7X_LANES] = lo.astype(bf16)
            xb_ref[:, c * PACK_SPAN + V7X_LANES:(c + 1) * PACK_SPAN] = hi.astype(bf16)
        acc_ref[...] = jnp.where(active, jnp.broadcast_to(bd_ref[...], acc_ref.shape), 0.0)

    for n_rows, cond in ((tm, n_valid > sub), (sub, active & (n_valid <= sub))):
        @pl.when(cond)
        def _():
            rows = slice(0, n_rows)
            xb = xb_ref[rows, :]
            glu = jnp.dot(xb, wg_ref[...].astype(bf16), preferred_element_type=f32) + bgl_ref[...]
            lin = jnp.dot(xb, wl_ref[...].astype(bf16), preferred_element_type=f32) + bll_ref[...]
            glu = jnp.minimum(glu, SWIGLU_LIMIT)
            lin = jnp.clip(lin, -SWIGLU_LIMIT, SWIGLU_LIMIT)
            act = glu * jax.nn.sigmoid(SWIGLU_ALPHA * glu) * (lin + 1.0)
            acc_ref[rows, :] += jnp.dot(act.astype(bf16), wd_ref[...].astype(bf16),
                                        preferred_element_type=f32)

    @pl.when(j == nf - 1)
    def _():
        _pack_rows(acc_ref[...], o_ref)


def _experts(xs, bexp, nused, nvalid, w_gate_up, b_gate_up, w_down, b_down, tm, tf):
    n_rows = xs.shape[0] // PACK_ROWS
    E, D, F2 = w_gate_up.shape
    F = F2 // 2
    assert n_rows % tm == 0 and F % tf == 0
    nf = F // tf
    nb = n_rows // tm

    def blk(b, nused):
        return jnp.minimum(b, nused[0] - 1)

    def ftile(b, j, nused):
        return jnp.where(b < nused[0], j, nf - 1)

    b_gu = b_gate_up.reshape(E, 1, F2)
    b_d = b_down.reshape(E, 1, D)
    return pl.pallas_call(
        functools.partial(_expert_body, nf, EXPERT_SUB_BLOCKS),
        grid_spec=pltpu.PrefetchScalarGridSpec(
            num_scalar_prefetch=3, grid=(nb, nf),
            in_specs=[
                pl.BlockSpec((tm * PACK_ROWS, V7X_LANES), lambda b, j, be, nu, nv: (blk(b, nu), 0)),
                pl.BlockSpec((None, D, tf), lambda b, j, be, nu, nv: (be[blk(b, nu)], 0, ftile(b, j, nu))),
                pl.BlockSpec((None, D, tf),
                             lambda b, j, be, nu, nv: (be[blk(b, nu)], 0, nf + ftile(b, j, nu))),
                pl.BlockSpec((None, 1, tf), lambda b, j, be, nu, nv: (be[blk(b, nu)], 0, ftile(b, j, nu))),
                pl.BlockSpec((None, 1, tf),
                             lambda b, j, be, nu, nv: (be[blk(b, nu)], 0, nf + ftile(b, j, nu))),
                pl.BlockSpec((None, tf, D), lambda b, j, be, nu, nv: (be[blk(b, nu)], ftile(b, j, nu), 0)),
                pl.BlockSpec((None, 1, D), lambda b, j, be, nu, nv: (be[blk(b, nu)], 0, 0)),
            ],
            out_specs=pl.BlockSpec((tm * PACK_ROWS, V7X_LANES), lambda b, j, be, nu, nv: (b, 0)),
            scratch_shapes=[pltpu.VMEM((tm, D), bf16), pltpu.VMEM((tm, D), f32)]),
        out_shape=jax.ShapeDtypeStruct((n_rows * PACK_ROWS, V7X_LANES), u32),
        compiler_params=_cparams(("arbitrary", "arbitrary")),
        name="moe_experts",
    )(bexp, nused, nvalid, xs, w_gate_up, w_gate_up, b_gu, b_gu, w_down, b_d)


COMBINE_TILE = 256


def _combine_body(n, dcur_ref, dnext_ref, x1_ref, gate_ref, gf_ref, ys_ref, y_ref, buf_ref, sem):
    i = pl.program_id(0)
    tt = x1_ref.shape[0]

    def packed(row):
        return pl.ds(pl.multiple_of(row * PACK_ROWS, PACK_ROWS), PACK_ROWS)

    def row_copy(d_ref, slot, t, r):
        return pltpu.make_async_copy(ys_ref.at[packed(d_ref[r, t])], buf_ref.at[slot, r, packed(t)],
                                     sem.at[slot])

    def issue(d_ref, slot):
        def body(t, carry):
            for r in range(TOP_K):
                row_copy(d_ref, slot, t, r).start(priority=r % DMA_PRIORITIES)
            return carry
        lax.fori_loop(0, tt, body, 0, unroll=DMA_LOOP_UNROLL)

    @pl.when(i == 0)
    def _():
        issue(dcur_ref, 0)

    @pl.when(i + 1 < n)
    def _():
        issue(dnext_ref, (i + 1) % 2)

    slot = i % 2

    def wait(t, carry):
        for r in range(TOP_K):
            row_copy(dcur_ref, slot, t, r).wait()
        return carry
    lax.fori_loop(0, tt, wait, 0, unroll=DMA_LOOP_UNROLL)

    gates = gate_ref[...]
    parts = []
    for c in range(PACK_ROWS):
        lo_sum = x1_ref[:, c * PACK_SPAN:c * PACK_SPAN + V7X_LANES]
        hi_sum = x1_ref[:, c * PACK_SPAN + V7X_LANES:(c + 1) * PACK_SPAN]
        for r in range(TOP_K):
            lo, hi = _unpack_chunk(buf_ref.at[slot, r], c, tt)
            lo_sum = lo_sum + gates[:, r:r + 1] * lo
            hi_sum = hi_sum + gates[:, r:r + 1] * hi
        parts += [lo_sum, hi_sum]
    x = jnp.concatenate(parts, axis=1)
    ms = jnp.mean(x * x, axis=-1, keepdims=True)
    y_ref[...] = x * lax.rsqrt(ms + RMS_EPS) * gf_ref[...]


def _combine(x1, gates_tm, dest, ys, g_final):
    N, D = x1.shape
    tt = min(COMBINE_TILE, N)
    n = N // tt
    smem = lambda f: pl.BlockSpec((TOP_K, tt), f, memory_space=pltpu.SMEM)
    return pl.pallas_call(
        functools.partial(_combine_body, n),
        grid=(n,),
        in_specs=[smem(lambda i: (0, i)), smem(lambda i: (0, jnp.minimum(i + 1, n - 1))),
                  pl.BlockSpec((tt, D), lambda i: (i, 0)), pl.BlockSpec((tt, TOP_K), lambda i: (i, 0)),
                  pl.BlockSpec((1, D), lambda i: (0, 0)), pl.BlockSpec(memory_space=pl.ANY)],
        out_specs=pl.BlockSpec((tt, D), lambda i: (i, 0)),
        out_shape=jax.ShapeDtypeStruct((N, D), f32),
        scratch_shapes=[pltpu.VMEM((2, TOP_K, tt * PACK_ROWS, V7X_LANES), u32),
                        pltpu.SemaphoreType.DMA((2,))],
        compiler_params=_cparams(("arbitrary",)),
        name="moe_combine",
    )(dest, dest, x1, gates_tm, g_final.reshape(1, D), ys)


def _moe(h2_groups, logit_groups, x1_groups, ew, g_final, tm, tf):
    w_gate_up, b_gate_up, w_down, b_down = ew
    E = w_gate_up.shape[0]
    routed = []
    count = jnp.zeros((E, V7X_LANES), f32)
    for lg in logit_groups:
        idx, gate, pos, count = _route(lg, count)
        routed.append((idx, gate, pos))
    total = sum(lg.shape[1] for lg in logit_groups) * TOP_K
    n_blocks = -(-(total + E * (tm - 1)) // tm)
    n_rows = n_blocks * tm
    counts = count[:, 0].astype(i32)
    padded = (counts + tm - 1) // tm * tm
    pad_end = jnp.cumsum(padded)
    pad_start = pad_end - padded
    n_used = pad_end[-1] // tm
    blocks = jnp.minimum(jnp.arange(n_blocks, dtype=i32), n_used - 1)
    bexp = jnp.minimum(jnp.sum(blocks[:, None] * tm >= pad_end[None, :], axis=1), E - 1).astype(i32)
    all_blocks = jnp.arange(n_blocks, dtype=i32)
    zoff = jnp.concatenate([jnp.maximum(pad_end - tm, 0).astype(i32), all_blocks * tm])
    zflag = jnp.concatenate([counts > 0, all_blocks >= n_used]).astype(i32)
    xs = None
    dests = []
    for h2, (idx, gate, pos) in zip(h2_groups, routed):
        dest = _dest(idx, pos, pad_start)
        dests.append(dest)
        xs = _scatter(h2, dest, zoff, zflag, n_rows, tm, xs)
    row0 = all_blocks * tm
    nvalid = jnp.clip(pad_start[bexp] + counts[bexp] - row0, 0, tm)
    nvalid = jnp.where(all_blocks < n_used, nvalid, 0).astype(i32)
    ys = _experts(xs, bexp, n_used.reshape(1).astype(i32), nvalid, w_gate_up, b_gate_up, w_down, b_down,
                  tm, tf)
    return [_combine(x1, gate.T, dest, ys, g_final)
            for x1, (idx, gate, pos), dest in zip(x1_groups, routed, dests)]


PROJ_TILE_M = 1024
PROJ_TILE_N = 1920
MERGE_TILE_M = 256
EXPERT_TILE_M = 1024
EXPERT_TILE_F = 256
EXPERT_SUB_BLOCKS = 2


def _mixers(x, state, norm_g, w_main, w_u, rw, sw, mw):
    B, T, D = x.shape
    W = w_u.shape[1]
    shift_w = 3 * W + W_LORA + A_LORA + G_LORA
    G, S = sw[0].shape[1] // 64, 64
    if state is None:
        s_rwkv0 = jnp.zeros((B, W // HEAD, HEAD, HEAD), f32)
        s5_re0 = jnp.zeros((B, W // 16, 64), f32)
        s5_im0 = jnp.zeros((B, W // 16, 64), f32)
        shift0 = jnp.zeros((B, shift_w), f32)
    else:
        s_rwkv0, s5_re0, s5_im0, shift0 = state
    per_batch = T % PROJ_TILE_M == 0
    xf = x if per_batch else x.reshape(1, B * T, D)
    tm = PROJ_TILE_M if per_batch else min(PROJ_TILE_M, B * T)
    proj = _norm_proj(xf, norm_g, w_main, tm, PROJ_TILE_N, False, bf16).reshape(B, T, MAIN_COLS)
    if per_batch:
        u_tb = _norm_proj(xf, norm_g, w_u, tm, W, True, f32)
    else:
        u_tb = jnp.swapaxes(_norm_proj(xf, norm_g, w_u, tm, W, False, f32).reshape(B, T, W), 0, 1)
    o_a, s_rwkv = _rwkv_stage(proj, shift0, s_rwkv0, rw)
    o_b, s5_re, s5_im = _s5_stage(u_tb, s5_re0, s5_im0, sw)
    if per_batch:
        x1, h2, logits_t = _merge_stage(x, o_a, o_b, True, proj, mw, MERGE_TILE_M)
    else:
        flat = lambda t: t.reshape(1, B * T, t.shape[-1])
        x1, h2, logits_t = _merge_stage(flat(x), flat(o_a), flat(jnp.swapaxes(o_b, 0, 1)), False,
                                        flat(proj), mw, min(MERGE_TILE_M, B * T))
    last = proj[:, -1, :].astype(f32)
    n_lora = W_LORA + A_LORA + G_LORA
    shift = jnp.concatenate([last[:, COL_R * W:(COL_R + 3) * W],
                             last[:, COL_LORA * LORA_PAD:COL_LORA * LORA_PAD + n_lora]], axis=1)
    return (x1.reshape(B * T, D), h2, logits_t), (s_rwkv, s5_re, s5_im, shift)


def kernel(x_prompt, x_sample, state_rwkv, state_s5_re, state_s5_im, state_shift, norm_mix_g, w_in, mu_shift, w0, w2, a0, a2, g2, k_k, k_a, r_k, lnx_g, lnx_b, lam_re, lam_im, log_dt, b_re, b_im, c_re, c_im, d_skip, w_glu, b_glu, w_up_a, w_up_b, w_out, norm_ffn_g, w_router, b_router, w_gate_up, b_gate_up, w_down, b_down, norm_final_g):
    assert w_in.shape[0] == 1, "single-layer trunk"
    D = x_prompt.shape[-1]
    W = w0.shape[-1]
    n_lora = W_LORA + A_LORA + G_LORA
    shift_w = 3 * W + n_lora
    w = w_in[0]
    ga0 = shift_w + W
    w_main = jnp.concatenate(
        [w[:, ga0:ga0 + 2 * D], w[:, :3 * W], w[:, 3 * W:shift_w],
         jnp.zeros((D, LORA_PAD - n_lora), w.dtype)], axis=1).astype(bf16)
    assert w_main.shape[1] == MAIN_COLS
    w_u = w[:, shift_w:shift_w + W].astype(bf16)
    rw = _rwkv_weights(mu_shift[0], w0[0], w2[0], a0[0], a2[0], g2[0], k_k[0], k_a[0], r_k[0],
                       lnx_g[0], lnx_b[0])
    sw = _s5_weights(lam_re[0], lam_im[0], log_dt[0], b_re[0], b_im[0], c_re[0], c_im[0],
                     d_skip[0], w_glu[0], b_glu[0])
    E = w_router.shape[-1]
    mw = (w_up_a[0].astype(bf16), w_up_b[0].astype(bf16), w_out[0].astype(bf16),
          norm_ffn_g[0].reshape(1, D).astype(f32), w_router[0].T.astype(bf16),
          b_router[0].reshape(E, 1).astype(f32))
    ew = (w_gate_up[0], b_gate_up[0], w_down[0], b_down[0])

    state_s = (state_rwkv[0], state_s5_re[0], state_s5_im[0], state_shift[0])
    tok_p, st_p = _mixers(x_prompt, None, norm_mix_g[0], w_main, w_u, rw, sw, mw)
    tok_s, st_s = _mixers(x_sample, state_s, norm_mix_g[0], w_main, w_u, rw, sw, mw)
    y_p, y_s = _moe([tok_p[1], tok_s[1]], [tok_p[2], tok_s[2]], [tok_p[0], tok_s[0]], ew, norm_final_g,
                    EXPERT_TILE_M, EXPERT_TILE_F)
    lead = lambda t: t[None]
    return (y_p.reshape(x_prompt.shape), y_s.reshape(x_sample.shape),
            lead(st_p[0]), lead(st_p[1]), lead(st_p[2]), lead(st_p[3]),
            lead(st_s[0]), lead(st_s[1]), lead(st_s[2]), lead(st_s[3]))
```

```python
import functools
import math

import jax
import jax.numpy as jnp
from jax import lax
from jax.experimental import pallas as pl
from jax.experimental.pallas import tpu as pltpu

f32 = jnp.float32
bf16 = jnp.bfloat16
i32 = jnp.int32
u32 = jnp.uint32

V7X_LANES = 128
V7X_SUBLANES = 8
V7X_VMEM_BYTES = 64 * 1024 * 1024
VMEM_LIMIT = 56 * 1024 * 1024

RMS_EPS = 1e-5
GN_EPS = 64e-5
HEAD = 64
HEAD_SHIFT = 6
HEADS_PER_GROUP = 4
GROUP = HEAD * HEADS_PER_GROUP
CHUNK = 64
W_LORA, A_LORA, G_LORA = 64, 64, 160
LORA_PAD = 512
TOP_K = 4
SWIGLU_LIMIT = 7.0
SWIGLU_ALPHA = 1.702


def _cparams(sem):
    return pltpu.CompilerParams(dimension_semantics=sem, vmem_limit_bytes=VMEM_LIMIT)


def _dot(a, b):
    return jnp.dot(a.astype(bf16), b.astype(bf16), preferred_element_type=f32)


def _dot_nt(a, b):
    return lax.dot_general(a.astype(bf16), b.astype(bf16), (((1,), (1,)), ((), ())),
                           preferred_element_type=f32)


def _dot_tn(a, b):
    return lax.dot_general(a.astype(bf16), b.astype(bf16), (((0,), (0,)), ((), ())),
                           preferred_element_type=f32)


def _split(x):
    hi = x.astype(bf16)
    lo = (x - hi.astype(f32)).astype(bf16)
    return hi, lo


def _dot_exact_rhs(a, b_exact):
    hi, lo = _split(a)
    return (jnp.dot(hi, b_exact, preferred_element_type=f32)
            + jnp.dot(lo, b_exact, preferred_element_type=f32))


def _dot_exact_lhs(a_exact, b):
    hi, lo = _split(b)
    return (jnp.dot(a_exact, hi, preferred_element_type=f32)
            + jnp.dot(a_exact, lo, preferred_element_type=f32))


def _norm_proj_body(x_ref, g_ref, w_ref, o_ref, h_ref):
    @pl.when(pl.program_id(2) == 0)
    def _():
        x = x_ref[...]
        ms = jnp.mean(x * x, axis=-1, keepdims=True)
        h_ref[...] = (x * lax.rsqrt(ms + RMS_EPS) * g_ref[...]).astype(bf16)

    o_ref[...] = jnp.dot(h_ref[...], w_ref[...], preferred_element_type=f32).astype(o_ref.dtype)


def _norm_proj(x, gain, w, tm, tn, time_major_out, out_dtype):
    B, T, D = x.shape
    N = w.shape[1]
    assert T % tm == 0 and N % tn == 0
    if time_major_out:
        out_shape = jax.ShapeDtypeStruct((T, B * N), out_dtype)
        out_spec = pl.BlockSpec((tm, tn), lambda b, i, n: (i, b * (N // tn) + n))
    else:
        out_shape = jax.ShapeDtypeStruct((B, T, N), out_dtype)
        out_spec = pl.BlockSpec((None, tm, tn), lambda b, i, n: (b, i, n))
    out = pl.pallas_call(
        _norm_proj_body,
        grid=(B, T // tm, N // tn),
        in_specs=[pl.BlockSpec((None, tm, D), lambda b, i, n: (b, i, 0)),
                  pl.BlockSpec((1, D), lambda b, i, n: (0, 0)),
                  pl.BlockSpec((D, tn), lambda b, i, n: (0, n))],
        out_specs=out_spec,
        out_shape=out_shape,
        scratch_shapes=[pltpu.VMEM((tm, D), bf16)],
        compiler_params=_cparams(("parallel", "parallel", "arbitrary")),
        name="norm_proj",
    )(x, gain.reshape(1, D), w)
    return out.reshape(T, B, N) if time_major_out else out


PV_MU_R, PV_MU_K, PV_MU_V, PV_W0, PV_A0, PV_KK, PV_KA, PV_RK, PV_LNG, PV_LNB = range(10)
PV_ROWS = 16
PREV_ROWS = 16
RWKV_CHUNKS_PER_STEP = 4


def _rwkv_body(nc, r_ref, k_ref, v_ref, l_ref, pr_ref, pk_ref, pv_ref, plo_ref,
               lr_ref, lk_ref, lv_ref, llo_ref, pvec_ref, mul_ref, wl_ref, s0_ref,
               o_ref, so_ref, s_ref):
    c = pl.program_id(1)
    C = CHUNK
    R, W = r_ref.shape
    n_chunks = R // C
    n_groups = W // GROUP

    @pl.when(c == 0)
    def _():
        for gq in range(n_groups):
            for h in range(HEADS_PER_GROUP):
                pieces = [jnp.zeros((HEAD, HEAD), f32)] * HEADS_PER_GROUP
                pieces[h] = s0_ref[gq * HEADS_PER_GROUP + h]
                s_ref[gq, h * HEAD:(h + 1) * HEAD, :] = jnp.concatenate(pieces, axis=1)

    row = lax.broadcasted_iota(i32, (R, 1), 0)

    def shifted(x_ref, prev_ref, last_ref, mu):
        x = x_ref[...].astype(f32)
        prev_row = jnp.where(c > 0, prev_ref[PREV_ROWS - 1:PREV_ROWS, :].astype(f32), last_ref[...])
        prev = jnp.where(row == 0, prev_row, pltpu.roll(x, shift=1, axis=0))
        return x + (prev - x) * mu

    pvec = pvec_ref[...]
    prm = lambda i: pvec[i:i + 1, :]
    r = shifted(r_ref, pr_ref, lr_ref, prm(PV_MU_R))
    k = shifted(k_ref, pk_ref, lk_ref, prm(PV_MU_K))
    v = shifted(v_ref, pv_ref, lv_ref, prm(PV_MU_V))
    xl = shifted(l_ref, plo_ref, llo_ref, mul_ref[...])

    lane_l = lax.broadcasted_iota(i32, (1, xl.shape[1]), 1)
    act = jnp.where(lane_l < W_LORA, jnp.tanh(xl),
                    jnp.where(lane_l < W_LORA + A_LORA, xl, jax.nn.sigmoid(xl)))
    lo = jnp.dot(act.astype(bf16), wl_ref[...], preferred_element_type=f32)
    z = -(prm(PV_W0) + lo[:, :W])
    w = -(jnp.maximum(z, 0.0) + jnp.log(1.0 + jnp.exp(-jnp.abs(z)))) - 0.5
    logd = -jnp.exp(w)
    a = jax.nn.sigmoid(prm(PV_A0) + lo[:, W:2 * W])
    g = lo[:, 2 * W:]

    gi = lax.broadcasted_iota(i32, (GROUP, GROUP), 0)
    gj = lax.broadcasted_iota(i32, (GROUP, GROUP), 1)
    same_head = (gi >> HEAD_SHIFT) == (gj >> HEAD_SHIFT)
    head_ones = jnp.where(same_head, 1.0, 0.0).astype(bf16)
    qt = lax.broadcasted_iota(i32, (C, GROUP), 0)
    qs = lax.broadcasted_iota(i32, (C, GROUP), 1) & (HEAD - 1)
    strict_q, incl_q = qt > qs, qt >= qs
    ci = lax.broadcasted_iota(i32, (R, R), 0)
    cj = lax.broadcasted_iota(i32, (R, R), 1)
    tri = jnp.where((ci >= cj) & ((ci >> HEAD_SHIFT) == (cj >> HEAD_SHIFT)), 1.0, 0.0).astype(bf16)
    lane_head = lax.broadcasted_iota(i32, (1, GROUP), 1) >> HEAD_SHIFT
    eye_quad = jnp.where(qt == qs, 1.0, 0.0)

    def head_sum(x, exact):
        parts = []
        for gq in range(n_groups):
            xg = x[:, gq * GROUP:(gq + 1) * GROUP]
            parts.append(_dot_exact_rhs(xg, head_ones) if exact
                         else jnp.dot(xg.astype(bf16), head_ones, preferred_element_type=f32))
        return jnp.concatenate(parts, axis=1)

    kk = k * prm(PV_KK)
    kk = kk * lax.rsqrt(jnp.maximum(head_sum(kk * kk, True), 1e-24))
    k2 = k * (1.0 + (a - 1.0) * prm(PV_KA))
    cl = _dot_exact_lhs(tri, logd)
    p_incl = jnp.exp(cl)
    p_excl = jnp.exp(cl - logd)
    p_inv = jnp.exp(-cl)
    rt = r * p_incl
    at = -kk * p_excl
    bt = kk * a * p_inv
    kt = k2 * p_inv
    bonus = head_sum(r * k2 * prm(PV_RK), False) * v

    def stack(x):
        return jnp.concatenate([jnp.where(lane_head == h, x, 0.0) for h in range(HEADS_PER_GROUP)], axis=0)

    def tile(x):
        return jnp.concatenate([x] * HEADS_PER_GROUP, axis=0)

    def block(x):
        return jnp.where(same_head, tile(x), 0.0)

    groups = range(n_groups)
    each = lambda f, *lists: [f(*xs) for xs in zip(*lists)]
    lanes = [slice(gq * GROUP, (gq + 1) * GROUP) for gq in groups]
    tiles = [(slice(ch * C, (ch + 1) * C), sl) for ch in range(n_chunks) for sl in lanes]
    ag, rg, bg, kg, vg = ([x[rows, sl] for rows, sl in tiles] for x in (at, rt, bt, kt, v))
    sa, sv = each(stack, ag), each(stack, vg)
    bd_t = lambda y: jnp.where(same_head, tile(y).T, 0.0)
    tb, tk = each(bd_t, bg), each(bd_t, kg)
    ab = each(lambda x, y: jnp.where(strict_q, _dot(x, y), 0.0), ag, tb)
    ak = each(lambda x, y: jnp.where(strict_q, _dot(x, y), 0.0), ag, tk)
    rb = each(lambda x, y: jnp.where(incl_q, _dot(x, y), 0.0), rg, tb)
    rk = each(lambda x, y: jnp.where(incl_q, _dot(x, y), 0.0), rg, tk)
    q4, qb = ab, each(block, ab)
    t4 = each(lambda q: eye_quad + q, q4)
    for _ in range(max(1, (C - 1).bit_length() - 1)):
        q4 = each(_dot, q4, qb)
        qb = each(block, q4)
        t4 = each(lambda t, q: t + _dot(t, q), t4, qb)
    w4 = each(_dot, t4, sa)
    akv = each(_dot, ak, sv)
    uv = each(lambda t, x: _dot(t, stack(x)), t4, akv)
    orv = each(_dot, rk, sv)
    state = [s_ref[gq] for gq in groups]
    o_rows = []
    for ch in range(n_chunks):
        of = lambda xs: xs[ch * n_groups:(ch + 1) * n_groups]
        u4 = each(lambda w_, s_, uv_: _dot_nt(w_, s_) + uv_, of(w4), state, of(uv))
        rs = each(_dot_nt, of(rg), state)
        upd = each(lambda u_, v_, b_, k_: _dot_tn(jnp.concatenate([u_, v_], axis=0),
                                                  jnp.concatenate([b_, k_], axis=0)),
                   u4, of(vg), of(bg), of(kg))
        o4 = each(lambda rs_, rb_, u_, orv_: rs_ + _dot(rb_, stack(u_)) + orv_,
                  rs, of(rb), u4, of(orv))
        p_end = p_incl[(ch + 1) * C - 1:(ch + 1) * C, :]
        state = [(s + jnp.where(same_head, d, 0.0)) * p_end[:, sl] for s, d, sl in zip(state, upd, lanes)]
        o_rows.append(jnp.concatenate(o4, axis=1))
    for gq in groups:
        s_ref[gq] = state[gq]
    o = jnp.concatenate(o_rows, axis=0)

    inv_n = 1.0 / HEAD
    mean = head_sum(o, False) * inv_n
    d = o - mean
    var = head_sum(d * d, False) * inv_n
    on = d * lax.rsqrt(var + GN_EPS)
    o_ref[...] = (on * prm(PV_LNG) + prm(PV_LNB) + bonus) * g

    @pl.when(c == nc - 1)
    def _():
        for gq in range(n_groups):
            for h in range(HEADS_PER_GROUP):
                hs = slice(h * HEAD, (h + 1) * HEAD)
                so_ref[gq * HEADS_PER_GROUP + h] = s_ref[gq, hs, hs]


def _rwkv(proj, col_r, col_lora, last_rkv, last_lora, pvec, mu_lora, w_lora, s0):
    B, T, _ = proj.shape
    W = pvec.shape[1]
    C = CHUNK
    assert T % C == 0 and C == HEAD and HEAD == 1 << HEAD_SHIFT
    n_groups = W // GROUP
    R = C * RWKV_CHUNKS_PER_STEP if T % (C * RWKV_CHUNKS_PER_STEP) == 0 else C
    rb = R // PREV_ROWS

    def cur(col, width):
        return pl.BlockSpec((None, R, width), lambda b, c, col=col: (b, c, col))

    def prev(col, width):
        return pl.BlockSpec((None, PREV_ROWS, width),
                            lambda b, c, col=col: (b, jnp.maximum(c * rb - 1, 0), col))

    def last(col, width):
        return pl.BlockSpec((None, 1, width), lambda b, c, col=col: (b, 0, col))

    const = lambda shape: pl.BlockSpec(shape, lambda b, c: (0,) * len(shape))
    H = W // HEAD
    state_spec = pl.BlockSpec((None, H, HEAD, HEAD), lambda b, c: (b, 0, 0, 0))
    return pl.pallas_call(
        functools.partial(_rwkv_body, T // R),
        grid=(B, T // R),
        in_specs=[cur(col_r, W), cur(col_r + 1, W), cur(col_r + 2, W), cur(col_lora, LORA_PAD),
                  prev(col_r, W), prev(col_r + 1, W), prev(col_r + 2, W), prev(col_lora, LORA_PAD),
                  last(0, W), last(1, W), last(2, W), last(0, LORA_PAD),
                  const((PV_ROWS, W)), const((1, LORA_PAD)), const((LORA_PAD, 3 * W)), state_spec],
        out_specs=[pl.BlockSpec((None, R, W), lambda b, c: (b, c, 0)), state_spec],
        out_shape=[jax.ShapeDtypeStruct((B, T, W), f32),
                   jax.ShapeDtypeStruct((B, H, HEAD, HEAD), f32)],
        scratch_shapes=[pltpu.VMEM((n_groups, GROUP, GROUP), f32)],
        compiler_params=_cparams(("parallel", "arbitrary")),
        name="rwkv7_chunked",
    )(proj, proj, proj, proj, proj, proj, proj, proj,
      last_rkv, last_rkv, last_rkv, last_lora, pvec, mu_lora, w_lora, s0)


COL_GA, COL_GB = 0, 1
COL_R = 4
COL_LORA = 14
MAIN_COLS = 7680


def _rwkv_weights(mu_shift, w0, w2, a0, a2, g2, k_k, k_a, r_k, lnx_g, lnx_b):
    W = w0.shape[0]
    rows = [mu_shift[:W], mu_shift[W:2 * W], mu_shift[2 * W:3 * W], w0, a0, k_k, k_a,
            r_k.reshape(W), lnx_g, lnx_b]
    pvec = jnp.zeros((PV_ROWS, W), f32).at[:len(rows)].set(jnp.stack(rows).astype(f32))
    n_lora = W_LORA + A_LORA + G_LORA
    mu_lora = jnp.zeros((1, LORA_PAD), f32).at[0, :n_lora].set(mu_shift[3 * W:])
    w_lora = jnp.zeros((LORA_PAD, 3 * W), f32)
    w_lora = w_lora.at[:W_LORA, :W].set(w2)
    w_lora = w_lora.at[W_LORA:W_LORA + A_LORA, W:2 * W].set(a2)
    w_lora = w_lora.at[W_LORA + A_LORA:n_lora, 2 * W:].set(g2)
    return pvec, mu_lora, w_lora.astype(bf16)


def _rwkv_stage(proj, shift0, s0, rw):
    pvec, mu_lora, w_lora = rw
    B = proj.shape[0]
    W = pvec.shape[1]
    last_rkv = shift0[:, None, :3 * W].astype(f32)
    last_lora = jnp.zeros((B, 1, LORA_PAD), f32).at[:, 0, :shift0.shape[1] - 3 * W].set(shift0[:, 3 * W:])
    return _rwkv(proj, COL_R, COL_LORA, last_rkv, last_lora, pvec, mu_lora, w_lora, s0.astype(f32))


S5_BLOCK_IN = 128
S5_BLOCK_STATE = 512


def _s5_prep_body(lr_ref, li_ref, ldt_ref, br_ref, bi_ref, lbr_ref, lbi_ref, bbr_ref, bbi_ref):
    lr, li = lr_ref[...], li_ref[...]
    dt = jnp.exp(ldt_ref[...])
    mag = jnp.exp(lr * dt)
    er, ei = mag * jnp.cos(li * dt), mag * jnp.sin(li * dt)
    lbr_ref[...] = er
    lbi_ref[...] = ei
    den = lr * lr + li * li
    cr = ((er - 1.0) * lr + ei * li) / den
    ci = (ei * lr - (er - 1.0) * li) / den
    br, bi = br_ref[...], bi_ref[...]
    bbr_ref[...] = cr * br - ci * bi
    bbi_ref[...] = cr * bi + ci * br


def _s5_weights(lam_re, lam_im, log_dt, b_re, b_im, c_re, c_im, d_skip, w_glu, b_glu):
    G, S = lam_re.shape
    P = b_re.shape[-1]
    n = G * S
    col = lambda t: t.reshape(n, 1).astype(f32)
    ldt = jnp.broadcast_to(log_dt[:, None], (G, S))
    shapes = [jax.ShapeDtypeStruct((n, 1), f32)] * 2 + [jax.ShapeDtypeStruct((n, P), f32)] * 2
    lbr, lbi, bbr, bbi = pl.pallas_call(_s5_prep_body, out_shape=shapes, name="s5_discretise")(
        col(lam_re), col(lam_im), col(ldt), b_re.reshape(n, P).astype(f32), b_im.reshape(n, P).astype(f32))
    gpb = S5_BLOCK_IN // P
    nb = G // gpb
    eye = jnp.eye(gpb, dtype=f32)

    def in_block(bb):
        t = bb.reshape(nb, gpb, S, P)
        return jnp.einsum('qgsp,gh->qgphs', t, eye).reshape(nb, gpb * P, gpb * S)

    def out_block(cc):
        t = cc.astype(f32).reshape(nb, gpb, P, S)
        return jnp.einsum('qgps,gh->qgshp', t, eye).reshape(nb, gpb * S, gpb * P)

    wb = jnp.concatenate([in_block(bbr), in_block(bbi)], axis=2).astype(bf16)
    wc = jnp.concatenate([out_block(c_re), -out_block(c_im)], axis=1).astype(bf16)
    W = G * P
    return (lbr.reshape(1, n), lbi.reshape(1, n), wb, wc, d_skip.reshape(1, W).astype(f32),
            w_glu.astype(bf16), b_glu.reshape(1, W).astype(f32))


def _s5_body(n_steps, u_ref, x0r_ref, x0i_ref, lbr_ref, lbi_ref, wb_ref, wc_ref, d_ref, wg_ref, bg_ref,
             o_ref, xr_out, xi_out, bur_ref, bui_ref, xr_ref, xi_ref, y_ref):
    i = pl.program_id(1)
    Tc, Bs, W = u_ref.shape
    rows = Tc * Bs
    nb = wb_ref.shape[0]
    ns = S5_BLOCK_STATE

    @pl.when(i == 0)
    def _():
        xr_ref[...] = x0r_ref[...]
        xi_ref[...] = x0i_ref[...]

    u = u_ref[...].reshape(rows, W)
    for q in range(nb):
        bu = jnp.dot(u[:, q * S5_BLOCK_IN:(q + 1) * S5_BLOCK_IN].astype(bf16), wb_ref[q],
                     preferred_element_type=f32)
        bur_ref[:, q * ns:(q + 1) * ns] = bu[:, :ns]
        bui_ref[:, q * ns:(q + 1) * ns] = bu[:, ns:]

    for q in range(nb):
        sl = slice(q * ns, (q + 1) * ns)
        lr = jnp.broadcast_to(lbr_ref[:, sl], (Bs, ns))
        li = jnp.broadcast_to(lbi_ref[:, sl], (Bs, ns))

        def step(t, carry):
            xr, xi = carry
            rs = pl.ds(pl.multiple_of(t * Bs, Bs), Bs)
            nr = lr * xr - li * xi + bur_ref[rs, sl]
            ni = lr * xi + li * xr + bui_ref[rs, sl]
            bur_ref[rs, sl] = nr
            bui_ref[rs, sl] = ni
            return nr, ni

        xr, xi = lax.fori_loop(0, Tc, step, (xr_ref[:, sl], xi_ref[:, sl]), unroll=True)
        xr_ref[:, sl] = xr
        xi_ref[:, sl] = xi
        y_ref[:, q * S5_BLOCK_IN:(q + 1) * S5_BLOCK_IN] = (
            jnp.dot(bur_ref[:, sl].astype(bf16), wc_ref[q, :ns, :], preferred_element_type=f32)
            + jnp.dot(bui_ref[:, sl].astype(bf16), wc_ref[q, ns:, :], preferred_element_type=f32))

    y = y_ref[...] + d_ref[...] * u
    y = 0.5 * y * (1.0 + jnp.tanh(math.sqrt(2.0 / math.pi) * (y + 0.044715 * (y * y * y))))
    z = jnp.dot(y.astype(bf16), wg_ref[...], preferred_element_type=f32) + bg_ref[...]
    o_ref[...] = (y * jax.nn.sigmoid(z)).reshape(Tc, Bs, W)

    @pl.when(i == n_steps - 1)
    def _():
        xr_out[...] = xr_ref[...]
        xi_out[...] = xi_ref[...]


def _s5_stage(u_tb, x0_re, x0_im, sw):
    lbr, lbi, wb, wc, d_skip, w_glu, b_glu = sw
    T, B, W = u_tb.shape
    G, S = x0_re.shape[1:]
    n = G * S
    Bs = V7X_SUBLANES
    Tc = min(64, T)
    assert B % Bs == 0 and T % Tc == 0
    rows = Tc * Bs
    const = lambda shape: pl.BlockSpec(shape, lambda j, i: (0,) * len(shape))
    state_spec = pl.BlockSpec((Bs, n), lambda j, i: (j, 0))
    o, xr, xi = pl.pallas_call(
        functools.partial(_s5_body, T // Tc),
        grid=(B // Bs, T // Tc),
        in_specs=[pl.BlockSpec((Tc, Bs, W), lambda j, i: (i, j, 0)), state_spec, state_spec,
                  const((1, n)), const((1, n)), const(wb.shape), const(wc.shape),
                  const((1, W)), const((W, W)), const((1, W))],
        out_specs=[pl.BlockSpec((Tc, Bs, W), lambda j, i: (i, j, 0)), state_spec, state_spec],
        out_shape=[jax.ShapeDtypeStruct((T, B, W), f32),
                   jax.ShapeDtypeStruct((B, n), f32), jax.ShapeDtypeStruct((B, n), f32)],
        scratch_shapes=[pltpu.VMEM((rows, n), f32), pltpu.VMEM((rows, n), f32),
                        pltpu.VMEM((Bs, n), f32), pltpu.VMEM((Bs, n), f32), pltpu.VMEM((rows, W), f32)],
        compiler_params=_cparams(("parallel", "arbitrary")),
        name="s5_scan",
    )(u_tb, x0_re.reshape(B, n).astype(f32), x0_im.reshape(B, n).astype(f32),
      lbr, lbi, wb, wc, d_skip, w_glu, b_glu)
    return o, xr.reshape(B, G, S), xi.reshape(B, G, S)


PACK_ROWS = V7X_SUBLANES
PACK_SPAN = 2 * V7X_LANES
HIGH_HALF = 0xFFFF0000


def _pack_rows(x, out_ref):
    n, D = x.shape
    assert D == PACK_ROWS * PACK_SPAN
    bits = lax.bitcast_convert_type(x.astype(bf16).astype(f32), u32)
    for c in range(PACK_ROWS):
        lo = bits[:, c * PACK_SPAN:c * PACK_SPAN + V7X_LANES]
        hi = bits[:, c * PACK_SPAN + V7X_LANES:(c + 1) * PACK_SPAN]
        out_ref[pl.ds(c, n, stride=PACK_ROWS), :] = (lo >> 16) | (hi & jnp.uint32(HIGH_HALF))


def _unpack_chunk(ref, c, n):
    words = ref[pl.ds(c, n, stride=PACK_ROWS), :]
    lo = lax.bitcast_convert_type(words << 16, f32)
    hi = lax.bitcast_convert_type(words & jnp.uint32(HIGH_HALF), f32)
    return lo, hi


def _merge_body(x_ref, oa_ref, ob_ref, ga_ref, gb_ref, wa_ref, wb_ref, wo_ref, gf_ref, wr_ref, br_ref,
                x1_ref, h2_ref, lg_ref):
    ya = jnp.dot(oa_ref[...].astype(bf16), wa_ref[...], preferred_element_type=f32)
    yb = jnp.dot(ob_ref[...].astype(bf16), wb_ref[...], preferred_element_type=f32)
    merged = (jax.nn.sigmoid(ga_ref[...].astype(f32)) * ya
              + jax.nn.sigmoid(gb_ref[...].astype(f32)) * yb)
    x1 = x_ref[...] + jnp.dot(merged.astype(bf16), wo_ref[...], preferred_element_type=f32)
    x1_ref[...] = x1
    ms = jnp.mean(x1 * x1, axis=-1, keepdims=True)
    h2 = x1 * lax.rsqrt(ms + RMS_EPS) * gf_ref[...]
    _pack_rows(h2, h2_ref)
    lg_ref[...] = _dot_nt(wr_ref[...], h2) + br_ref[...]


def _resident(shape, nargs):
    return pl.BlockSpec(shape, lambda *_: (0,) * len(shape), pipeline_mode=pl.Buffered(1))


def _merge_stage(x, o_a, o_b, ob_time_major, proj, mw, tm):
    w_up_a, w_up_b, w_out, g_ffn, w_rt, b_r = mw
    B, T, D = x.shape
    W = o_a.shape[-1]
    E = w_rt.shape[0]
    assert T % tm == 0
    nt = T // tm
    tok = lambda width, col=0: pl.BlockSpec((None, tm, width), lambda b, i, col=col: (b, i, col))
    if ob_time_major:
        o_b = o_b.reshape(T, B * W)
        ob_spec = pl.BlockSpec((tm, W), lambda b, i: (i, b))
    else:
        ob_spec = tok(W)
    return pl.pallas_call(
        _merge_body,
        grid=(B, nt),
        in_specs=[tok(D), tok(W), ob_spec, tok(D, COL_GA), tok(D, COL_GB),
                  _resident((W, D), 2), _resident((W, D), 2), _resident((D, D), 2),
                  _resident((1, D), 2), _resident((E, D), 2), _resident((E, 1), 2)],
        out_specs=[tok(D), pl.BlockSpec((tm * PACK_ROWS, V7X_LANES), lambda b, i: (b * nt + i, 0)),
                   pl.BlockSpec((E, tm), lambda b, i: (0, b * nt + i))],
        out_shape=[jax.ShapeDtypeStruct((B, T, D), f32),
                   jax.ShapeDtypeStruct((B * T * PACK_ROWS, V7X_LANES), u32),
                   jax.ShapeDtypeStruct((E, B * T), f32)],
        compiler_params=_cparams(("parallel", "parallel")),
        name="merge_outproj",
    )(x, o_a, o_b, proj, proj, w_up_a, w_up_b, w_out, g_ffn, w_rt, b_r)


ROUTE_TILE = 512


def _route_body(lg_ref, c0_ref, idx_ref, gate_ref, pos_ref, cnt_ref, carry_ref):
    i = pl.program_id(0)
    E, tt = lg_ref.shape

    @pl.when(i == 0)
    def _():
        carry_ref[...] = c0_ref[...]

    l = lg_ref[...]
    e_iota = lax.broadcasted_iota(i32, (E, tt), 0)
    vals, sels = [], []
    for r in range(TOP_K):
        m = jnp.max(l, axis=0, keepdims=True)
        idx = jnp.min(jnp.where(l == m, e_iota, E), axis=0, keepdims=True)
        sel = e_iota == idx
        l = jnp.where(sel, -jnp.inf, l)
        idx_ref[r:r + 1, :] = idx
        vals.append(m)
        sels.append(sel)
    ex = [jnp.exp(v - vals[0]) for v in vals]
    tot = ex[0] + ex[1] + ex[2] + ex[3]
    for r in range(TOP_K):
        gate_ref[r:r + 1, :] = ex[r] / tot
    member = sels[0] | sels[1] | sels[2] | sels[3]
    mb = jnp.where(member, 1.0, 0.0)
    ti = lax.broadcasted_iota(i32, (tt, tt), 0)
    tj = lax.broadcasted_iota(i32, (tt, tt), 1)
    before = (ti < tj).astype(bf16)
    rank = carry_ref[:, 0:1] + jnp.dot(mb.astype(bf16), before, preferred_element_type=f32)
    for r in range(TOP_K):
        pos_ref[r:r + 1, :] = jnp.sum(jnp.where(sels[r], rank, 0.0), axis=0, keepdims=True).astype(i32)
    carry_ref[...] = carry_ref[...] + jnp.sum(mb, axis=1, keepdims=True)
    cnt_ref[...] = carry_ref[...]


def _route(logits_t, count0):
    E, N = logits_t.shape
    tt = min(ROUTE_TILE, N)
    assert N % tt == 0
    tokspec = pl.BlockSpec((TOP_K, tt), lambda i: (0, i))
    cspec = pl.BlockSpec((E, V7X_LANES), lambda i: (0, 0))
    return pl.pallas_call(
        _route_body,
        grid=(N // tt,),
        in_specs=[pl.BlockSpec((E, tt), lambda i: (0, i)), cspec],
        out_specs=[tokspec, tokspec, tokspec, cspec],
        out_shape=[jax.ShapeDtypeStruct((TOP_K, N), i32), jax.ShapeDtypeStruct((TOP_K, N), f32),
                   jax.ShapeDtypeStruct((TOP_K, N), i32), jax.ShapeDtypeStruct((E, V7X_LANES), f32)],
        scratch_shapes=[pltpu.VMEM((E, V7X_LANES), f32)],
        compiler_params=_cparams(("arbitrary",)),
        name="moe_route",
    )(logits_t, count0)


def _dest_body(idx_ref, pos_ref, start_ref, dest_ref):
    E = start_ref.shape[0]
    tt = idx_ref.shape[1]
    e_iota = lax.broadcasted_iota(i32, (E, tt), 0)
    start = start_ref[:, 0:1]
    for r in range(TOP_K):
        base = jnp.sum(jnp.where(e_iota == idx_ref[r:r + 1, :], start, 0.0), axis=0, keepdims=True)
        dest_ref[r:r + 1, :] = base.astype(i32) + pos_ref[r:r + 1, :]


def _dest(idx, pos, pad_start):
    N = idx.shape[1]
    E = pad_start.shape[0]
    tt = min(ROUTE_TILE, N)
    tokspec = pl.BlockSpec((TOP_K, tt), lambda i: (0, i))
    start = jnp.broadcast_to(pad_start.astype(f32)[:, None], (E, V7X_LANES))
    return pl.pallas_call(
        _dest_body,
        grid=(N // tt,),
        in_specs=[tokspec, tokspec, pl.BlockSpec((E, V7X_LANES), lambda i: (0, 0))],
        out_specs=tokspec,
        out_shape=jax.ShapeDtypeStruct((TOP_K, N), i32),
        compiler_params=_cparams(("parallel",)),
        name="moe_dest",
    )(idx, pos, start)


ZERO_ROWS = 256
DMA_PRIORITIES = 2
DMA_LOOP_UNROLL = 8


def _scatter_body(zero_fill, block_rows, zoff_ref, zflag_ref, dest_ref, h_ref, *rest):
    if zero_fill:
        xs_ref, zeros_ref, sem = rest
    else:
        _, xs_ref, zeros_ref, sem = rest
    i = pl.program_id(0)
    tt = dest_ref.shape[1]

    def packed(ref, row, n):
        return ref.at[pl.ds(pl.multiple_of(row * PACK_ROWS, PACK_ROWS), n * PACK_ROWS)]

    def zero_copy(e, piece):
        return pltpu.make_async_copy(
            zeros_ref, packed(xs_ref, zoff_ref[e] + piece * ZERO_ROWS, ZERO_ROWS), sem.at[0])

    if zero_fill:
        @pl.when(i == 0)
        def _():
            zeros_ref[...] = jnp.zeros_like(zeros_ref)
            n_e = zoff_ref.shape[0]
            for phase in ("start", "wait"):
                def per_expert(e, carry, phase=phase):
                    @pl.when(zflag_ref[e] > 0)
                    def _():
                        for piece in range(block_rows // ZERO_ROWS):
                            cp = zero_copy(e, piece)
                            cp.start() if phase == "start" else cp.wait()
                    return carry
                lax.fori_loop(0, n_e, per_expert, 0)

    def row_copy(t, r):
        return pltpu.make_async_copy(packed(h_ref, t, 1), packed(xs_ref, dest_ref[r, t], 1), sem.at[1])

    def start(t, carry):
        for r in range(TOP_K):
            row_copy(t, r).start(priority=r % DMA_PRIORITIES)
        return carry

    def wait(t, carry):
        for r in range(TOP_K):
            row_copy(t, r).wait()
        return carry

    lax.fori_loop(0, tt, start, 0, unroll=DMA_LOOP_UNROLL)
    lax.fori_loop(0, tt, wait, 0, unroll=DMA_LOOP_UNROLL)


def _scatter(h2, dest, zoff, zflag, n_rows, block_rows, xs_prev):
    N = h2.shape[0] // PACK_ROWS
    tt = min(ROUTE_TILE, N)
    zero_fill = xs_prev is None
    any_spec = pl.BlockSpec(memory_space=pl.ANY)
    in_specs = [pl.BlockSpec((TOP_K, tt), lambda i, *_: (0, i), memory_space=pltpu.SMEM),
                pl.BlockSpec((tt * PACK_ROWS, V7X_LANES), lambda i, *_: (i, 0))]
    args = [dest, h2]
    aliases = {}
    if not zero_fill:
        in_specs.append(any_spec)
        args.append(xs_prev)
        aliases = {4: 0}
    return pl.pallas_call(
        functools.partial(_scatter_body, zero_fill, block_rows),
        grid_spec=pltpu.PrefetchScalarGridSpec(
            num_scalar_prefetch=2, grid=(N // tt,), in_specs=in_specs, out_specs=any_spec,
            scratch_shapes=[pltpu.VMEM((ZERO_ROWS * PACK_ROWS, V7X_LANES), u32),
                            pltpu.SemaphoreType.DMA((2,))]),
        out_shape=jax.ShapeDtypeStruct((n_rows * PACK_ROWS, V7X_LANES), u32),
        input_output_aliases=aliases,
        compiler_params=_cparams(("arbitrary",)),
        name="moe_scatter",
    )(zoff, zflag, *args)


def _expert_body(nf, n_sub, bexp_ref, nused_ref, nvalid_ref, x_ref, wg_ref, wl_ref, bgl_ref, bll_ref,
                 wd_ref, bd_ref, o_ref, xb_ref, acc_ref):
    b = pl.program_id(0)
    j = pl.program_id(1)
    n_valid = nvalid_ref[b]
    active = n_valid > 0
    tm = xb_ref.shape[0]
    sub = tm // n_sub

    @pl.when(j == 0)
    def _():
        for c in range(PACK_ROWS):
            lo, hi = _unpack_chunk(x_ref, c, tm)
            xb_ref[:, c * PACK_SPAN:c * PACK_SPAN + V7X_LANES] = lo.astype(bf16)
            xb_ref[:, c * PACK_SPAN + V7X_LANES:(c + 1) * PACK_SPAN] = hi.astype(bf16)
        acc_ref[...] = jnp.where(active, jnp.broadcast_to(bd_ref[...], acc_ref.shape), 0.0)

    for q in range(1, n_sub + 1):
        @pl.when((n_valid > (q - 1) * sub) & (n_valid <= q * sub))
        def _():
            rows = slice(0, q * sub)
            xb = xb_ref[rows, :]
            glu = jnp.dot(xb, wg_ref[...].astype(bf16), preferred_element_type=f32) + bgl_ref[...]
            lin = jnp.dot(xb, wl_ref[...].astype(bf16), preferred_element_type=f32) + bll_ref[...]
            glu = jnp.minimum(glu, SWIGLU_LIMIT)
            lin = jnp.clip(lin, -SWIGLU_LIMIT, SWIGLU_LIMIT)
            act = glu * jax.nn.sigmoid(SWIGLU_ALPHA * glu) * (lin + 1.0)
            acc_ref[rows, :] += jnp.dot(act.astype(bf16), wd_ref[...].astype(bf16),
                                        preferred_element_type=f32)

    @pl.when(j == nf - 1)
    def _():
        _pack_rows(acc_ref[...], o_ref)


def _experts(xs, bexp, nused, nvalid, w_gate_up, b_gate_up, w_down, b_down, tm, tf):
    n_rows = xs.shape[0] // PACK_ROWS
    E, D, F2 = w_gate_up.shape
    F = F2 // 2
    assert n_rows % tm == 0 and F % tf == 0
    nf = F // tf
    nb = n_rows // tm

    def blk(b, nused):
        return jnp.minimum(b, nused[0] - 1)

    def ftile(b, j, nused):
        return jnp.where(b < nused[0], j, nf - 1)

    b_gu = b_gate_up.reshape(E, 1, F2)
    b_d = b_down.reshape(E, 1, D)
    return pl.pallas_call(
        functools.partial(_expert_body, nf, EXPERT_SUB_BLOCKS),
        grid_spec=pltpu.PrefetchScalarGridSpec(
            num_scalar_prefetch=3, grid=(nb, nf),
            in_specs=[
                pl.BlockSpec((tm * PACK_ROWS, V7X_LANES), lambda b, j, be, nu, nv: (blk(b, nu), 0)),
                pl.BlockSpec((None, D, tf), lambda b, j, be, nu, nv: (be[blk(b, nu)], 0, ftile(b, j, nu))),
                pl.BlockSpec((None, D, tf),
                             lambda b, j, be, nu, nv: (be[blk(b, nu)], 0, nf + ftile(b, j, nu))),
                pl.BlockSpec((None, 1, tf), lambda b, j, be, nu, nv: (be[blk(b, nu)], 0, ftile(b, j, nu))),
                pl.BlockSpec((None, 1, tf),
                             lambda b, j, be, nu, nv: (be[blk(b, nu)], 0, nf + ftile(b, j, nu))),
                pl.BlockSpec((None, tf, D), lambda b, j, be, nu, nv: (be[blk(b, nu)], ftile(b, j, nu), 0)),
                pl.BlockSpec((None, 1, D), lambda b, j, be, nu, nv: (be[blk(b, nu)], 0, 0)),
            ],
            out_specs=pl.BlockSpec((tm * PACK_ROWS, V7X_LANES), lambda b, j, be, nu, nv: (b, 0)),
            scratch_shapes=[pltpu.VMEM((tm, D), bf16), pltpu.VMEM((tm, D), f32)]),
        out_shape=jax.ShapeDtypeStruct((n_rows * PACK_ROWS, V7X_LANES), u32),
        compiler_params=_cparams(("arbitrary", "arbitrary")),
        name="moe_experts",
    )(bexp, nused, nvalid, xs, w_gate_up, w_gate_up, b_gu, b_gu, w_down, b_d)


COMBINE_TILE = 256


def _combine_body(n, dcur_ref, dnext_ref, x1_ref, gate_ref, gf_ref, ys_ref, y_ref, buf_ref, sem):
    i = pl.program_id(0)
    tt = x1_ref.shape[0]

    def packed(row):
        return pl.ds(pl.multiple_of(row * PACK_ROWS, PACK_ROWS), PACK_ROWS)

    def row_copy(d_ref, slot, t, r):
        return pltpu.make_async_copy(ys_ref.at[packed(d_ref[r, t])], buf_ref.at[slot, r, packed(t)],
                                     sem.at[slot])

    def issue(d_ref, slot):
        def body(t, carry):
            for r in range(TOP_K):
                row_copy(d_ref, slot, t, r).start(priority=r % DMA_PRIORITIES)
            return carry
        lax.fori_loop(0, tt, body, 0, unroll=DMA_LOOP_UNROLL)

    @pl.when(i == 0)
    def _():
        issue(dcur_ref, 0)

    @pl.when(i + 1 < n)
    def _():
        issue(dnext_ref, (i + 1) % 2)

    slot = i % 2

    def wait(t, carry):
        for r in range(TOP_K):
            row_copy(dcur_ref, slot, t, r).wait()
        return carry
    lax.fori_loop(0, tt, wait, 0, unroll=DMA_LOOP_UNROLL)

    gates = gate_ref[...]
    parts = []
    for c in range(PACK_ROWS):
        lo_sum = x1_ref[:, c * PACK_SPAN:c * PACK_SPAN + V7X_LANES]
        hi_sum = x1_ref[:, c * PACK_SPAN + V7X_LANES:(c + 1) * PACK_SPAN]
        for r in range(TOP_K):
            lo, hi = _unpack_chunk(buf_ref.at[slot, r], c, tt)
            lo_sum = lo_sum + gates[:, r:r + 1] * lo
            hi_sum = hi_sum + gates[:, r:r + 1] * hi
        parts += [lo_sum, hi_sum]
    x = jnp.concatenate(parts, axis=1)
    ms = jnp.mean(x * x, axis=-1, keepdims=True)
    y_ref[...] = x * lax.rsqrt(ms + RMS_EPS) * gf_ref[...]


def _combine(x1, gates_tm, dest, ys, g_final):
    N, D = x1.shape
    tt = min(COMBINE_TILE, N)
    n = N // tt
    smem = lambda f: pl.BlockSpec((TOP_K, tt), f, memory_space=pltpu.SMEM)
    return pl.pallas_call(
        functools.partial(_combine_body, n),
        grid=(n,),
        in_specs=[smem(lambda i: (0, i)), smem(lambda i: (0, jnp.minimum(i + 1, n - 1))),
                  pl.BlockSpec((tt, D), lambda i: (i, 0)), pl.BlockSpec((tt, TOP_K), lambda i: (i, 0)),
                  pl.BlockSpec((1, D), lambda i: (0, 0)), pl.BlockSpec(memory_space=pl.ANY)],
        out_specs=pl.BlockSpec((tt, D), lambda i: (i, 0)),
        out_shape=jax.ShapeDtypeStruct((N, D), f32),
        scratch_shapes=[pltpu.VMEM((2, TOP_K, tt * PACK_ROWS, V7X_LANES), u32),
                        pltpu.SemaphoreType.DMA((2,))],
        compiler_params=_cparams(("arbitrary",)),
        name="moe_combine",
    )(dest, dest, x1, gates_tm, g_final.reshape(1, D), ys)


def _moe(h2_groups, logit_groups, x1_groups, ew, g_final, tm, tf):
    w_gate_up, b_gate_up, w_down, b_down = ew
    E = w_gate_up.shape[0]
    routed = []
    count = jnp.zeros((E, V7X_LANES), f32)
    for lg in logit_groups:
        idx, gate, pos, count = _route(lg, count)
        routed.append((idx, gate, pos))
    total = sum(lg.shape[1] for lg in logit_groups) * TOP_K
    n_blocks = -(-(total + E * (tm - 1)) // tm)
    n_rows = n_blocks * tm
    counts = count[:, 0].astype(i32)
    padded = (counts + tm - 1) // tm * tm
    pad_end = jnp.cumsum(padded)
    pad_start = pad_end - padded
    n_used = pad_end[-1] // tm
    blocks = jnp.minimum(jnp.arange(n_blocks, dtype=i32), n_used - 1)
    bexp = jnp.minimum(jnp.sum(blocks[:, None] * tm >= pad_end[None, :], axis=1), E - 1).astype(i32)
    all_blocks = jnp.arange(n_blocks, dtype=i32)
    zoff = jnp.concatenate([jnp.maximum(pad_end - tm, 0).astype(i32), all_blocks * tm])
    zflag = jnp.concatenate([counts > 0, all_blocks >= n_used]).astype(i32)
    xs = None
    dests = []
    for h2, (idx, gate, pos) in zip(h2_groups, routed):
        dest = _dest(idx, pos, pad_start)
        dests.append(dest)
        xs = _scatter(h2, dest, zoff, zflag, n_rows, tm, xs)
    row0 = all_blocks * tm
    nvalid = jnp.clip(pad_start[bexp] + counts[bexp] - row0, 0, tm)
    nvalid = jnp.where(all_blocks < n_used, nvalid, 0).astype(i32)
    ys = _experts(xs, bexp, n_used.reshape(1).astype(i32), nvalid, w_gate_up, b_gate_up, w_down, b_down,
                  tm, tf)
    return [_combine(x1, gate.T, dest, ys, g_final)
            for x1, (idx, gate, pos), dest in zip(x1_groups, routed, dests)]


PROJ_TILE_M = 1024
PROJ_TILE_N = 1920
MERGE_TILE_M = 256
EXPERT_TILE_M = 1024
EXPERT_TILE_F = 256
EXPERT_SUB_BLOCKS = 4


def _mixers(x, state, norm_g, w_main, w_u, rw, sw, mw):
    B, T, D = x.shape
    W = w_u.shape[1]
    shift_w = 3 * W + W_LORA + A_LORA + G_LORA
    G, S = sw[0].shape[1] // 64, 64
    if state is None:
        s_rwkv0 = jnp.zeros((B, W // HEAD, HEAD, HEAD), f32)
        s5_re0 = jnp.zeros((B, W // 16, 64), f32)
        s5_im0 = jnp.zeros((B, W // 16, 64), f32)
        shift0 = jnp.zeros((B, shift_w), f32)
    else:
        s_rwkv0, s5_re0, s5_im0, shift0 = state
    per_batch = T % PROJ_TILE_M == 0
    xf = x if per_batch else x.reshape(1, B * T, D)
    tm = PROJ_TILE_M if per_batch else min(PROJ_TILE_M, B * T)
    proj = _norm_proj(xf, norm_g, w_main, tm, PROJ_TILE_N, False, bf16).reshape(B, T, MAIN_COLS)
    if per_batch:
        u_tb = _norm_proj(xf, norm_g, w_u, tm, W, True, f32)
    else:
        u_tb = jnp.swapaxes(_norm_proj(xf, norm_g, w_u, tm, W, False, f32).reshape(B, T, W), 0, 1)
    o_a, s_rwkv = _rwkv_stage(proj, shift0, s_rwkv0, rw)
    o_b, s5_re, s5_im = _s5_stage(u_tb, s5_re0, s5_im0, sw)
    if per_batch:
        x1, h2, logits_t = _merge_stage(x, o_a, o_b, True, proj, mw, MERGE_TILE_M)
    else:
        flat = lambda t: t.reshape(1, B * T, t.shape[-1])
        x1, h2, logits_t = _merge_stage(flat(x), flat(o_a), flat(jnp.swapaxes(o_b, 0, 1)), False,
                                        flat(proj), mw, min(MERGE_TILE_M, B * T))
    last = proj[:, -1, :].astype(f32)
    n_lora = W_LORA + A_LORA + G_LORA
    shift = jnp.concatenate([last[:, COL_R * W:(COL_R + 3) * W],
                             last[:, COL_LORA * LORA_PAD:COL_LORA * LORA_PAD + n_lora]], axis=1)
    return (x1.reshape(B * T, D), h2, logits_t), (s_rwkv, s5_re, s5_im, shift)


def kernel(x_prompt, x_sample, state_rwkv, state_s5_re, state_s5_im, state_shift, norm_mix_g, w_in, mu_shift, w0, w2, a0, a2, g2, k_k, k_a, r_k, lnx_g, lnx_b, lam_re, lam_im, log_dt, b_re, b_im, c_re, c_im, d_skip, w_glu, b_glu, w_up_a, w_up_b, w_out, norm_ffn_g, w_router, b_router, w_gate_up, b_gate_up, w_down, b_down, norm_final_g):
    assert w_in.shape[0] == 1, "single-layer trunk"
    D = x_prompt.shape[-1]
    W = w0.shape[-1]
    n_lora = W_LORA + A_LORA + G_LORA
    shift_w = 3 * W + n_lora
    w = w_in[0]
    ga0 = shift_w + W
    w_main = jnp.concatenate(
        [w[:, ga0:ga0 + 2 * D], w[:, :3 * W], w[:, 3 * W:shift_w],
         jnp.zeros((D, LORA_PAD - n_lora), w.dtype)], axis=1).astype(bf16)
    assert w_main.shape[1] == MAIN_COLS
    w_u = w[:, shift_w:shift_w + W].astype(bf16)
    rw = _rwkv_weights(mu_shift[0], w0[0], w2[0], a0[0], a2[0], g2[0], k_k[0], k_a[0], r_k[0],
                       lnx_g[0], lnx_b[0])
    sw = _s5_weights(lam_re[0], lam_im[0], log_dt[0], b_re[0], b_im[0], c_re[0], c_im[0],
                     d_skip[0], w_glu[0], b_glu[0])
    E = w_router.shape[-1]
    mw = (w_up_a[0].astype(bf16), w_up_b[0].astype(bf16), w_out[0].astype(bf16),
          norm_ffn_g[0].reshape(1, D).astype(f32), w_router[0].T.astype(bf16),
          b_router[0].reshape(E, 1).astype(f32))
    ew = (w_gate_up[0], b_gate_up[0], w_down[0], b_down[0])

    state_s = (state_rwkv[0], state_s5_re[0], state_s5_im[0], state_shift[0])
    tok_p, st_p = _mixers(x_prompt, None, norm_mix_g[0], w_main, w_u, rw, sw, mw)
    tok_s, st_s = _mixers(x_sample, state_s, norm_mix_g[0], w_main, w_u, rw, sw, mw)
    y_p, y_s = _moe([tok_p[1], tok_s[1]], [tok_p[2], tok_s[2]], [tok_p[0], tok_s[0]], ew, norm_final_g,
                    EXPERT_TILE_M, EXPERT_TILE_F)
    lead = lambda t: t[None]
    return (y_p.reshape(x_prompt.shape), y_s.reshape(x_sample.shape),
            lead(st_p[0]), lead(st_p[1]), lead(st_p[2]), lead(st_p[3]),
            lead(st_s[0]), lead(st_s[1]), lead(st_s[2]), lead(st_s[3]))
```

```python
import functools
import math

import jax
import jax.numpy as jnp
from jax import lax
from jax.experimental import pallas as pl
from jax.experimental.pallas import tpu as pltpu

f32 = jnp.float32
bf16 = jnp.bfloat16
i32 = jnp.int32
u32 = jnp.uint32

V7X_LANES = 128
V7X_SUBLANES = 8
V7X_VMEM_BYTES = 64 * 1024 * 1024
VMEM_LIMIT = 56 * 1024 * 1024

RMS_EPS = 1e-5
GN_EPS = 64e-5
HEAD = 64
HEAD_SHIFT = 6
HEADS_PER_GROUP = 4
GROUP = HEAD * HEADS_PER_GROUP
CHUNK = 64
W_LORA, A_LORA, G_LORA = 64, 64, 160
LORA_PAD = 512
TOP_K = 4
SWIGLU_LIMIT = 7.0
SWIGLU_ALPHA = 1.702


def _cparams(sem):
    return pltpu.CompilerParams(dimension_semantics=sem, vmem_limit_bytes=VMEM_LIMIT)


def _dot(a, b):
    return jnp.dot(a.astype(bf16), b.astype(bf16), preferred_element_type=f32)


def _dot_nt(a, b):
    return lax.dot_general(a.astype(bf16), b.astype(bf16), (((1,), (1,)), ((), ())),
                           preferred_element_type=f32)


def _dot_tn(a, b):
    return lax.dot_general(a.astype(bf16), b.astype(bf16), (((0,), (0,)), ((), ())),
                           preferred_element_type=f32)


def _split(x):
    hi = x.astype(bf16)
    lo = (x - hi.astype(f32)).astype(bf16)
    return hi, lo


def _dot_exact_rhs(a, b_exact):
    hi, lo = _split(a)
    return (jnp.dot(hi, b_exact, preferred_element_type=f32)
            + jnp.dot(lo, b_exact, preferred_element_type=f32))


def _dot_exact_lhs(a_exact, b):
    hi, lo = _split(b)
    return (jnp.dot(a_exact, hi, preferred_element_type=f32)
            + jnp.dot(a_exact, lo, preferred_element_type=f32))


def _norm_proj_body(x_ref, g_ref, w_ref, o_ref, h_ref):
    @pl.when(pl.program_id(2) == 0)
    def _():
        x = x_ref[...]
        ms = jnp.mean(x * x, axis=-1, keepdims=True)
        h_ref[...] = (x * lax.rsqrt(ms + RMS_EPS) * g_ref[...]).astype(bf16)

    o_ref[...] = jnp.dot(h_ref[...], w_ref[...], preferred_element_type=f32).astype(o_ref.dtype)


def _norm_proj(x, gain, w, tm, tn, time_major_out, out_dtype):
    B, T, D = x.shape
    N = w.shape[1]
    assert T % tm == 0 and N % tn == 0
    if time_major_out:
        out_shape = jax.ShapeDtypeStruct((T, B * N), out_dtype)
        out_spec = pl.BlockSpec((tm, tn), lambda b, i, n: (i, b * (N // tn) + n))
    else:
        out_shape = jax.ShapeDtypeStruct((B, T, N), out_dtype)
        out_spec = pl.BlockSpec((None, tm, tn), lambda b, i, n: (b, i, n))
    out = pl.pallas_call(
        _norm_proj_body,
        grid=(B, T // tm, N // tn),
        in_specs=[pl.BlockSpec((None, tm, D), lambda b, i, n: (b, i, 0)),
                  pl.BlockSpec((1, D), lambda b, i, n: (0, 0)),
                  pl.BlockSpec((D, tn), lambda b, i, n: (0, n))],
        out_specs=out_spec,
        out_shape=out_shape,
        scratch_shapes=[pltpu.VMEM((tm, D), bf16)],
        compiler_params=_cparams(("parallel", "parallel", "arbitrary")),
        name="norm_proj",
    )(x, gain.reshape(1, D), w)
    return out.reshape(T, B, N) if time_major_out else out


PV_MU_R, PV_MU_K, PV_MU_V, PV_W0, PV_A0, PV_KK, PV_KA, PV_RK, PV_LNG, PV_LNB = range(10)
PV_ROWS = 16
PREV_ROWS = 16
RWKV_CHUNKS_PER_STEP = 4


def _rwkv_body(nc, r_ref, k_ref, v_ref, l_ref, pr_ref, pk_ref, pv_ref, plo_ref,
               lr_ref, lk_ref, lv_ref, llo_ref, pvec_ref, mul_ref, wl_ref, s0_ref,
               o_ref, so_ref, s_ref):
    c = pl.program_id(1)
    C = CHUNK
    R, W = r_ref.shape
    n_chunks = R // C
    n_groups = W // GROUP

    @pl.when(c == 0)
    def _():
        for gq in range(n_groups):
            for h in range(HEADS_PER_GROUP):
                pieces = [jnp.zeros((HEAD, HEAD), f32)] * HEADS_PER_GROUP
                pieces[h] = s0_ref[gq * HEADS_PER_GROUP + h]
                s_ref[gq, h * HEAD:(h + 1) * HEAD, :] = jnp.concatenate(pieces, axis=1)

    row = lax.broadcasted_iota(i32, (R, 1), 0)
    pvec = pvec_ref[...]
    prm = lambda i: pvec[i:i + 1, :]

    gi = lax.broadcasted_iota(i32, (GROUP, GROUP), 0)
    gj = lax.broadcasted_iota(i32, (GROUP, GROUP), 1)
    same_head = (gi >> HEAD_SHIFT) == (gj >> HEAD_SHIFT)
    head_ones = jnp.where(same_head, 1.0, 0.0).astype(bf16)
    qt = lax.broadcasted_iota(i32, (C, GROUP), 0)
    qs = lax.broadcasted_iota(i32, (C, GROUP), 1) & (HEAD - 1)
    strict_q, incl_q = qt > qs, qt >= qs
    ci = lax.broadcasted_iota(i32, (R, R), 0)
    cj = lax.broadcasted_iota(i32, (R, R), 1)
    tri = jnp.where((ci >= cj) & ((ci >> HEAD_SHIFT) == (cj >> HEAD_SHIFT)), 1.0, 0.0).astype(bf16)
    lane_head = lax.broadcasted_iota(i32, (1, GROUP), 1) >> HEAD_SHIFT
    eye_quad = jnp.where(qt == qs, 1.0, 0.0)

    def head_sum(x, exact):
        parts = []
        for gq in range(n_groups):
            xg = x[:, gq * GROUP:(gq + 1) * GROUP]
            parts.append(_dot_exact_rhs(xg, head_ones) if exact
                         else jnp.dot(xg.astype(bf16), head_ones, preferred_element_type=f32))
        return jnp.concatenate(parts, axis=1)

    def shifted(x_ref, prev_ref, last_ref, mu):
        x = x_ref[...].astype(f32)
        prev_row = jnp.where(c > 0, prev_ref[PREV_ROWS - 1:PREV_ROWS, :].astype(f32), last_ref[...])
        prev = jnp.where(row == 0, prev_row, pltpu.roll(x, shift=1, axis=0))
        return x + (prev - x) * mu

    r = shifted(r_ref, pr_ref, lr_ref, prm(PV_MU_R))
    k = shifted(k_ref, pk_ref, lk_ref, prm(PV_MU_K))
    v = shifted(v_ref, pv_ref, lv_ref, prm(PV_MU_V))
    xl = shifted(l_ref, plo_ref, llo_ref, mul_ref[...])
    lane_l = lax.broadcasted_iota(i32, (1, xl.shape[1]), 1)
    act = jnp.where(lane_l < W_LORA, jnp.tanh(xl),
                    jnp.where(lane_l < W_LORA + A_LORA, xl, jax.nn.sigmoid(xl)))
    lo = jnp.dot(act.astype(bf16), wl_ref[...], preferred_element_type=f32)
    z = -(prm(PV_W0) + lo[:, :W])
    w = -(jnp.maximum(z, 0.0) + jnp.log(1.0 + jnp.exp(-jnp.abs(z)))) - 0.5
    logd = -jnp.exp(w)
    a = jax.nn.sigmoid(prm(PV_A0) + lo[:, W:2 * W])
    g = lo[:, 2 * W:]
    kk = k * prm(PV_KK)
    kk = kk * lax.rsqrt(jnp.maximum(head_sum(kk * kk, True), 1e-24))
    k2 = k * (1.0 + (a - 1.0) * prm(PV_KA))
    cl = _dot_exact_lhs(tri, logd)
    p_incl = jnp.exp(cl)
    p_excl = jnp.exp(cl - logd)
    p_inv = jnp.exp(-cl)
    rt = r * p_incl
    at = -kk * p_excl
    bt = kk * a * p_inv
    kt = k2 * p_inv
    bonus = head_sum(r * k2 * prm(PV_RK), False) * v

    def stack(x):
        return jnp.concatenate([jnp.where(lane_head == h, x, 0.0) for h in range(HEADS_PER_GROUP)], axis=0)

    def tile(x):
        return jnp.concatenate([x] * HEADS_PER_GROUP, axis=0)

    def block(x):
        return jnp.where(same_head, tile(x), 0.0)

    groups = range(n_groups)
    each = lambda f, *lists: [f(*xs) for xs in zip(*lists)]
    lanes = [slice(gq * GROUP, (gq + 1) * GROUP) for gq in groups]
    tiles = [(slice(ch * C, (ch + 1) * C), sl) for ch in range(n_chunks) for sl in lanes]
    ag, rg, bg, kg, vg = ([x[rows, sl] for rows, sl in tiles] for x in (at, rt, bt, kt, v))
    sa, sv = each(stack, ag), each(stack, vg)
    bd_t = lambda y: jnp.where(same_head, tile(y).T, 0.0)
    tb, tk = each(bd_t, bg), each(bd_t, kg)
    above = lambda x: x[:C]
    below = lambda x: x[C:]
    pair = lambda x, y: jnp.concatenate([x, y], axis=0)
    ar = each(pair, ag, rg)
    arb, ark = each(_dot, ar, tb), each(_dot, ar, tk)
    ab = each(lambda x: jnp.where(strict_q, above(x), 0.0), arb)
    rb = each(lambda x: jnp.where(incl_q, below(x), 0.0), arb)
    ak = each(lambda x: jnp.where(strict_q, above(x), 0.0), ark)
    rk = each(lambda x: jnp.where(incl_q, below(x), 0.0), ark)
    t4 = each(lambda q: eye_quad + q, ab)
    q4 = each(lambda q: _dot(q, block(q)), ab)
    n_sq = max(1, (C - 1).bit_length() - 1)
    for it in range(n_sq):
        if it < n_sq - 1:
            tq = each(lambda t, q: _dot(pair(t, q), block(q)), t4, q4)
            t4 = each(lambda t, x: t + above(x), t4, tq)
            q4 = each(below, tq)
        else:
            t4 = each(lambda t, q: t + _dot(t, block(q)), t4, q4)
    w4 = each(_dot, t4, sa)
    akr = each(lambda x, y, z: _dot(pair(x, y), z), ak, rk, sv)
    uv = each(lambda t, x: _dot(t, stack(above(x))), t4, akr)
    orv = each(below, akr)
    state = [s_ref[gq] for gq in groups]
    o_rows = []
    for ch in range(n_chunks):
        of = lambda xs: xs[ch * n_groups:(ch + 1) * n_groups]
        wr = each(lambda w_, r_, s_: _dot_nt(pair(w_, r_), s_), of(w4), of(rg), state)
        u4 = each(lambda x, uv_: above(x) + uv_, wr, of(uv))
        rs = each(below, wr)
        upd = each(lambda u_, v_, b_, k_: _dot_tn(jnp.concatenate([u_, v_], axis=0),
                                                  jnp.concatenate([b_, k_], axis=0)),
                   u4, of(vg), of(bg), of(kg))
        o4 = each(lambda rs_, rb_, u_, orv_: rs_ + _dot(rb_, stack(u_)) + orv_,
                  rs, of(rb), u4, of(orv))
        p_end = p_incl[(ch + 1) * C - 1:(ch + 1) * C, :]
        state = [(s + jnp.where(same_head, d, 0.0)) * p_end[:, sl] for s, d, sl in zip(state, upd, lanes)]
        o_rows.append(jnp.concatenate(o4, axis=1))
    for gq in groups:
        s_ref[gq] = state[gq]
    o = jnp.concatenate(o_rows, axis=0)

    inv_n = 1.0 / HEAD
    mean = head_sum(o, False) * inv_n
    d = o - mean
    var = head_sum(d * d, False) * inv_n
    on = d * lax.rsqrt(var + GN_EPS)
    o_ref[...] = (on * prm(PV_LNG) + prm(PV_LNB) + bonus) * g

    @pl.when(c == nc - 1)
    def _():
        for gq in range(n_groups):
            for h in range(HEADS_PER_GROUP):
                hs = slice(h * HEAD, (h + 1) * HEAD)
                so_ref[gq * HEADS_PER_GROUP + h] = s_ref[gq, hs, hs]


def _rwkv(proj, col_r, col_lora, last_rkv, last_lora, pvec, mu_lora, w_lora, s0):
    B, T, _ = proj.shape
    W = pvec.shape[1]
    C = CHUNK
    assert T % C == 0 and C == HEAD and HEAD == 1 << HEAD_SHIFT
    n_groups = W // GROUP
    R = C * RWKV_CHUNKS_PER_STEP if T % (C * RWKV_CHUNKS_PER_STEP) == 0 else C
    rb = R // PREV_ROWS

    def cur(col, width):
        return pl.BlockSpec((None, R, width), lambda b, c, col=col: (b, c, col))

    def prev(col, width):
        return pl.BlockSpec((None, PREV_ROWS, width),
                            lambda b, c, col=col: (b, jnp.maximum(c * rb - 1, 0), col))

    def last(col, width):
        return pl.BlockSpec((None, 1, width), lambda b, c, col=col: (b, 0, col))

    const = lambda shape: pl.BlockSpec(shape, lambda b, c: (0,) * len(shape))
    H = W // HEAD
    state_spec = pl.BlockSpec((None, H, HEAD, HEAD), lambda b, c: (b, 0, 0, 0))
    return pl.pallas_call(
        functools.partial(_rwkv_body, T // R),
        grid=(B, T // R),
        in_specs=[cur(col_r, W), cur(col_r + 1, W), cur(col_r + 2, W), cur(col_lora, LORA_PAD),
                  prev(col_r, W), prev(col_r + 1, W), prev(col_r + 2, W), prev(col_lora, LORA_PAD),
                  last(0, W), last(1, W), last(2, W), last(0, LORA_PAD),
                  const((PV_ROWS, W)), const((1, LORA_PAD)), const((LORA_PAD, 3 * W)), state_spec],
        out_specs=[pl.BlockSpec((None, R, W), lambda b, c: (b, c, 0)), state_spec],
        out_shape=[jax.ShapeDtypeStruct((B, T, W), f32),
                   jax.ShapeDtypeStruct((B, H, HEAD, HEAD), f32)],
        scratch_shapes=[pltpu.VMEM((n_groups, GROUP, GROUP), f32)],
        compiler_params=_cparams(("parallel", "arbitrary")),
        name="rwkv7_chunked",
    )(proj, proj, proj, proj, proj, proj, proj, proj,
      last_rkv, last_rkv, last_rkv, last_lora, pvec, mu_lora, w_lora, s0)


COL_GA, COL_GB = 0, 1
COL_R = 4
COL_LORA = 14
MAIN_COLS = 7680


def _rwkv_weights(mu_shift, w0, w2, a0, a2, g2, k_k, k_a, r_k, lnx_g, lnx_b):
    W = w0.shape[0]
    rows = [mu_shift[:W], mu_shift[W:2 * W], mu_shift[2 * W:3 * W], w0, a0, k_k, k_a,
            r_k.reshape(W), lnx_g, lnx_b]
    pvec = jnp.zeros((PV_ROWS, W), f32).at[:len(rows)].set(jnp.stack(rows).astype(f32))
    n_lora = W_LORA + A_LORA + G_LORA
    mu_lora = jnp.zeros((1, LORA_PAD), f32).at[0, :n_lora].set(mu_shift[3 * W:])
    w_lora = jnp.zeros((LORA_PAD, 3 * W), f32)
    w_lora = w_lora.at[:W_LORA, :W].set(w2)
    w_lora = w_lora.at[W_LORA:W_LORA + A_LORA, W:2 * W].set(a2)
    w_lora = w_lora.at[W_LORA + A_LORA:n_lora, 2 * W:].set(g2)
    return pvec, mu_lora, w_lora.astype(bf16)


def _rwkv_stage(proj, shift0, s0, rw):
    pvec, mu_lora, w_lora = rw
    B = proj.shape[0]
    W = pvec.shape[1]
    last_rkv = shift0[:, None, :3 * W].astype(f32)
    last_lora = jnp.zeros((B, 1, LORA_PAD), f32).at[:, 0, :shift0.shape[1] - 3 * W].set(shift0[:, 3 * W:])
    return _rwkv(proj, COL_R, COL_LORA, last_rkv, last_lora, pvec, mu_lora, w_lora, s0.astype(f32))


S5_BLOCK_IN = 128
S5_BLOCK_STATE = 512


def _s5_prep_body(lr_ref, li_ref, ldt_ref, br_ref, bi_ref, lbr_ref, lbi_ref, bbr_ref, bbi_ref):
    lr, li = lr_ref[...], li_ref[...]
    dt = jnp.exp(ldt_ref[...])
    mag = jnp.exp(lr * dt)
    er, ei = mag * jnp.cos(li * dt), mag * jnp.sin(li * dt)
    lbr_ref[...] = er
    lbi_ref[...] = ei
    den = lr * lr + li * li
    cr = ((er - 1.0) * lr + ei * li) / den
    ci = (ei * lr - (er - 1.0) * li) / den
    br, bi = br_ref[...], bi_ref[...]
    bbr_ref[...] = cr * br - ci * bi
    bbi_ref[...] = cr * bi + ci * br


def _s5_weights(lam_re, lam_im, log_dt, b_re, b_im, c_re, c_im, d_skip, w_glu, b_glu):
    G, S = lam_re.shape
    P = b_re.shape[-1]
    n = G * S
    col = lambda t: t.reshape(n, 1).astype(f32)
    ldt = jnp.broadcast_to(log_dt[:, None], (G, S))
    shapes = [jax.ShapeDtypeStruct((n, 1), f32)] * 2 + [jax.ShapeDtypeStruct((n, P), f32)] * 2
    lbr, lbi, bbr, bbi = pl.pallas_call(_s5_prep_body, out_shape=shapes, name="s5_discretise")(
        col(lam_re), col(lam_im), col(ldt), b_re.reshape(n, P).astype(f32), b_im.reshape(n, P).astype(f32))
    gpb = S5_BLOCK_IN // P
    nb = G // gpb
    eye = jnp.eye(gpb, dtype=f32)

    def in_block(bb):
        t = bb.reshape(nb, gpb, S, P)
        return jnp.einsum('qgsp,gh->qgphs', t, eye).reshape(nb, gpb * P, gpb * S)

    def out_block(cc):
        t = cc.astype(f32).reshape(nb, gpb, P, S)
        return jnp.einsum('qgps,gh->qgshp', t, eye).reshape(nb, gpb * S, gpb * P)

    wb = jnp.concatenate([in_block(bbr), in_block(bbi)], axis=2).astype(bf16)
    wc = jnp.concatenate([out_block(c_re), -out_block(c_im)], axis=1).astype(bf16)
    W = G * P
    return (lbr.reshape(1, n), lbi.reshape(1, n), wb, wc, d_skip.reshape(1, W).astype(f32),
            w_glu.astype(bf16), b_glu.reshape(1, W).astype(f32))


def _s5_body(n_steps, u_ref, x0r_ref, x0i_ref, lbr_ref, lbi_ref, wb_ref, wc_ref, d_ref, wg_ref, bg_ref,
             o_ref, xr_out, xi_out, bur_ref, bui_ref, xr_ref, xi_ref, y_ref):
    i = pl.program_id(1)
    Tc, Bs, W = u_ref.shape
    rows = Tc * Bs
    nb = wb_ref.shape[0]
    ns = S5_BLOCK_STATE

    @pl.when(i == 0)
    def _():
        xr_ref[...] = x0r_ref[...]
        xi_ref[...] = x0i_ref[...]

    u = u_ref[...].reshape(rows, W)
    for q in range(nb):
        bu = jnp.dot(u[:, q * S5_BLOCK_IN:(q + 1) * S5_BLOCK_IN].astype(bf16), wb_ref[q],
                     preferred_element_type=f32)
        bur_ref[:, q * ns:(q + 1) * ns] = bu[:, :ns]
        bui_ref[:, q * ns:(q + 1) * ns] = bu[:, ns:]

    for q in range(nb):
        sl = slice(q * ns, (q + 1) * ns)
        lr = jnp.broadcast_to(lbr_ref[:, sl], (Bs, ns))
        li = jnp.broadcast_to(lbi_ref[:, sl], (Bs, ns))

        def step(t, carry):
            xr, xi = carry
            rs = pl.ds(pl.multiple_of(t * Bs, Bs), Bs)
            nr = lr * xr - li * xi + bur_ref[rs, sl]
            ni = lr * xi + li * xr + bui_ref[rs, sl]
            bur_ref[rs, sl] = nr
            bui_ref[rs, sl] = ni
            return nr, ni

        xr, xi = lax.fori_loop(0, Tc, step, (xr_ref[:, sl], xi_ref[:, sl]), unroll=True)
        xr_ref[:, sl] = xr
        xi_ref[:, sl] = xi
        y_ref[:, q * S5_BLOCK_IN:(q + 1) * S5_BLOCK_IN] = (
            jnp.dot(bur_ref[:, sl].astype(bf16), wc_ref[q, :ns, :], preferred_element_type=f32)
            + jnp.dot(bui_ref[:, sl].astype(bf16), wc_ref[q, ns:, :], preferred_element_type=f32))

    y = y_ref[...] + d_ref[...] * u
    y = 0.5 * y * (1.0 + jnp.tanh(math.sqrt(2.0 / math.pi) * (y + 0.044715 * (y * y * y))))
    z = jnp.dot(y.astype(bf16), wg_ref[...], preferred_element_type=f32) + bg_ref[...]
    o_ref[...] = (y * jax.nn.sigmoid(z)).reshape(Tc, Bs, W)

    @pl.when(i == n_steps - 1)
    def _():
        xr_out[...] = xr_ref[...]
        xi_out[...] = xi_ref[...]


def _s5_stage(u_tb, x0_re, x0_im, sw):
    lbr, lbi, wb, wc, d_skip, w_glu, b_glu = sw
    T, B, W = u_tb.shape
    G, S = x0_re.shape[1:]
    n = G * S
    Bs = V7X_SUBLANES
    Tc = min(64, T)
    assert B % Bs == 0 and T % Tc == 0
    rows = Tc * Bs
    const = lambda shape: pl.BlockSpec(shape, lambda j, i: (0,) * len(shape))
    state_spec = pl.BlockSpec((Bs, n), lambda j, i: (j, 0))
    o, xr, xi = pl.pallas_call(
        functools.partial(_s5_body, T // Tc),
        grid=(B // Bs, T // Tc),
        in_specs=[pl.BlockSpec((Tc, Bs, W), lambda j, i: (i, j, 0)), state_spec, state_spec,
                  const((1, n)), const((1, n)), const(wb.shape), const(wc.shape),
                  const((1, W)), const((W, W)), const((1, W))],
        out_specs=[pl.BlockSpec((Tc, Bs, W), lambda j, i: (i, j, 0)), state_spec, state_spec],
        out_shape=[jax.ShapeDtypeStruct((T, B, W), f32),
                   jax.ShapeDtypeStruct((B, n), f32), jax.ShapeDtypeStruct((B, n), f32)],
        scratch_shapes=[pltpu.VMEM((rows, n), f32), pltpu.VMEM((rows, n), f32),
                        pltpu.VMEM((Bs, n), f32), pltpu.VMEM((Bs, n), f32), pltpu.VMEM((rows, W), f32)],
        compiler_params=_cparams(("parallel", "arbitrary")),
        name="s5_scan",
    )(u_tb, x0_re.reshape(B, n).astype(f32), x0_im.reshape(B, n).astype(f32),
      lbr, lbi, wb, wc, d_skip, w_glu, b_glu)
    return o, xr.reshape(B, G, S), xi.reshape(B, G, S)


PACK_ROWS = V7X_SUBLANES
PACK_SPAN = 2 * V7X_LANES
HIGH_HALF = 0xFFFF0000


def _pack_rows(x, out_ref):
    n, D = x.shape
    assert D == PACK_ROWS * PACK_SPAN
    bits = lax.bitcast_convert_type(x.astype(bf16).astype(f32), u32)
    for c in range(PACK_ROWS):
        lo = bits[:, c * PACK_SPAN:c * PACK_SPAN + V7X_LANES]
        hi = bits[:, c * PACK_SPAN + V7X_LANES:(c + 1) * PACK_SPAN]
        out_ref[pl.ds(c, n, stride=PACK_ROWS), :] = (lo >> 16) | (hi & jnp.uint32(HIGH_HALF))


def _unpack_chunk(ref, c, n):
    words = ref[pl.ds(c, n, stride=PACK_ROWS), :]
    lo = lax.bitcast_convert_type(words << 16, f32)
    hi = lax.bitcast_convert_type(words & jnp.uint32(HIGH_HALF), f32)
    return lo, hi


def _merge_body(x_ref, oa_ref, ob_ref, ga_ref, gb_ref, wa_ref, wb_ref, wo_ref, gf_ref, wr_ref, br_ref,
                x1_ref, h2_ref, lg_ref):
    ya = jnp.dot(oa_ref[...].astype(bf16), wa_ref[...], preferred_element_type=f32)
    yb = jnp.dot(ob_ref[...].astype(bf16), wb_ref[...], preferred_element_type=f32)
    merged = (jax.nn.sigmoid(ga_ref[...].astype(f32)) * ya
              + jax.nn.sigmoid(gb_ref[...].astype(f32)) * yb)
    x1 = x_ref[...] + jnp.dot(merged.astype(bf16), wo_ref[...], preferred_element_type=f32)
    x1_ref[...] = x1
    ms = jnp.mean(x1 * x1, axis=-1, keepdims=True)
    h2 = x1 * lax.rsqrt(ms + RMS_EPS) * gf_ref[...]
    _pack_rows(h2, h2_ref)
    lg_ref[...] = _dot_nt(wr_ref[...], h2) + br_ref[...]


def _resident(shape, nargs):
    return pl.BlockSpec(shape, lambda *_: (0,) * len(shape), pipeline_mode=pl.Buffered(1))


def _merge_stage(x, o_a, o_b, ob_time_major, proj, mw, tm):
    w_up_a, w_up_b, w_out, g_ffn, w_rt, b_r = mw
    B, T, D = x.shape
    W = o_a.shape[-1]
    E = w_rt.shape[0]
    assert T % tm == 0
    nt = T // tm
    tok = lambda width, col=0: pl.BlockSpec((None, tm, width), lambda b, i, col=col: (b, i, col))
    if ob_time_major:
        o_b = o_b.reshape(T, B * W)
        ob_spec = pl.BlockSpec((tm, W), lambda b, i: (i, b))
    else:
        ob_spec = tok(W)
    return pl.pallas_call(
        _merge_body,
        grid=(B, nt),
        in_specs=[tok(D), tok(W), ob_spec, tok(D, COL_GA), tok(D, COL_GB),
                  _resident((W, D), 2), _resident((W, D), 2), _resident((D, D), 2),
                  _resident((1, D), 2), _resident((E, D), 2), _resident((E, 1), 2)],
        out_specs=[tok(D), pl.BlockSpec((tm * PACK_ROWS, V7X_LANES), lambda b, i: (b * nt + i, 0)),
                   pl.BlockSpec((E, tm), lambda b, i: (0, b * nt + i))],
        out_shape=[jax.ShapeDtypeStruct((B, T, D), f32),
                   jax.ShapeDtypeStruct((B * T * PACK_ROWS, V7X_LANES), u32),
                   jax.ShapeDtypeStruct((E, B * T), f32)],
        compiler_params=_cparams(("parallel", "parallel")),
        name="merge_outproj",
    )(x, o_a, o_b, proj, proj, w_up_a, w_up_b, w_out, g_ffn, w_rt, b_r)


ROUTE_TILE = 512


def _route_body(lg_ref, c0_ref, idx_ref, gate_ref, pos_ref, cnt_ref, carry_ref):
    i = pl.program_id(0)
    E, tt = lg_ref.shape

    @pl.when(i == 0)
    def _():
        carry_ref[...] = c0_ref[...]

    l = lg_ref[...]
    e_iota = lax.broadcasted_iota(i32, (E, tt), 0)
    vals, sels = [], []
    for r in range(TOP_K):
        m = jnp.max(l, axis=0, keepdims=True)
        idx = jnp.min(jnp.where(l == m, e_iota, E), axis=0, keepdims=True)
        sel = e_iota == idx
        l = jnp.where(sel, -jnp.inf, l)
        idx_ref[r:r + 1, :] = idx
        vals.append(m)
        sels.append(sel)
    ex = [jnp.exp(v - vals[0]) for v in vals]
    tot = ex[0] + ex[1] + ex[2] + ex[3]
    for r in range(TOP_K):
        gate_ref[r:r + 1, :] = ex[r] / tot
    member = sels[0] | sels[1] | sels[2] | sels[3]
    mb = jnp.where(member, 1.0, 0.0)
    ti = lax.broadcasted_iota(i32, (tt, tt), 0)
    tj = lax.broadcasted_iota(i32, (tt, tt), 1)
    before = (ti < tj).astype(bf16)
    rank = carry_ref[:, 0:1] + jnp.dot(mb.astype(bf16), before, preferred_element_type=f32)
    for r in range(TOP_K):
        pos_ref[r:r + 1, :] = jnp.sum(jnp.where(sels[r], rank, 0.0), axis=0, keepdims=True).astype(i32)
    carry_ref[...] = carry_ref[...] + jnp.sum(mb, axis=1, keepdims=True)
    cnt_ref[...] = carry_ref[...]


def _route(logits_t, count0):
    E, N = logits_t.shape
    tt = min(ROUTE_TILE, N)
    assert N % tt == 0
    tokspec = pl.BlockSpec((TOP_K, tt), lambda i: (0, i))
    cspec = pl.BlockSpec((E, V7X_LANES), lambda i: (0, 0))
    return pl.pallas_call(
        _route_body,
        grid=(N // tt,),
        in_specs=[pl.BlockSpec((E, tt), lambda i: (0, i)), cspec],
        out_specs=[tokspec, tokspec, tokspec, cspec],
        out_shape=[jax.ShapeDtypeStruct((TOP_K, N), i32), jax.ShapeDtypeStruct((TOP_K, N), f32),
                   jax.ShapeDtypeStruct((TOP_K, N), i32), jax.ShapeDtypeStruct((E, V7X_LANES), f32)],
        scratch_shapes=[pltpu.VMEM((E, V7X_LANES), f32)],
        compiler_params=_cparams(("arbitrary",)),
        name="moe_route",
    )(logits_t, count0)


def _dest_body(idx_ref, pos_ref, start_ref, dest_ref):
    E = start_ref.shape[0]
    tt = idx_ref.shape[1]
    e_iota = lax.broadcasted_iota(i32, (E, tt), 0)
    start = start_ref[:, 0:1]
    for r in range(TOP_K):
        base = jnp.sum(jnp.where(e_iota == idx_ref[r:r + 1, :], start, 0.0), axis=0, keepdims=True)
        dest_ref[r:r + 1, :] = base.astype(i32) + pos_ref[r:r + 1, :]


def _dest(idx, pos, pad_start):
    N = idx.shape[1]
    E = pad_start.shape[0]
    tt = min(ROUTE_TILE, N)
    tokspec = pl.BlockSpec((TOP_K, tt), lambda i: (0, i))
    start = jnp.broadcast_to(pad_start.astype(f32)[:, None], (E, V7X_LANES))
    return pl.pallas_call(
        _dest_body,
        grid=(N // tt,),
        in_specs=[tokspec, tokspec, pl.BlockSpec((E, V7X_LANES), lambda i: (0, 0))],
        out_specs=tokspec,
        out_shape=jax.ShapeDtypeStruct((TOP_K, N), i32),
        compiler_params=_cparams(("parallel",)),
        name="moe_dest",
    )(idx, pos, start)


ZERO_ROWS = 256
DMA_PRIORITIES = 2
DMA_LOOP_UNROLL = 8


def _scatter_body(zero_fill, block_rows, zoff_ref, zflag_ref, dest_ref, h_ref, *rest):
    if zero_fill:
        xs_ref, zeros_ref, sem = rest
    else:
        _, xs_ref, zeros_ref, sem = rest
    i = pl.program_id(0)
    tt = dest_ref.shape[1]

    def packed(ref, row, n):
        return ref.at[pl.ds(pl.multiple_of(row * PACK_ROWS, PACK_ROWS), n * PACK_ROWS)]

    def zero_copy(e, piece):
        return pltpu.make_async_copy(
            zeros_ref, packed(xs_ref, zoff_ref[e] + piece * ZERO_ROWS, ZERO_ROWS), sem.at[0])

    if zero_fill:
        @pl.when(i == 0)
        def _():
            zeros_ref[...] = jnp.zeros_like(zeros_ref)
            n_e = zoff_ref.shape[0]
            for phase in ("start", "wait"):
                def per_expert(e, carry, phase=phase):
                    @pl.when(zflag_ref[e] > 0)
                    def _():
                        for piece in range(block_rows // ZERO_ROWS):
                            cp = zero_copy(e, piece)
                            cp.start() if phase == "start" else cp.wait()
                    return carry
                lax.fori_loop(0, n_e, per_expert, 0)

    def row_copy(t, r):
        return pltpu.make_async_copy(packed(h_ref, t, 1), packed(xs_ref, dest_ref[r, t], 1), sem.at[1])

    def start(t, carry):
        for r in range(TOP_K):
            row_copy(t, r).start(priority=r % DMA_PRIORITIES)
        return carry

    def wait(t, carry):
        for r in range(TOP_K):
            row_copy(t, r).wait()
        return carry

    lax.fori_loop(0, tt, start, 0, unroll=DMA_LOOP_UNROLL)
    lax.fori_loop(0, tt, wait, 0, unroll=DMA_LOOP_UNROLL)


def _scatter(h2, dest, zoff, zflag, n_rows, block_rows, xs_prev):
    N = h2.shape[0] // PACK_ROWS
    tt = min(ROUTE_TILE, N)
    zero_fill = xs_prev is None
    any_spec = pl.BlockSpec(memory_space=pl.ANY)
    in_specs = [pl.BlockSpec((TOP_K, tt), lambda i, *_: (0, i), memory_space=pltpu.SMEM),
                pl.BlockSpec((tt * PACK_ROWS, V7X_LANES), lambda i, *_: (i, 0))]
    args = [dest, h2]
    aliases = {}
    if not zero_fill:
        in_specs.append(any_spec)
        args.append(xs_prev)
        aliases = {4: 0}
    return pl.pallas_call(
        functools.partial(_scatter_body, zero_fill, block_rows),
        grid_spec=pltpu.PrefetchScalarGridSpec(
            num_scalar_prefetch=2, grid=(N // tt,), in_specs=in_specs, out_specs=any_spec,
            scratch_shapes=[pltpu.VMEM((ZERO_ROWS * PACK_ROWS, V7X_LANES), u32),
                            pltpu.SemaphoreType.DMA((2,))]),
        out_shape=jax.ShapeDtypeStruct((n_rows * PACK_ROWS, V7X_LANES), u32),
        input_output_aliases=aliases,
        compiler_params=_cparams(("arbitrary",)),
        name="moe_scatter",
    )(zoff, zflag, *args)


def _expert_body(nf, n_sub, bexp_ref, nused_ref, nvalid_ref, x_ref, wg_ref, wl_ref, bgl_ref, bll_ref,
                 wd_ref, bd_ref, o_ref, xb_ref, acc_ref):
    b = pl.program_id(0)
    j = pl.program_id(1)
    n_valid = nvalid_ref[b]
    active = n_valid > 0
    tm = xb_ref.shape[0]
    sub = tm // n_sub

    @pl.when(j == 0)
    def _():
        for c in range(PACK_ROWS):
            lo, hi = _unpack_chunk(x_ref, c, tm)
            xb_ref[:, c * PACK_SPAN:c * PACK_SPAN + V7X_LANES] = lo.astype(bf16)
            xb_ref[:, c * PACK_SPAN + V7X_LANES:(c + 1) * PACK_SPAN] = hi.astype(bf16)
        acc_ref[...] = jnp.where(active, jnp.broadcast_to(bd_ref[...], acc_ref.shape), 0.0)

    for q in range(1, n_sub + 1):
        @pl.when((n_valid > (q - 1) * sub) & (n_valid <= q * sub))
        def _():
            rows = slice(0, q * sub)
            xb = xb_ref[rows, :]
            glu = jnp.dot(xb, wg_ref[...].astype(bf16), preferred_element_type=f32) + bgl_ref[...]
            lin = jnp.dot(xb, wl_ref[...].astype(bf16), preferred_element_type=f32) + bll_ref[...]
            glu = jnp.minimum(glu, SWIGLU_LIMIT)
            lin = jnp.clip(lin, -SWIGLU_LIMIT, SWIGLU_LIMIT)
            act = glu * jax.nn.sigmoid(SWIGLU_ALPHA * glu) * (lin + 1.0)
            acc_ref[rows, :] += jnp.dot(act.astype(bf16), wd_ref[...].astype(bf16),
                                        preferred_element_type=f32)

    @pl.when(j == nf - 1)
    def _():
        _pack_rows(acc_ref[...], o_ref)


def _experts(xs, bexp, nused, nvalid, w_gate_up, b_gate_up, w_down, b_down, tm, tf):
    n_rows = xs.shape[0] // PACK_ROWS
    E, D, F2 = w_gate_up.shape
    F = F2 // 2
    assert n_rows % tm == 0 and F % tf == 0
    nf = F // tf
    nb = n_rows // tm

    def blk(b, nused):
        return jnp.minimum(b, nused[0] - 1)

    def ftile(b, j, nused):
        return jnp.where(b < nused[0], j, nf - 1)

    b_gu = b_gate_up.reshape(E, 1, F2)
    b_d = b_down.reshape(E, 1, D)
    return pl.pallas_call(
        functools.partial(_expert_body, nf, EXPERT_SUB_BLOCKS),
        grid_spec=pltpu.PrefetchScalarGridSpec(
            num_scalar_prefetch=3, grid=(nb, nf),
            in_specs=[
                pl.BlockSpec((tm * PACK_ROWS, V7X_LANES), lambda b, j, be, nu, nv: (blk(b, nu), 0)),
                pl.BlockSpec((None, D, tf), lambda b, j, be, nu, nv: (be[blk(b, nu)], 0, ftile(b, j, nu))),
                pl.BlockSpec((None, D, tf),
                             lambda b, j, be, nu, nv: (be[blk(b, nu)], 0, nf + ftile(b, j, nu))),
                pl.BlockSpec((None, 1, tf), lambda b, j, be, nu, nv: (be[blk(b, nu)], 0, ftile(b, j, nu))),
                pl.BlockSpec((None, 1, tf),
                             lambda b, j, be, nu, nv: (be[blk(b, nu)], 0, nf + ftile(b, j, nu))),
                pl.BlockSpec((None, tf, D), lambda b, j, be, nu, nv: (be[blk(b, nu)], ftile(b, j, nu), 0)),
                pl.BlockSpec((None, 1, D), lambda b, j, be, nu, nv: (be[blk(b, nu)], 0, 0)),
            ],
            out_specs=pl.BlockSpec((tm * PACK_ROWS, V7X_LANES), lambda b, j, be, nu, nv: (b, 0)),
            scratch_shapes=[pltpu.VMEM((tm, D), bf16), pltpu.VMEM((tm, D), f32)]),
        out_shape=jax.ShapeDtypeStruct((n_rows * PACK_ROWS, V7X_LANES), u32),
        compiler_params=_cparams(("arbitrary", "arbitrary")),
        name="moe_experts",
    )(bexp, nused, nvalid, xs, w_gate_up, w_gate_up, b_gu, b_gu, w_down, b_d)


COMBINE_TILE = 256


def _combine_body(n, dcur_ref, dnext_ref, x1_ref, gate_ref, gf_ref, ys_ref, y_ref, buf_ref, sem):
    i = pl.program_id(0)
    tt = x1_ref.shape[0]

    def packed(row):
        return pl.ds(pl.multiple_of(row * PACK_ROWS, PACK_ROWS), PACK_ROWS)

    def row_copy(d_ref, slot, t, r):
        return pltpu.make_async_copy(ys_ref.at[packed(d_ref[r, t])], buf_ref.at[slot, r, packed(t)],
                                     sem.at[slot])

    def issue(d_ref, slot):
        def body(t, carry):
            for r in range(TOP_K):
                row_copy(d_ref, slot, t, r).start(priority=r % DMA_PRIORITIES)
            return carry
        lax.fori_loop(0, tt, body, 0, unroll=DMA_LOOP_UNROLL)

    @pl.when(i == 0)
    def _():
        issue(dcur_ref, 0)

    @pl.when(i + 1 < n)
    def _():
        issue(dnext_ref, (i + 1) % 2)

    slot = i % 2

    def wait(t, carry):
        for r in range(TOP_K):
            row_copy(dcur_ref, slot, t, r).wait()
        return carry
    lax.fori_loop(0, tt, wait, 0, unroll=DMA_LOOP_UNROLL)

    gates = gate_ref[...]
    parts = []
    for c in range(PACK_ROWS):
        lo_sum = x1_ref[:, c * PACK_SPAN:c * PACK_SPAN + V7X_LANES]
        hi_sum = x1_ref[:, c * PACK_SPAN + V7X_LANES:(c + 1) * PACK_SPAN]
        for r in range(TOP_K):
            lo, hi = _unpack_chunk(buf_ref.at[slot, r], c, tt)
            lo_sum = lo_sum + gates[:, r:r + 1] * lo
            hi_sum = hi_sum + gates[:, r:r + 1] * hi
        parts += [lo_sum, hi_sum]
    x = jnp.concatenate(parts, axis=1)
    ms = jnp.mean(x * x, axis=-1, keepdims=True)
    y_ref[...] = x * lax.rsqrt(ms + RMS_EPS) * gf_ref[...]


def _combine(x1, gates_tm, dest, ys, g_final):
    N, D = x1.shape
    tt = min(COMBINE_TILE, N)
    n = N // tt
    smem = lambda f: pl.BlockSpec((TOP_K, tt), f, memory_space=pltpu.SMEM)
    return pl.pallas_call(
        functools.partial(_combine_body, n),
        grid=(n,),
        in_specs=[smem(lambda i: (0, i)), smem(lambda i: (0, jnp.minimum(i + 1, n - 1))),
                  pl.BlockSpec((tt, D), lambda i: (i, 0)), pl.BlockSpec((tt, TOP_K), lambda i: (i, 0)),
                  pl.BlockSpec((1, D), lambda i: (0, 0)), pl.BlockSpec(memory_space=pl.ANY)],
        out_specs=pl.BlockSpec((tt, D), lambda i: (i, 0)),
        out_shape=jax.ShapeDtypeStruct((N, D), f32),
        scratch_shapes=[pltpu.VMEM((2, TOP_K, tt * PACK_ROWS, V7X_LANES), u32),
                        pltpu.SemaphoreType.DMA((2,))],
        compiler_params=_cparams(("arbitrary",)),
        name="moe_combine",
    )(dest, dest, x1, gates_tm, g_final.reshape(1, D), ys)


def _moe(h2_groups, logit_groups, x1_groups, ew, g_final, tm, tf):
    w_gate_up, b_gate_up, w_down, b_down = ew
    E = w_gate_up.shape[0]
    routed = []
    count = jnp.zeros((E, V7X_LANES), f32)
    for lg in logit_groups:
        idx, gate, pos, count = _route(lg, count)
        routed.append((idx, gate, pos))
    total = sum(lg.shape[1] for lg in logit_groups) * TOP_K
    n_blocks = -(-(total + E * (tm - 1)) // tm)
    n_rows = n_blocks * tm
    counts = count[:, 0].astype(i32)
    padded = (counts + tm - 1) // tm * tm
    pad_end = jnp.cumsum(padded)
    pad_start = pad_end - padded
    n_used = pad_end[-1] // tm
    blocks = jnp.minimum(jnp.arange(n_blocks, dtype=i32), n_used - 1)
    bexp = jnp.minimum(jnp.sum(blocks[:, None] * tm >= pad_end[None, :], axis=1), E - 1).astype(i32)
    all_blocks = jnp.arange(n_blocks, dtype=i32)
    zoff = jnp.concatenate([jnp.maximum(pad_end - tm, 0).astype(i32), all_blocks * tm])
    zflag = jnp.concatenate([counts > 0, all_blocks >= n_used]).astype(i32)
    xs = None
    dests = []
    for h2, (idx, gate, pos) in zip(h2_groups, routed):
        dest = _dest(idx, pos, pad_start)
        dests.append(dest)
        xs = _scatter(h2, dest, zoff, zflag, n_rows, tm, xs)
    row0 = all_blocks * tm
    nvalid = jnp.clip(pad_start[bexp] + counts[bexp] - row0, 0, tm)
    nvalid = jnp.where(all_blocks < n_used, nvalid, 0).astype(i32)
    ys = _experts(xs, bexp, n_used.reshape(1).astype(i32), nvalid, w_gate_up, b_gate_up, w_down, b_down,
                  tm, tf)
    return [_combine(x1, gate.T, dest, ys, g_final)
            for x1, (idx, gate, pos), dest in zip(x1_groups, routed, dests)]


PROJ_TILE_M = 1024
PROJ_TILE_N = 1920
MERGE_TILE_M = 256
EXPERT_TILE_M = 1024
EXPERT_TILE_F = 256
EXPERT_SUB_BLOCKS = 4


def _mixers(x, state, norm_g, w_main, w_u, rw, sw, mw):
    B, T, D = x.shape
    W = w_u.shape[1]
    shift_w = 3 * W + W_LORA + A_LORA + G_LORA
    G, S = sw[0].shape[1] // 64, 64
    if state is None:
        s_rwkv0 = jnp.zeros((B, W // HEAD, HEAD, HEAD), f32)
        s5_re0 = jnp.zeros((B, W // 16, 64), f32)
        s5_im0 = jnp.zeros((B, W // 16, 64), f32)
        shift0 = jnp.zeros((B, shift_w), f32)
    else:
        s_rwkv0, s5_re0, s5_im0, shift0 = state
    per_batch = T % PROJ_TILE_M == 0
    xf = x if per_batch else x.reshape(1, B * T, D)
    tm = PROJ_TILE_M if per_batch else min(PROJ_TILE_M, B * T)
    proj = _norm_proj(xf, norm_g, w_main, tm, PROJ_TILE_N, False, bf16).reshape(B, T, MAIN_COLS)
    if per_batch:
        u_tb = _norm_proj(xf, norm_g, w_u, tm, W, True, f32)
    else:
        u_tb = jnp.swapaxes(_norm_proj(xf, norm_g, w_u, tm, W, False, f32).reshape(B, T, W), 0, 1)
    o_a, s_rwkv = _rwkv_stage(proj, shift0, s_rwkv0, rw)
    o_b, s5_re, s5_im = _s5_stage(u_tb, s5_re0, s5_im0, sw)
    if per_batch:
        x1, h2, logits_t = _merge_stage(x, o_a, o_b, True, proj, mw, MERGE_TILE_M)
    else:
        flat = lambda t: t.reshape(1, B * T, t.shape[-1])
        x1, h2, logits_t = _merge_stage(flat(x), flat(o_a), flat(jnp.swapaxes(o_b, 0, 1)), False,
                                        flat(proj), mw, min(MERGE_TILE_M, B * T))
    last = proj[:, -1, :].astype(f32)
    n_lora = W_LORA + A_LORA + G_LORA
    shift = jnp.concatenate([last[:, COL_R * W:(COL_R + 3) * W],
                             last[:, COL_LORA * LORA_PAD:COL_LORA * LORA_PAD + n_lora]], axis=1)
    return (x1.reshape(B * T, D), h2, logits_t), (s_rwkv, s5_re, s5_im, shift)


def kernel(x_prompt, x_sample, state_rwkv, state_s5_re, state_s5_im, state_shift, norm_mix_g, w_in, mu_shift, w0, w2, a0, a2, g2, k_k, k_a, r_k, lnx_g, lnx_b, lam_re, lam_im, log_dt, b_re, b_im, c_re, c_im, d_skip, w_glu, b_glu, w_up_a, w_up_b, w_out, norm_ffn_g, w_router, b_router, w_gate_up, b_gate_up, w_down, b_down, norm_final_g):
    assert w_in.shape[0] == 1, "single-layer trunk"
    D = x_prompt.shape[-1]
    W = w0.shape[-1]
    n_lora = W_LORA + A_LORA + G_LORA
    shift_w = 3 * W + n_lora
    w = w_in[0]
    ga0 = shift_w + W
    w_main = jnp.concatenate(
        [w[:, ga0:ga0 + 2 * D], w[:, :3 * W], w[:, 3 * W:shift_w],
         jnp.zeros((D, LORA_PAD - n_lora), w.dtype)], axis=1).astype(bf16)
    assert w_main.shape[1] == MAIN_COLS
    w_u = w[:, shift_w:shift_w + W].astype(bf16)
    rw = _rwkv_weights(mu_shift[0], w0[0], w2[0], a0[0], a2[0], g2[0], k_k[0], k_a[0], r_k[0],
                       lnx_g[0], lnx_b[0])
    sw = _s5_weights(lam_re[0], lam_im[0], log_dt[0], b_re[0], b_im[0], c_re[0], c_im[0],
                     d_skip[0], w_glu[0], b_glu[0])
    E = w_router.shape[-1]
    mw = (w_up_a[0].astype(bf16), w_up_b[0].astype(bf16), w_out[0].astype(bf16),
          norm_ffn_g[0].reshape(1, D).astype(f32), w_router[0].T.astype(bf16),
          b_router[0].reshape(E, 1).astype(f32))
    ew = (w_gate_up[0], b_gate_up[0], w_down[0], b_down[0])

    state_s = (state_rwkv[0], state_s5_re[0], state_s5_im[0], state_shift[0])
    tok_p, st_p = _mixers(x_prompt, None, norm_mix_g[0], w_main, w_u, rw, sw, mw)
    tok_s, st_s = _mixers(x_sample, state_s, norm_mix_g[0], w_main, w_u, rw, sw, mw)
    y_p, y_s = _moe([tok_p[1], tok_s[1]], [tok_p[2], tok_s[2]], [tok_p[0], tok_s[0]], ew, norm_final_g,
                    EXPERT_TILE_M, EXPERT_TILE_F)
    lead = lambda t: t[None]
    return (y_p.reshape(x_prompt.shape), y_s.reshape(x_sample.shape),
            lead(st_p[0]), lead(st_p[1]), lead(st_p[2]), lead(st_p[3]),
            lead(st_s[0]), lead(st_s[1]), lead(st_s[2]), lead(st_s[3]))
```

```python
import functools
import math

import jax
import jax.numpy as jnp
from jax import lax
from jax.experimental import pallas as pl
from jax.experimental.pallas import tpu as pltpu

f32 = jnp.float32
bf16 = jnp.bfloat16
i32 = jnp.int32
u32 = jnp.uint32

V7X_LANES = 128
V7X_SUBLANES = 8
V7X_VMEM_BYTES = 64 * 1024 * 1024
VMEM_LIMIT = 56 * 1024 * 1024

RMS_EPS = 1e-5
GN_EPS = 64e-5
HEAD = 64
HEAD_SHIFT = 6
HEADS_PER_GROUP = 4
GROUP = HEAD * HEADS_PER_GROUP
CHUNK = 64
W_LORA, A_LORA, G_LORA = 64, 64, 160
LORA_PAD = 512
TOP_K = 4
SWIGLU_LIMIT = 7.0
SWIGLU_ALPHA = 1.702


def _cparams(sem):
    return pltpu.CompilerParams(dimension_semantics=sem, vmem_limit_bytes=VMEM_LIMIT)


def _dot(a, b):
    return jnp.dot(a.astype(bf16), b.astype(bf16), preferred_element_type=f32)


def _dot_nt(a, b):
    return lax.dot_general(a.astype(bf16), b.astype(bf16), (((1,), (1,)), ((), ())),
                           preferred_element_type=f32)


def _dot_tn(a, b):
    return lax.dot_general(a.astype(bf16), b.astype(bf16), (((0,), (0,)), ((), ())),
                           preferred_element_type=f32)


def _split(x):
    hi = x.astype(bf16)
    lo = (x - hi.astype(f32)).astype(bf16)
    return hi, lo


def _dot_exact_rhs(a, b_exact):
    hi, lo = _split(a)
    return (jnp.dot(hi, b_exact, preferred_element_type=f32)
            + jnp.dot(lo, b_exact, preferred_element_type=f32))


def _dot_exact_lhs(a_exact, b):
    hi, lo = _split(b)
    return (jnp.dot(a_exact, hi, preferred_element_type=f32)
            + jnp.dot(a_exact, lo, preferred_element_type=f32))


def _norm_proj_body(x_ref, g_ref, w_ref, o_ref, h_ref):
    @pl.when(pl.program_id(2) == 0)
    def _():
        x = x_ref[...]
        ms = jnp.mean(x * x, axis=-1, keepdims=True)
        h_ref[...] = (x * lax.rsqrt(ms + RMS_EPS) * g_ref[...]).astype(bf16)

    o_ref[...] = jnp.dot(h_ref[...], w_ref[...], preferred_element_type=f32).astype(o_ref.dtype)


def _norm_proj(x, gain, w, tm, tn, time_major_out, out_dtype):
    B, T, D = x.shape
    N = w.shape[1]
    assert T % tm == 0 and N % tn == 0
    if time_major_out:
        out_shape = jax.ShapeDtypeStruct((T, B * N), out_dtype)
        out_spec = pl.BlockSpec((tm, tn), lambda b, i, n: (i, b * (N // tn) + n))
    else:
        out_shape = jax.ShapeDtypeStruct((B, T, N), out_dtype)
        out_spec = pl.BlockSpec((None, tm, tn), lambda b, i, n: (b, i, n))
    out = pl.pallas_call(
        _norm_proj_body,
        grid=(B, T // tm, N // tn),
        in_specs=[pl.BlockSpec((None, tm, D), lambda b, i, n: (b, i, 0)),
                  pl.BlockSpec((1, D), lambda b, i, n: (0, 0)),
                  pl.BlockSpec((D, tn), lambda b, i, n: (0, n))],
        out_specs=out_spec,
        out_shape=out_shape,
        scratch_shapes=[pltpu.VMEM((tm, D), bf16)],
        compiler_params=_cparams(("parallel", "parallel", "arbitrary")),
        name="norm_proj",
    )(x, gain.reshape(1, D), w)
    return out.reshape(T, B, N) if time_major_out else out


PV_MU_R, PV_MU_K, PV_MU_V, PV_W0, PV_A0, PV_KK, PV_KA, PV_RK, PV_LNG, PV_LNB = range(10)
PV_ROWS = 16
PREV_ROWS = 16
RWKV_CHUNKS_PER_STEP = 4


def _rwkv_body(nc, r_ref, k_ref, v_ref, l_ref, pr_ref, pk_ref, pv_ref, plo_ref,
               lr_ref, lk_ref, lv_ref, llo_ref, pvec_ref, mul_ref, wl_ref, s0_ref,
               o_ref, so_ref, s_ref):
    c = pl.program_id(1)
    C = CHUNK
    R, W = r_ref.shape
    n_chunks = R // C
    n_groups = W // GROUP

    @pl.when(c == 0)
    def _():
        for gq in range(n_groups):
            for h in range(HEADS_PER_GROUP):
                pieces = [jnp.zeros((HEAD, HEAD), f32)] * HEADS_PER_GROUP
                pieces[h] = s0_ref[gq * HEADS_PER_GROUP + h]
                s_ref[gq, h * HEAD:(h + 1) * HEAD, :] = jnp.concatenate(pieces, axis=1)

    row = lax.broadcasted_iota(i32, (R, 1), 0)
    pvec = pvec_ref[...]
    prm = lambda i: pvec[i:i + 1, :]

    gi = lax.broadcasted_iota(i32, (GROUP, GROUP), 0)
    gj = lax.broadcasted_iota(i32, (GROUP, GROUP), 1)
    same_head = (gi >> HEAD_SHIFT) == (gj >> HEAD_SHIFT)
    head_ones = jnp.where(same_head, 1.0, 0.0).astype(bf16)
    qt = lax.broadcasted_iota(i32, (C, GROUP), 0)
    qs = lax.broadcasted_iota(i32, (C, GROUP), 1) & (HEAD - 1)
    strict_q, incl_q = qt > qs, qt >= qs
    ci = lax.broadcasted_iota(i32, (R, R), 0)
    cj = lax.broadcasted_iota(i32, (R, R), 1)
    tri = jnp.where((ci >= cj) & ((ci >> HEAD_SHIFT) == (cj >> HEAD_SHIFT)), 1.0, 0.0).astype(bf16)
    lane_head = lax.broadcasted_iota(i32, (1, GROUP), 1) >> HEAD_SHIFT
    eye_quad = jnp.where(qt == qs, 1.0, 0.0)

    def head_sum(x, exact):
        parts = []
        for gq in range(n_groups):
            xg = x[:, gq * GROUP:(gq + 1) * GROUP]
            parts.append(_dot_exact_rhs(xg, head_ones) if exact
                         else jnp.dot(xg.astype(bf16), head_ones, preferred_element_type=f32))
        return jnp.concatenate(parts, axis=1)

    def shifted(x_ref, prev_ref, last_ref, mu):
        x = x_ref[...].astype(f32)
        prev_row = jnp.where(c > 0, prev_ref[PREV_ROWS - 1:PREV_ROWS, :].astype(f32), last_ref[...])
        prev = jnp.where(row == 0, prev_row, pltpu.roll(x, shift=1, axis=0))
        return x + (prev - x) * mu

    r = shifted(r_ref, pr_ref, lr_ref, prm(PV_MU_R))
    k = shifted(k_ref, pk_ref, lk_ref, prm(PV_MU_K))
    v = shifted(v_ref, pv_ref, lv_ref, prm(PV_MU_V))
    xl = shifted(l_ref, plo_ref, llo_ref, mul_ref[...])
    lane_l = lax.broadcasted_iota(i32, (1, xl.shape[1]), 1)
    act = jnp.where(lane_l < W_LORA, jnp.tanh(xl),
                    jnp.where(lane_l < W_LORA + A_LORA, xl, jax.nn.sigmoid(xl)))
    lo = jnp.dot(act.astype(bf16), wl_ref[...], preferred_element_type=f32)
    z = -(prm(PV_W0) + lo[:, :W])
    w = -(jnp.maximum(z, 0.0) + jnp.log(1.0 + jnp.exp(-jnp.abs(z)))) - 0.5
    logd = -jnp.exp(w)
    a = jax.nn.sigmoid(prm(PV_A0) + lo[:, W:2 * W])
    g = lo[:, 2 * W:]
    kk = k * prm(PV_KK)
    kk = kk * lax.rsqrt(jnp.maximum(head_sum(kk * kk, True), 1e-24))
    k2 = k * (1.0 + (a - 1.0) * prm(PV_KA))
    cl = _dot_exact_lhs(tri, logd)
    p_incl = jnp.exp(cl)
    p_excl = jnp.exp(cl - logd)
    p_inv = jnp.exp(-cl)
    rt = r * p_incl
    at = -kk * p_excl
    bt = kk * a * p_inv
    kt = k2 * p_inv
    bonus = head_sum(r * k2 * prm(PV_RK), False) * v

    def stack(x):
        return jnp.concatenate([jnp.where(lane_head == h, x, 0.0) for h in range(HEADS_PER_GROUP)], axis=0)

    def tile(x):
        return jnp.concatenate([x] * HEADS_PER_GROUP, axis=0)

    def block(x):
        return jnp.where(same_head, tile(x), 0.0)

    groups = range(n_groups)
    each = lambda f, *lists: [f(*xs) for xs in zip(*lists)]
    lanes = [slice(gq * GROUP, (gq + 1) * GROUP) for gq in groups]
    tiles = [(slice(ch * C, (ch + 1) * C), sl) for ch in range(n_chunks) for sl in lanes]
    ag, rg, bg, kg, vg = ([x[rows, sl] for rows, sl in tiles] for x in (at, rt, bt, kt, v))
    sa, sv = each(stack, ag), each(stack, vg)
    bd_t = lambda y: jnp.where(same_head, tile(y).T, 0.0)
    tb, tk = each(bd_t, bg), each(bd_t, kg)
    above = lambda x: x[:C]
    below = lambda x: x[C:]
    pair = lambda x, y: jnp.concatenate([x, y], axis=0)
    ar = each(pair, ag, rg)
    arb, ark = each(_dot, ar, tb), each(_dot, ar, tk)
    ab = each(lambda x: jnp.where(strict_q, above(x), 0.0), arb)
    rb = each(lambda x: jnp.where(incl_q, below(x), 0.0), arb)
    ak = each(lambda x: jnp.where(strict_q, above(x), 0.0), ark)
    rk = each(lambda x: jnp.where(incl_q, below(x), 0.0), ark)
    t4 = each(lambda q: eye_quad + q, ab)
    q4 = each(lambda q: _dot(q, block(q)), ab)
    n_sq = max(1, (C - 1).bit_length() - 1)
    for it in range(n_sq):
        if it < n_sq - 1:
            tq = each(lambda t, q: _dot(pair(t, q), block(q)), t4, q4)
            t4 = each(lambda t, x: t + above(x), t4, tq)
            q4 = each(below, tq)
        else:
            t4 = each(lambda t, q: t + _dot(t, block(q)), t4, q4)
    w4 = each(_dot, t4, sa)
    akr = each(lambda x, y, z: _dot(pair(x, y), z), ak, rk, sv)
    uv = each(lambda t, x: _dot(t, stack(above(x))), t4, akr)
    orv = each(below, akr)
    state = [s_ref[gq] for gq in groups]
    o_rows = []
    for ch in range(n_chunks):
        of = lambda xs: xs[ch * n_groups:(ch + 1) * n_groups]
        wr = each(lambda w_, r_, s_: _dot_nt(pair(w_, r_), s_), of(w4), of(rg), state)
        u4 = each(lambda x, uv_: above(x) + uv_, wr, of(uv))
        rs = each(below, wr)
        upd = each(lambda u_, v_, b_, k_: _dot_tn(jnp.concatenate([u_, v_], axis=0),
                                                  jnp.concatenate([b_, k_], axis=0)),
                   u4, of(vg), of(bg), of(kg))
        o4 = each(lambda rs_, rb_, u_, orv_: rs_ + _dot(rb_, stack(u_)) + orv_,
                  rs, of(rb), u4, of(orv))
        p_end = p_incl[(ch + 1) * C - 1:(ch + 1) * C, :]
        state = [(s + jnp.where(same_head, d, 0.0)) * p_end[:, sl] for s, d, sl in zip(state, upd, lanes)]
        o_rows.append(jnp.concatenate(o4, axis=1))
    for gq in groups:
        s_ref[gq] = state[gq]
    o = jnp.concatenate(o_rows, axis=0)

    inv_n = 1.0 / HEAD
    mean = head_sum(o, False) * inv_n
    d = o - mean
    var = head_sum(d * d, False) * inv_n
    on = d * lax.rsqrt(var + GN_EPS)
    o_ref[...] = (on * prm(PV_LNG) + prm(PV_LNB) + bonus) * g

    @pl.when(c == nc - 1)
    def _():
        for gq in range(n_groups):
            for h in range(HEADS_PER_GROUP):
                hs = slice(h * HEAD, (h + 1) * HEAD)
                so_ref[gq * HEADS_PER_GROUP + h] = s_ref[gq, hs, hs]


def _rwkv(proj, col_r, col_lora, last_rkv, last_lora, pvec, mu_lora, w_lora, s0):
    B, T, _ = proj.shape
    W = pvec.shape[1]
    C = CHUNK
    assert T % C == 0 and C == HEAD and HEAD == 1 << HEAD_SHIFT
    n_groups = W // GROUP
    R = C * RWKV_CHUNKS_PER_STEP if T % (C * RWKV_CHUNKS_PER_STEP) == 0 else C
    rb = R // PREV_ROWS

    def cur(col, width):
        return pl.BlockSpec((None, R, width), lambda b, c, col=col: (b, c, col))

    def prev(col, width):
        return pl.BlockSpec((None, PREV_ROWS, width),
                            lambda b, c, col=col: (b, jnp.maximum(c * rb - 1, 0), col))

    def last(col, width):
        return pl.BlockSpec((None, 1, width), lambda b, c, col=col: (b, 0, col))

    const = lambda shape: pl.BlockSpec(shape, lambda b, c: (0,) * len(shape))
    H = W // HEAD
    state_spec = pl.BlockSpec((None, H, HEAD, HEAD), lambda b, c: (b, 0, 0, 0))
    return pl.pallas_call(
        functools.partial(_rwkv_body, T // R),
        grid=(B, T // R),
        in_specs=[cur(col_r, W), cur(col_r + 1, W), cur(col_r + 2, W), cur(col_lora, LORA_PAD),
                  prev(col_r, W), prev(col_r + 1, W), prev(col_r + 2, W), prev(col_lora, LORA_PAD),
                  last(0, W), last(1, W), last(2, W), last(0, LORA_PAD),
                  const((PV_ROWS, W)), const((1, LORA_PAD)), const((LORA_PAD, 3 * W)), state_spec],
        out_specs=[pl.BlockSpec((None, R, W), lambda b, c: (b, c, 0)), state_spec],
        out_shape=[jax.ShapeDtypeStruct((B, T, W), f32),
                   jax.ShapeDtypeStruct((B, H, HEAD, HEAD), f32)],
        scratch_shapes=[pltpu.VMEM((n_groups, GROUP, GROUP), f32)],
        compiler_params=_cparams(("parallel", "arbitrary")),
        name="rwkv7_chunked",
    )(proj, proj, proj, proj, proj, proj, proj, proj,
      last_rkv, last_rkv, last_rkv, last_lora, pvec, mu_lora, w_lora, s0)


COL_GA, COL_GB = 0, 1
COL_R = 4
COL_LORA = 14
MAIN_COLS = 7680


def _rwkv_weights(mu_shift, w0, w2, a0, a2, g2, k_k, k_a, r_k, lnx_g, lnx_b):
    W = w0.shape[0]
    rows = [mu_shift[:W], mu_shift[W:2 * W], mu_shift[2 * W:3 * W], w0, a0, k_k, k_a,
            r_k.reshape(W), lnx_g, lnx_b]
    pvec = jnp.zeros((PV_ROWS, W), f32).at[:len(rows)].set(jnp.stack(rows).astype(f32))
    n_lora = W_LORA + A_LORA + G_LORA
    mu_lora = jnp.zeros((1, LORA_PAD), f32).at[0, :n_lora].set(mu_shift[3 * W:])
    w_lora = jnp.zeros((LORA_PAD, 3 * W), f32)
    w_lora = w_lora.at[:W_LORA, :W].set(w2)
    w_lora = w_lora.at[W_LORA:W_LORA + A_LORA, W:2 * W].set(a2)
    w_lora = w_lora.at[W_LORA + A_LORA:n_lora, 2 * W:].set(g2)
    return pvec, mu_lora, w_lora.astype(bf16)


def _rwkv_stage(proj, shift0, s0, rw):
    pvec, mu_lora, w_lora = rw
    B = proj.shape[0]
    W = pvec.shape[1]
    last_rkv = shift0[:, None, :3 * W].astype(f32)
    last_lora = jnp.zeros((B, 1, LORA_PAD), f32).at[:, 0, :shift0.shape[1] - 3 * W].set(shift0[:, 3 * W:])
    return _rwkv(proj, COL_R, COL_LORA, last_rkv, last_lora, pvec, mu_lora, w_lora, s0.astype(f32))


S5_BLOCK_IN = 128
S5_BLOCK_STATE = 512


def _s5_prep_body(lr_ref, li_ref, ldt_ref, br_ref, bi_ref, lbr_ref, lbi_ref, bbr_ref, bbi_ref):
    lr, li = lr_ref[...], li_ref[...]
    dt = jnp.exp(ldt_ref[...])
    mag = jnp.exp(lr * dt)
    er, ei = mag * jnp.cos(li * dt), mag * jnp.sin(li * dt)
    lbr_ref[...] = er
    lbi_ref[...] = ei
    den = lr * lr + li * li
    cr = ((er - 1.0) * lr + ei * li) / den
    ci = (ei * lr - (er - 1.0) * li) / den
    br, bi = br_ref[...], bi_ref[...]
    bbr_ref[...] = cr * br - ci * bi
    bbi_ref[...] = cr * bi + ci * br


def _s5_weights(lam_re, lam_im, log_dt, b_re, b_im, c_re, c_im, d_skip, w_glu, b_glu):
    G, S = lam_re.shape
    P = b_re.shape[-1]
    n = G * S
    col = lambda t: t.reshape(n, 1).astype(f32)
    ldt = jnp.broadcast_to(log_dt[:, None], (G, S))
    shapes = [jax.ShapeDtypeStruct((n, 1), f32)] * 2 + [jax.ShapeDtypeStruct((n, P), f32)] * 2
    lbr, lbi, bbr, bbi = pl.pallas_call(_s5_prep_body, out_shape=shapes, name="s5_discretise")(
        col(lam_re), col(lam_im), col(ldt), b_re.reshape(n, P).astype(f32), b_im.reshape(n, P).astype(f32))
    gpb = S5_BLOCK_IN // P
    nb = G // gpb
    eye = jnp.eye(gpb, dtype=f32)

    def in_block(bb):
        t = bb.reshape(nb, gpb, S, P)
        return jnp.einsum('qgsp,gh->qgphs', t, eye).reshape(nb, gpb * P, gpb * S)

    def out_block(cc):
        t = cc.astype(f32).reshape(nb, gpb, P, S)
        return jnp.einsum('qgps,gh->qgshp', t, eye).reshape(nb, gpb * S, gpb * P)

    wb = jnp.concatenate([in_block(bbr), in_block(bbi)], axis=2).astype(bf16)
    wc = jnp.concatenate([out_block(c_re), -out_block(c_im)], axis=1).astype(bf16)
    W = G * P
    return (lbr.reshape(1, n), lbi.reshape(1, n), wb, wc, d_skip.reshape(1, W).astype(f32),
            w_glu.astype(bf16), b_glu.reshape(1, W).astype(f32))


def _s5_body(n_steps, u_ref, x0r_ref, x0i_ref, lbr_ref, lbi_ref, wb_ref, wc_ref, d_ref, wg_ref, bg_ref,
             o_ref, xr_out, xi_out, bur_ref, bui_ref, xr_ref, xi_ref, y_ref):
    i = pl.program_id(1)
    Tc, Bs, W = u_ref.shape
    rows = Tc * Bs
    nb = wb_ref.shape[0]
    ns = S5_BLOCK_STATE

    @pl.when(i == 0)
    def _():
        xr_ref[...] = x0r_ref[...]
        xi_ref[...] = x0i_ref[...]

    u = u_ref[...].reshape(rows, W)
    for q in range(nb):
        bu = jnp.dot(u[:, q * S5_BLOCK_IN:(q + 1) * S5_BLOCK_IN].astype(bf16), wb_ref[q],
                     preferred_element_type=f32)
        bur_ref[:, q * ns:(q + 1) * ns] = bu[:, :ns]
        bui_ref[:, q * ns:(q + 1) * ns] = bu[:, ns:]

    for q in range(nb):
        sl = slice(q * ns, (q + 1) * ns)
        lr = jnp.broadcast_to(lbr_ref[:, sl], (Bs, ns))
        li = jnp.broadcast_to(lbi_ref[:, sl], (Bs, ns))

        def step(t, carry):
            xr, xi = carry
            rs = pl.ds(pl.multiple_of(t * Bs, Bs), Bs)
            nr = lr * xr - li * xi + bur_ref[rs, sl]
            ni = lr * xi + li * xr + bui_ref[rs, sl]
            bur_ref[rs, sl] = nr
            bui_ref[rs, sl] = ni
            return nr, ni

        xr, xi = lax.fori_loop(0, Tc, step, (xr_ref[:, sl], xi_ref[:, sl]), unroll=True)
        xr_ref[:, sl] = xr
        xi_ref[:, sl] = xi
        y_ref[:, q * S5_BLOCK_IN:(q + 1) * S5_BLOCK_IN] = (
            jnp.dot(bur_ref[:, sl].astype(bf16), wc_ref[q, :ns, :], preferred_element_type=f32)
            + jnp.dot(bui_ref[:, sl].astype(bf16), wc_ref[q, ns:, :], preferred_element_type=f32))

    y = y_ref[...] + d_ref[...] * u
    y = 0.5 * y * (1.0 + jnp.tanh(math.sqrt(2.0 / math.pi) * (y + 0.044715 * (y * y * y))))
    z = jnp.dot(y.astype(bf16), wg_ref[...], preferred_element_type=f32) + bg_ref[...]
    o_ref[...] = (y * jax.nn.sigmoid(z)).reshape(Tc, Bs, W)

    @pl.when(i == n_steps - 1)
    def _():
        xr_out[...] = xr_ref[...]
        xi_out[...] = xi_ref[...]


def _s5_stage(u_tb, x0_re, x0_im, sw):
    lbr, lbi, wb, wc, d_skip, w_glu, b_glu = sw
    T, B, W = u_tb.shape
    G, S = x0_re.shape[1:]
    n = G * S
    Bs = V7X_SUBLANES
    Tc = min(64, T)
    assert B % Bs == 0 and T % Tc == 0
    rows = Tc * Bs
    const = lambda shape: pl.BlockSpec(shape, lambda j, i: (0,) * len(shape))
    state_spec = pl.BlockSpec((Bs, n), lambda j, i: (j, 0))
    o, xr, xi = pl.pallas_call(
        functools.partial(_s5_body, T // Tc),
        grid=(B // Bs, T // Tc),
        in_specs=[pl.BlockSpec((Tc, Bs, W), lambda j, i: (i, j, 0)), state_spec, state_spec,
                  const((1, n)), const((1, n)), const(wb.shape), const(wc.shape),
                  const((1, W)), const((W, W)), const((1, W))],
        out_specs=[pl.BlockSpec((Tc, Bs, W), lambda j, i: (i, j, 0)), state_spec, state_spec],
        out_shape=[jax.ShapeDtypeStruct((T, B, W), f32),
                   jax.ShapeDtypeStruct((B, n), f32), jax.ShapeDtypeStruct((B, n), f32)],
        scratch_shapes=[pltpu.VMEM((rows, n), f32), pltpu.VMEM((rows, n), f32),
                        pltpu.VMEM((Bs, n), f32), pltpu.VMEM((Bs, n), f32), pltpu.VMEM((rows, W), f32)],
        compiler_params=_cparams(("parallel", "arbitrary")),
        name="s5_scan",
    )(u_tb, x0_re.reshape(B, n).astype(f32), x0_im.reshape(B, n).astype(f32),
      lbr, lbi, wb, wc, d_skip, w_glu, b_glu)
    return o, xr.reshape(B, G, S), xi.reshape(B, G, S)


PACK_ROWS = V7X_SUBLANES
PACK_SPAN = 2 * V7X_LANES
HIGH_HALF = 0xFFFF0000


def _pack_rows(x, out_ref):
    n, D = x.shape
    assert D == PACK_ROWS * PACK_SPAN
    bits = lax.bitcast_convert_type(x.astype(bf16).astype(f32), u32)
    for c in range(PACK_ROWS):
        lo = bits[:, c * PACK_SPAN:c * PACK_SPAN + V7X_LANES]
        hi = bits[:, c * PACK_SPAN + V7X_LANES:(c + 1) * PACK_SPAN]
        out_ref[pl.ds(c, n, stride=PACK_ROWS), :] = (lo >> 16) | (hi & jnp.uint32(HIGH_HALF))


def _unpack_chunk(ref, c, n):
    words = ref[pl.ds(c, n, stride=PACK_ROWS), :]
    lo = lax.bitcast_convert_type(words << 16, f32)
    hi = lax.bitcast_convert_type(words & jnp.uint32(HIGH_HALF), f32)
    return lo, hi


def _merge_body(x_ref, oa_ref, ob_ref, ga_ref, gb_ref, wa_ref, wb_ref, wo_ref, gf_ref, wr_ref, br_ref,
                x1_ref, h2_ref, lg_ref):
    ya = jnp.dot(oa_ref[...].astype(bf16), wa_ref[...], preferred_element_type=f32)
    yb = jnp.dot(ob_ref[...].astype(bf16), wb_ref[...], preferred_element_type=f32)
    merged = (jax.nn.sigmoid(ga_ref[...].astype(f32)) * ya
              + jax.nn.sigmoid(gb_ref[...].astype(f32)) * yb)
    x1 = x_ref[...] + jnp.dot(merged.astype(bf16), wo_ref[...], preferred_element_type=f32)
    x1_ref[...] = x1
    ms = jnp.mean(x1 * x1, axis=-1, keepdims=True)
    h2 = x1 * lax.rsqrt(ms + RMS_EPS) * gf_ref[...]
    _pack_rows(h2, h2_ref)
    lg_ref[...] = _dot_nt(wr_ref[...], h2) + br_ref[...]


def _resident(shape, nargs):
    return pl.BlockSpec(shape, lambda *_: (0,) * len(shape), pipeline_mode=pl.Buffered(1))


def _merge_stage(x, o_a, o_b, ob_time_major, proj, mw, tm):
    w_up_a, w_up_b, w_out, g_ffn, w_rt, b_r = mw
    B, T, D = x.shape
    W = o_a.shape[-1]
    E = w_rt.shape[0]
    assert T % tm == 0
    nt = T // tm
    tok = lambda width, col=0: pl.BlockSpec((None, tm, width), lambda b, i, col=col: (b, i, col))
    if ob_time_major:
        o_b = o_b.reshape(T, B * W)
        ob_spec = pl.BlockSpec((tm, W), lambda b, i: (i, b))
    else:
        ob_spec = tok(W)
    return pl.pallas_call(
        _merge_body,
        grid=(B, nt),
        in_specs=[tok(D), tok(W), ob_spec, tok(D, COL_GA), tok(D, COL_GB),
                  _resident((W, D), 2), _resident((W, D), 2), _resident((D, D), 2),
                  _resident((1, D), 2), _resident((E, D), 2), _resident((E, 1), 2)],
        out_specs=[tok(D), pl.BlockSpec((tm * PACK_ROWS, V7X_LANES), lambda b, i: (b * nt + i, 0)),
                   pl.BlockSpec((E, tm), lambda b, i: (0, b * nt + i))],
        out_shape=[jax.ShapeDtypeStruct((B, T, D), f32),
                   jax.ShapeDtypeStruct((B * T * PACK_ROWS, V7X_LANES), u32),
                   jax.ShapeDtypeStruct((E, B * T), f32)],
        compiler_params=_cparams(("parallel", "parallel")),
        name="merge_outproj",
    )(x, o_a, o_b, proj, proj, w_up_a, w_up_b, w_out, g_ffn, w_rt, b_r)


ROUTE_TILE = 512


def _route_body(lg_ref, c0_ref, idx_ref, gate_ref, pos_ref, cnt_ref, carry_ref):
    i = pl.program_id(0)
    E, tt = lg_ref.shape

    @pl.when(i == 0)
    def _():
        carry_ref[...] = c0_ref[...]

    l = lg_ref[...]
    e_iota = lax.broadcasted_iota(i32, (E, tt), 0)
    vals, sels = [], []
    for r in range(TOP_K):
        m = jnp.max(l, axis=0, keepdims=True)
        idx = jnp.min(jnp.where(l == m, e_iota, E), axis=0, keepdims=True)
        sel = e_iota == idx
        l = jnp.where(sel, -jnp.inf, l)
        idx_ref[r:r + 1, :] = idx
        vals.append(m)
        sels.append(sel)
    ex = [jnp.exp(v - vals[0]) for v in vals]
    tot = ex[0] + ex[1] + ex[2] + ex[3]
    for r in range(TOP_K):
        gate_ref[r:r + 1, :] = ex[r] / tot
    member = sels[0] | sels[1] | sels[2] | sels[3]
    mb = jnp.where(member, 1.0, 0.0)
    ti = lax.broadcasted_iota(i32, (tt, tt), 0)
    tj = lax.broadcasted_iota(i32, (tt, tt), 1)
    before = (ti < tj).astype(bf16)
    rank = carry_ref[:, 0:1] + jnp.dot(mb.astype(bf16), before, preferred_element_type=f32)
    for r in range(TOP_K):
        pos_ref[r:r + 1, :] = jnp.sum(jnp.where(sels[r], rank, 0.0), axis=0, keepdims=True).astype(i32)
    carry_ref[...] = carry_ref[...] + jnp.sum(mb, axis=1, keepdims=True)
    cnt_ref[...] = carry_ref[...]


def _route(logits_t, count0):
    E, N = logits_t.shape
    tt = min(ROUTE_TILE, N)
    assert N % tt == 0
    tokspec = pl.BlockSpec((TOP_K, tt), lambda i: (0, i))
    cspec = pl.BlockSpec((E, V7X_LANES), lambda i: (0, 0))
    return pl.pallas_call(
        _route_body,
        grid=(N // tt,),
        in_specs=[pl.BlockSpec((E, tt), lambda i: (0, i)), cspec],
        out_specs=[tokspec, tokspec, tokspec, cspec],
        out_shape=[jax.ShapeDtypeStruct((TOP_K, N), i32), jax.ShapeDtypeStruct((TOP_K, N), f32),
                   jax.ShapeDtypeStruct((TOP_K, N), i32), jax.ShapeDtypeStruct((E, V7X_LANES), f32)],
        scratch_shapes=[pltpu.VMEM((E, V7X_LANES), f32)],
        compiler_params=_cparams(("arbitrary",)),
        name="moe_route",
    )(logits_t, count0)


def _dest_body(idx_ref, pos_ref, start_ref, dest_ref):
    E = start_ref.shape[0]
    tt = idx_ref.shape[1]
    e_iota = lax.broadcasted_iota(i32, (E, tt), 0)
    start = start_ref[:, 0:1]
    for r in range(TOP_K):
        base = jnp.sum(jnp.where(e_iota == idx_ref[r:r + 1, :], start, 0.0), axis=0, keepdims=True)
        dest_ref[r:r + 1, :] = (base.astype(i32) + pos_ref[r:r + 1, :]) * PACK_ROWS


def _dest(idx, pos, pad_start):
    N = idx.shape[1]
    E = pad_start.shape[0]
    tt = min(ROUTE_TILE, N)
    tokspec = pl.BlockSpec((TOP_K, tt), lambda i: (0, i))
    start = jnp.broadcast_to(pad_start.astype(f32)[:, None], (E, V7X_LANES))
    return pl.pallas_call(
        _dest_body,
        grid=(N // tt,),
        in_specs=[tokspec, tokspec, pl.BlockSpec((E, V7X_LANES), lambda i: (0, 0))],
        out_specs=tokspec,
        out_shape=jax.ShapeDtypeStruct((TOP_K, N), i32),
        compiler_params=_cparams(("parallel",)),
        name="moe_dest",
    )(idx, pos, start)


ZERO_ROWS = 256
DMA_PRIORITIES = 2
DMA_LOOP_UNROLL = 8


def _scatter_body(zero_fill, block_rows, zoff_ref, zflag_ref, dest_ref, h_ref, *rest):
    if zero_fill:
        xs_ref, zeros_ref, sem = rest
    else:
        _, xs_ref, zeros_ref, sem = rest
    i = pl.program_id(0)
    tt = dest_ref.shape[1]

    def packed(ref, row, n):
        return ref.at[pl.ds(pl.multiple_of(row * PACK_ROWS, PACK_ROWS), n * PACK_ROWS)]

    def zero_copy(e, piece):
        return pltpu.make_async_copy(
            zeros_ref, packed(xs_ref, zoff_ref[e] + piece * ZERO_ROWS, ZERO_ROWS), sem.at[0])

    if zero_fill:
        @pl.when(i == 0)
        def _():
            zeros_ref[...] = jnp.zeros_like(zeros_ref)
            n_e = zoff_ref.shape[0]
            for phase in ("start", "wait"):
                def per_expert(e, carry, phase=phase):
                    @pl.when(zflag_ref[e] > 0)
                    def _():
                        for piece in range(block_rows // ZERO_ROWS):
                            cp = zero_copy(e, piece)
                            cp.start() if phase == "start" else cp.wait()
                    return carry
                lax.fori_loop(0, n_e, per_expert, 0)

    def row_copy(t, r):
        dst = xs_ref.at[pl.ds(pl.multiple_of(dest_ref[r, t], PACK_ROWS), PACK_ROWS)]
        return pltpu.make_async_copy(packed(h_ref, t, 1), dst, sem.at[1])

    def start(t, carry):
        for r in range(TOP_K):
            row_copy(t, r).start(priority=r % DMA_PRIORITIES)
        return carry

    def wait(t, carry):
        for r in range(TOP_K):
            row_copy(t, r).wait()
        return carry

    lax.fori_loop(0, tt, start, 0, unroll=DMA_LOOP_UNROLL)
    lax.fori_loop(0, tt, wait, 0, unroll=DMA_LOOP_UNROLL)


def _scatter(h2, dest, zoff, zflag, n_rows, block_rows, xs_prev):
    N = h2.shape[0] // PACK_ROWS
    tt = min(ROUTE_TILE, N)
    zero_fill = xs_prev is None
    any_spec = pl.BlockSpec(memory_space=pl.ANY)
    in_specs = [pl.BlockSpec((TOP_K, tt), lambda i, *_: (0, i), memory_space=pltpu.SMEM),
                pl.BlockSpec((tt * PACK_ROWS, V7X_LANES), lambda i, *_: (i, 0))]
    args = [dest, h2]
    aliases = {}
    if not zero_fill:
        in_specs.append(any_spec)
        args.append(xs_prev)
        aliases = {4: 0}
    return pl.pallas_call(
        functools.partial(_scatter_body, zero_fill, block_rows),
        grid_spec=pltpu.PrefetchScalarGridSpec(
            num_scalar_prefetch=2, grid=(N // tt,), in_specs=in_specs, out_specs=any_spec,
            scratch_shapes=[pltpu.VMEM((ZERO_ROWS * PACK_ROWS, V7X_LANES), u32),
                            pltpu.SemaphoreType.DMA((2,))]),
        out_shape=jax.ShapeDtypeStruct((n_rows * PACK_ROWS, V7X_LANES), u32),
        input_output_aliases=aliases,
        compiler_params=_cparams(("arbitrary",)),
        name="moe_scatter",
    )(zoff, zflag, *args)


def _expert_body(nf, n_sub, bexp_ref, nused_ref, nvalid_ref, x_ref, wg_ref, wl_ref, bgu_ref,
                 wd_ref, bd_ref, o_ref, xb_ref, acc_ref):
    b = pl.program_id(0)
    j = pl.program_id(1)
    n_valid = nvalid_ref[b]
    active = n_valid > 0
    tm = xb_ref.shape[0]
    sub = tm // n_sub

    @pl.when(j == 0)
    def _():
        for c in range(PACK_ROWS):
            lo, hi = _unpack_chunk(x_ref, c, tm)
            xb_ref[:, c * PACK_SPAN:c * PACK_SPAN + V7X_LANES] = lo.astype(bf16)
            xb_ref[:, c * PACK_SPAN + V7X_LANES:(c + 1) * PACK_SPAN] = hi.astype(bf16)
        acc_ref[...] = jnp.where(active, jnp.broadcast_to(bd_ref[...], acc_ref.shape), 0.0)

    for q in range(1, n_sub + 1):
        @pl.when((n_valid > (q - 1) * sub) & (n_valid <= q * sub))
        def _():
            rows = slice(0, q * sub)
            xb = xb_ref[rows, :]
            glu = (jnp.dot(xb, wg_ref[...].astype(bf16), preferred_element_type=f32)
                   + bgu_ref[pl.ds(j, 1), :])
            lin = (jnp.dot(xb, wl_ref[...].astype(bf16), preferred_element_type=f32)
                   + bgu_ref[pl.ds(nf + j, 1), :])
            glu = jnp.minimum(glu, SWIGLU_LIMIT)
            lin = jnp.clip(lin, -SWIGLU_LIMIT, SWIGLU_LIMIT)
            act = glu * jax.nn.sigmoid(SWIGLU_ALPHA * glu) * (lin + 1.0)
            acc_ref[rows, :] += jnp.dot(act.astype(bf16), wd_ref[...].astype(bf16),
                                        preferred_element_type=f32)

    @pl.when(j == nf - 1)
    def _():
        _pack_rows(acc_ref[...], o_ref)


def _experts(xs, bexp, nused, nvalid, w_gate_up, b_gate_up, w_down, b_down, tm, tf):
    n_rows = xs.shape[0] // PACK_ROWS
    E, D, F2 = w_gate_up.shape
    F = F2 // 2
    assert n_rows % tm == 0 and F % tf == 0
    nf = F // tf
    nb = n_rows // tm

    def blk(b, nused):
        return jnp.minimum(b, nused[0] - 1)

    def ftile(b, j, nused):
        return jnp.where(b < nused[0], j, nf - 1)

    def x_blk(b, j, nused):
        return blk(b + jnp.where(j >= nf // 2, 1, 0), nused)

    b_gu = b_gate_up.reshape(E, 2 * nf, tf)
    b_d = b_down.reshape(E, 1, D)
    return pl.pallas_call(
        functools.partial(_expert_body, nf, EXPERT_SUB_BLOCKS),
        grid_spec=pltpu.PrefetchScalarGridSpec(
            num_scalar_prefetch=3, grid=(nb, nf),
            in_specs=[
                pl.BlockSpec((tm * PACK_ROWS, V7X_LANES), lambda b, j, be, nu, nv: (x_blk(b, j, nu), 0)),
                pl.BlockSpec((None, D, tf), lambda b, j, be, nu, nv: (be[blk(b, nu)], 0, ftile(b, j, nu))),
                pl.BlockSpec((None, D, tf),
                             lambda b, j, be, nu, nv: (be[blk(b, nu)], 0, nf + ftile(b, j, nu))),
                pl.BlockSpec((None, 2 * nf, tf), lambda b, j, be, nu, nv: (be[blk(b, nu)], 0, 0)),
                pl.BlockSpec((None, tf, D), lambda b, j, be, nu, nv: (be[blk(b, nu)], ftile(b, j, nu), 0)),
                pl.BlockSpec((None, 1, D), lambda b, j, be, nu, nv: (be[blk(b, nu)], 0, 0)),
            ],
            out_specs=pl.BlockSpec((tm * PACK_ROWS, V7X_LANES), lambda b, j, be, nu, nv: (b, 0)),
            scratch_shapes=[pltpu.VMEM((tm, D), bf16), pltpu.VMEM((tm, D), f32)]),
        out_shape=jax.ShapeDtypeStruct((n_rows * PACK_ROWS, V7X_LANES), u32),
        compiler_params=_cparams(("arbitrary", "arbitrary")),
        name="moe_experts",
    )(bexp, nused, nvalid, xs, w_gate_up, w_gate_up, b_gu, w_down, b_d)


COMBINE_TILE = 256


def _combine_body(n, dcur_ref, dnext_ref, x1_ref, gate_ref, gf_ref, ys_ref, y_ref, buf_ref, sem):
    i = pl.program_id(0)
    tt = x1_ref.shape[0]

    def packed(row):
        return pl.ds(pl.multiple_of(row * PACK_ROWS, PACK_ROWS), PACK_ROWS)

    def row_copy(d_ref, slot, t, r):
        src = ys_ref.at[pl.ds(pl.multiple_of(d_ref[r, t], PACK_ROWS), PACK_ROWS)]
        return pltpu.make_async_copy(src, buf_ref.at[slot, r, packed(t)], sem.at[slot])

    def issue(d_ref, slot):
        def body(t, carry):
            for r in range(TOP_K):
                row_copy(d_ref, slot, t, r).start(priority=r % DMA_PRIORITIES)
            return carry
        lax.fori_loop(0, tt, body, 0, unroll=DMA_LOOP_UNROLL)

    @pl.when(i == 0)
    def _():
        issue(dcur_ref, 0)

    @pl.when(i + 1 < n)
    def _():
        issue(dnext_ref, (i + 1) % 2)

    slot = i % 2

    def wait(t, carry):
        for r in range(TOP_K):
            row_copy(dcur_ref, slot, t, r).wait()
        return carry
    lax.fori_loop(0, tt, wait, 0, unroll=DMA_LOOP_UNROLL)

    gates = gate_ref[...]
    parts = []
    for c in range(PACK_ROWS):
        lo_sum = x1_ref[:, c * PACK_SPAN:c * PACK_SPAN + V7X_LANES]
        hi_sum = x1_ref[:, c * PACK_SPAN + V7X_LANES:(c + 1) * PACK_SPAN]
        for r in range(TOP_K):
            lo, hi = _unpack_chunk(buf_ref.at[slot, r], c, tt)
            lo_sum = lo_sum + gates[:, r:r + 1] * lo
            hi_sum = hi_sum + gates[:, r:r + 1] * hi
        parts += [lo_sum, hi_sum]
    x = jnp.concatenate(parts, axis=1)
    ms = jnp.mean(x * x, axis=-1, keepdims=True)
    y_ref[...] = x * lax.rsqrt(ms + RMS_EPS) * gf_ref[...]


def _combine(x1, gates_tm, dest, ys, g_final):
    N, D = x1.shape
    tt = min(COMBINE_TILE, N)
    n = N // tt
    smem = lambda f: pl.BlockSpec((TOP_K, tt), f, memory_space=pltpu.SMEM)
    return pl.pallas_call(
        functools.partial(_combine_body, n),
        grid=(n,),
        in_specs=[smem(lambda i: (0, i)), smem(lambda i: (0, jnp.minimum(i + 1, n - 1))),
                  pl.BlockSpec((tt, D), lambda i: (i, 0)), pl.BlockSpec((tt, TOP_K), lambda i: (i, 0)),
                  pl.BlockSpec((1, D), lambda i: (0, 0)), pl.BlockSpec(memory_space=pl.ANY)],
        out_specs=pl.BlockSpec((tt, D), lambda i: (i, 0)),
        out_shape=jax.ShapeDtypeStruct((N, D), f32),
        scratch_shapes=[pltpu.VMEM((2, TOP_K, tt * PACK_ROWS, V7X_LANES), u32),
                        pltpu.SemaphoreType.DMA((2,))],
        compiler_params=_cparams(("arbitrary",)),
        name="moe_combine",
    )(dest, dest, x1, gates_tm, g_final.reshape(1, D), ys)


def _moe(h2_groups, logit_groups, x1_groups, ew, g_final, tm, tf):
    w_gate_up, b_gate_up, w_down, b_down = ew
    E = w_gate_up.shape[0]
    routed = []
    count = jnp.zeros((E, V7X_LANES), f32)
    for lg in logit_groups:
        idx, gate, pos, count = _route(lg, count)
        routed.append((idx, gate, pos))
    total = sum(lg.shape[1] for lg in logit_groups) * TOP_K
    n_blocks = -(-(total + E * (tm - 1)) // tm)
    n_rows = n_blocks * tm
    counts = count[:, 0].astype(i32)
    padded = (counts + tm - 1) // tm * tm
    pad_end = jnp.cumsum(padded)
    pad_start = pad_end - padded
    n_used = pad_end[-1] // tm
    blocks = jnp.minimum(jnp.arange(n_blocks, dtype=i32), n_used - 1)
    bexp = jnp.minimum(jnp.sum(blocks[:, None] * tm >= pad_end[None, :], axis=1), E - 1).astype(i32)
    all_blocks = jnp.arange(n_blocks, dtype=i32)
    zoff = jnp.concatenate([jnp.maximum(pad_end - tm, 0).astype(i32), all_blocks * tm])
    zflag = jnp.concatenate([counts > 0, all_blocks >= n_used]).astype(i32)
    xs = None
    dests = []
    for h2, (idx, gate, pos) in zip(h2_groups, routed):
        dest = _dest(idx, pos, pad_start)
        dests.append(dest)
        xs = _scatter(h2, dest, zoff, zflag, n_rows, tm, xs)
    row0 = all_blocks * tm
    nvalid = jnp.clip(pad_start[bexp] + counts[bexp] - row0, 0, tm)
    nvalid = jnp.where(all_blocks < n_used, nvalid, 0).astype(i32)
    ys = _experts(xs, bexp, n_used.reshape(1).astype(i32), nvalid, w_gate_up, b_gate_up, w_down, b_down,
                  tm, tf)
    return [_combine(x1, gate.T, dest, ys, g_final)
            for x1, (idx, gate, pos), dest in zip(x1_groups, routed, dests)]


PROJ_TILE_M = 1024
PROJ_TILE_N = 1920
MERGE_TILE_M = 256
EXPERT_TILE_M = 1024
EXPERT_TILE_F = 256
EXPERT_SUB_BLOCKS = 4


def _mixers(x, state, norm_g, w_main, w_u, rw, sw, mw):
    B, T, D = x.shape
    W = w_u.shape[1]
    shift_w = 3 * W + W_LORA + A_LORA + G_LORA
    G, S = sw[0].shape[1] // 64, 64
    if state is None:
        s_rwkv0 = jnp.zeros((B, W // HEAD, HEAD, HEAD), f32)
        s5_re0 = jnp.zeros((B, W // 16, 64), f32)
        s5_im0 = jnp.zeros((B, W // 16, 64), f32)
        shift0 = jnp.zeros((B, shift_w), f32)
    else:
        s_rwkv0, s5_re0, s5_im0, shift0 = state
    per_batch = T % PROJ_TILE_M == 0
    xf = x if per_batch else x.reshape(1, B * T, D)
    tm = PROJ_TILE_M if per_batch else min(PROJ_TILE_M, B * T)
    proj = _norm_proj(xf, norm_g, w_main, tm, PROJ_TILE_N, False, bf16).reshape(B, T, MAIN_COLS)
    if per_batch:
        u_tb = _norm_proj(xf, norm_g, w_u, tm, W, True, f32)
    else:
        u_tb = jnp.swapaxes(_norm_proj(xf, norm_g, w_u, tm, W, False, f32).reshape(B, T, W), 0, 1)
    o_a, s_rwkv = _rwkv_stage(proj, shift0, s_rwkv0, rw)
    o_b, s5_re, s5_im = _s5_stage(u_tb, s5_re0, s5_im0, sw)
    if per_batch:
        x1, h2, logits_t = _merge_stage(x, o_a, o_b, True, proj, mw, MERGE_TILE_M)
    else:
        flat = lambda t: t.reshape(1, B * T, t.shape[-1])
        x1, h2, logits_t = _merge_stage(flat(x), flat(o_a), flat(jnp.swapaxes(o_b, 0, 1)), False,
                                        flat(proj), mw, min(MERGE_TILE_M, B * T))
    last = proj[:, -1, :].astype(f32)
    n_lora = W_LORA + A_LORA + G_LORA
    shift = jnp.concatenate([last[:, COL_R * W:(COL_R + 3) * W],
                             last[:, COL_LORA * LORA_PAD:COL_LORA * LORA_PAD + n_lora]], axis=1)
    return (x1.reshape(B * T, D), h2, logits_t), (s_rwkv, s5_re, s5_im, shift)


def kernel(x_prompt, x_sample, state_rwkv, state_s5_re, state_s5_im, state_shift, norm_mix_g, w_in, mu_shift, w0, w2, a0, a2, g2, k_k, k_a, r_k, lnx_g, lnx_b, lam_re, lam_im, log_dt, b_re, b_im, c_re, c_im, d_skip, w_glu, b_glu, w_up_a, w_up_b, w_out, norm_ffn_g, w_router, b_router, w_gate_up, b_gate_up, w_down, b_down, norm_final_g):
    assert w_in.shape[0] == 1, "single-layer trunk"
    D = x_prompt.shape[-1]
    W = w0.shape[-1]
    n_lora = W_LORA + A_LORA + G_LORA
    shift_w = 3 * W + n_lora
    w = w_in[0]
    ga0 = shift_w + W
    w_main = jnp.concatenate(
        [w[:, ga0:ga0 + 2 * D], w[:, :3 * W], w[:, 3 * W:shift_w],
         jnp.zeros((D, LORA_PAD - n_lora), w.dtype)], axis=1).astype(bf16)
    assert w_main.shape[1] == MAIN_COLS
    w_u = w[:, shift_w:shift_w + W].astype(bf16)
    rw = _rwkv_weights(mu_shift[0], w0[0], w2[0], a0[0], a2[0], g2[0], k_k[0], k_a[0], r_k[0],
                       lnx_g[0], lnx_b[0])
    sw = _s5_weights(lam_re[0], lam_im[0], log_dt[0], b_re[0], b_im[0], c_re[0], c_im[0],
                     d_skip[0], w_glu[0], b_glu[0])
    E = w_router.shape[-1]
    mw = (w_up_a[0].astype(bf16), w_up_b[0].astype(bf16), w_out[0].astype(bf16),
          norm_ffn_g[0].reshape(1, D).astype(f32), w_router[0].T.astype(bf16),
          b_router[0].reshape(E, 1).astype(f32))
    ew = (w_gate_up[0], b_gate_up[0], w_down[0], b_down[0])

    state_s = (state_rwkv[0], state_s5_re[0], state_s5_im[0], state_shift[0])
    tok_p, st_p = _mixers(x_prompt, None, norm_mix_g[0], w_main, w_u, rw, sw, mw)
    tok_s, st_s = _mixers(x_sample, state_s, norm_mix_g[0], w_main, w_u, rw, sw, mw)
    y_p, y_s = _moe([tok_p[1], tok_s[1]], [tok_p[2], tok_s[2]], [tok_p[0], tok_s[0]], ew, norm_final_g,
                    EXPERT_TILE_M, EXPERT_TILE_F)
    lead = lambda t: t[None]
    return (y_p.reshape(x_prompt.shape), y_s.reshape(x_sample.shape),
            lead(st_p[0]), lead(st_p[1]), lead(st_p[2]), lead(st_p[3]),
            lead(st_s[0]), lead(st_s[1]), lead(st_s[2]), lead(st_s[3]))
```

```python
import functools
import math

import jax
import jax.numpy as jnp
from jax import lax
from jax.experimental import pallas as pl
from jax.experimental.pallas import tpu as pltpu

f32 = jnp.float32
bf16 = jnp.bfloat16
i32 = jnp.int32
u32 = jnp.uint32

V7X_LANES = 128
V7X_SUBLANES = 8
V7X_VMEM_BYTES = 64 * 1024 * 1024
VMEM_LIMIT = V7X_VMEM_BYTES * 7 // 8

RMS_EPS = 1e-5
GN_EPS = 64e-5
HEAD = 64
HEAD_SHIFT = 6
HEADS_PER_GROUP = 4
GROUP = HEAD * HEADS_PER_GROUP
CHUNK = 64
W_LORA, A_LORA, G_LORA = 64, 64, 160
LORA_PAD = 512
TOP_K = 4
SWIGLU_LIMIT = 7.0
SWIGLU_ALPHA = 1.702


def _cparams(sem):
    return pltpu.CompilerParams(dimension_semantics=sem, vmem_limit_bytes=VMEM_LIMIT)


def _dot(a, b):
    return jnp.dot(a.astype(bf16), b.astype(bf16), preferred_element_type=f32)


def _dot_nt(a, b):
    return lax.dot_general(a.astype(bf16), b.astype(bf16), (((1,), (1,)), ((), ())),
                           preferred_element_type=f32)


def _dot_tn(a, b):
    return lax.dot_general(a.astype(bf16), b.astype(bf16), (((0,), (0,)), ((), ())),
                           preferred_element_type=f32)


def _split(x):
    hi = x.astype(bf16)
    lo = (x - hi.astype(f32)).astype(bf16)
    return hi, lo


def _dot_exact_rhs(a, b_exact):
    hi, lo = _split(a)
    return (jnp.dot(hi, b_exact, preferred_element_type=f32)
            + jnp.dot(lo, b_exact, preferred_element_type=f32))


def _dot_exact_lhs(a_exact, b):
    hi, lo = _split(b)
    return (jnp.dot(a_exact, hi, preferred_element_type=f32)
            + jnp.dot(a_exact, lo, preferred_element_type=f32))


def _norm_proj_body(x_ref, g_ref, w_ref, o_ref, h_ref):
    @pl.when(pl.program_id(2) == 0)
    def _():
        x = x_ref[...]
        ms = jnp.mean(x * x, axis=-1, keepdims=True)
        h_ref[...] = (x * lax.rsqrt(ms + RMS_EPS) * g_ref[...]).astype(bf16)

    o_ref[...] = jnp.dot(h_ref[...], w_ref[...], preferred_element_type=f32).astype(o_ref.dtype)


def _norm_proj(x, gain, w, tm, tn, time_major_out, out_dtype):
    B, T, D = x.shape
    N = w.shape[1]
    assert T % tm == 0 and N % tn == 0
    if time_major_out:
        out_shape = jax.ShapeDtypeStruct((T, B * N), out_dtype)
        out_spec = pl.BlockSpec((tm, tn), lambda b, i, n: (i, b * (N // tn) + n))
    else:
        out_shape = jax.ShapeDtypeStruct((B, T, N), out_dtype)
        out_spec = pl.BlockSpec((None, tm, tn), lambda b, i, n: (b, i, n))
    out = pl.pallas_call(
        _norm_proj_body,
        grid=(B, T // tm, N // tn),
        in_specs=[pl.BlockSpec((None, tm, D), lambda b, i, n: (b, i, 0)),
                  pl.BlockSpec((1, D), lambda b, i, n: (0, 0)),
                  pl.BlockSpec((D, tn), lambda b, i, n: (0, n))],
        out_specs=out_spec,
        out_shape=out_shape,
        scratch_shapes=[pltpu.VMEM((tm, D), bf16)],
        compiler_params=_cparams(("parallel", "parallel", "arbitrary")),
        name="norm_proj",
    )(x, gain.reshape(1, D), w)
    return out.reshape(T, B, N) if time_major_out else out


PV_MU_R, PV_MU_K, PV_MU_V, PV_W0, PV_A0, PV_KK, PV_KA, PV_RK, PV_LNG, PV_LNB = range(10)
PV_ROWS = 16
PREV_ROWS = 16
RWKV_CHUNKS_PER_STEP = 4


def _rwkv_body(nc, r_ref, k_ref, v_ref, l_ref, pr_ref, pk_ref, pv_ref, plo_ref,
               lr_ref, lk_ref, lv_ref, llo_ref, pvec_ref, mul_ref, wl_ref, s0_ref,
               o_ref, so_ref, s_ref):
    c = pl.program_id(1)
    C = CHUNK
    R, W = r_ref.shape
    n_chunks = R // C
    n_groups = W // GROUP

    @pl.when(c == 0)
    def _():
        for gq in range(n_groups):
            for h in range(HEADS_PER_GROUP):
                pieces = [jnp.zeros((HEAD, HEAD), f32)] * HEADS_PER_GROUP
                pieces[h] = s0_ref[gq * HEADS_PER_GROUP + h]
                s_ref[gq, h * HEAD:(h + 1) * HEAD, :] = jnp.concatenate(pieces, axis=1)

    row = lax.broadcasted_iota(i32, (R, 1), 0)
    pvec = pvec_ref[...]
    prm = lambda i: pvec[i:i + 1, :]

    gi = lax.broadcasted_iota(i32, (GROUP, GROUP), 0)
    gj = lax.broadcasted_iota(i32, (GROUP, GROUP), 1)
    same_head = (gi >> HEAD_SHIFT) == (gj >> HEAD_SHIFT)
    head_ones = jnp.where(same_head, 1.0, 0.0).astype(bf16)
    qt = lax.broadcasted_iota(i32, (C, GROUP), 0)
    qs = lax.broadcasted_iota(i32, (C, GROUP), 1) & (HEAD - 1)
    strict_q, incl_q = qt > qs, qt >= qs
    ci = lax.broadcasted_iota(i32, (R, R), 0)
    cj = lax.broadcasted_iota(i32, (R, R), 1)
    tri = jnp.where((ci >= cj) & ((ci >> HEAD_SHIFT) == (cj >> HEAD_SHIFT)), 1.0, 0.0).astype(bf16)
    lane_head = lax.broadcasted_iota(i32, (1, GROUP), 1) >> HEAD_SHIFT
    eye_quad = jnp.where(qt == qs, 1.0, 0.0)

    def head_sum(x, exact):
        parts = []
        for gq in range(n_groups):
            xg = x[:, gq * GROUP:(gq + 1) * GROUP]
            parts.append(_dot_exact_rhs(xg, head_ones) if exact
                         else jnp.dot(xg.astype(bf16), head_ones, preferred_element_type=f32))
        return jnp.concatenate(parts, axis=1)

    def shifted(x_ref, prev_ref, last_ref, mu):
        x = x_ref[...].astype(f32)
        prev_row = jnp.where(c > 0, prev_ref[PREV_ROWS - 1:PREV_ROWS, :].astype(f32), last_ref[...])
        prev = jnp.where(row == 0, prev_row, pltpu.roll(x, shift=1, axis=0))
        return x + (prev - x) * mu

    r = shifted(r_ref, pr_ref, lr_ref, prm(PV_MU_R))
    k = shifted(k_ref, pk_ref, lk_ref, prm(PV_MU_K))
    v = shifted(v_ref, pv_ref, lv_ref, prm(PV_MU_V))
    xl = shifted(l_ref, plo_ref, llo_ref, mul_ref[...])
    lane_l = lax.broadcasted_iota(i32, (1, xl.shape[1]), 1)
    act = jnp.where(lane_l < W_LORA, jnp.tanh(xl),
                    jnp.where(lane_l < W_LORA + A_LORA, xl, jax.nn.sigmoid(xl)))
    lo = jnp.dot(act.astype(bf16), wl_ref[...], preferred_element_type=f32)
    z = -(prm(PV_W0) + lo[:, :W])
    w = -(jnp.maximum(z, 0.0) + jnp.log(1.0 + jnp.exp(-jnp.abs(z)))) - 0.5
    logd = -jnp.exp(w)
    a = jax.nn.sigmoid(prm(PV_A0) + lo[:, W:2 * W])
    g = lo[:, 2 * W:]
    kk = k * prm(PV_KK)
    kk = kk * lax.rsqrt(jnp.maximum(head_sum(kk * kk, True), 1e-24))
    k2 = k * (1.0 + (a - 1.0) * prm(PV_KA))
    cl = _dot_exact_lhs(tri, logd)
    p_incl = jnp.exp(cl)
    p_excl = jnp.exp(cl - logd)
    p_inv = jnp.exp(-cl)
    rt = r * p_incl
    at = -kk * p_excl
    bt = kk * a * p_inv
    kt = k2 * p_inv
    bonus = head_sum(r * k2 * prm(PV_RK), False) * v

    def stack(x):
        return jnp.concatenate([jnp.where(lane_head == h, x, 0.0) for h in range(HEADS_PER_GROUP)], axis=0)

    def tile(x):
        return jnp.concatenate([x] * HEADS_PER_GROUP, axis=0)

    def block(x):
        return jnp.where(same_head, tile(x), 0.0)

    groups = range(n_groups)
    each = lambda f, *lists: [f(*xs) for xs in zip(*lists)]
    lanes = [slice(gq * GROUP, (gq + 1) * GROUP) for gq in groups]
    tiles = [(slice(ch * C, (ch + 1) * C), sl) for ch in range(n_chunks) for sl in lanes]
    ag, rg, bg, kg, vg = ([x[rows, sl] for rows, sl in tiles] for x in (at, rt, bt, kt, v))
    sa, sv = each(stack, ag), each(stack, vg)
    bd_t = lambda y: jnp.where(same_head, tile(y).T, 0.0)
    tb, tk = each(bd_t, bg), each(bd_t, kg)
    above = lambda x: x[:C]
    below = lambda x: x[C:]
    pair = lambda x, y: jnp.concatenate([x, y], axis=0)
    ar = each(pair, ag, rg)
    arb, ark = each(_dot, ar, tb), each(_dot, ar, tk)
    ab = each(lambda x: jnp.where(strict_q, above(x), 0.0), arb)
    rb = each(lambda x: jnp.where(incl_q, below(x), 0.0), arb)
    ak = each(lambda x: jnp.where(strict_q, above(x), 0.0), ark)
    rk = each(lambda x: jnp.where(incl_q, below(x), 0.0), ark)
    t4 = each(lambda q: eye_quad + q, ab)
    q4 = each(lambda q: _dot(q, block(q)), ab)
    n_sq = max(1, (C - 1).bit_length() - 1)
    for it in range(n_sq):
        if it < n_sq - 1:
            tq = each(lambda t, q: _dot(pair(t, q), block(q)), t4, q4)
            t4 = each(lambda t, x: t + above(x), t4, tq)
            q4 = each(below, tq)
        else:
            t4 = each(lambda t, q: t + _dot(t, block(q)), t4, q4)
    w4 = each(_dot, t4, sa)
    akr = each(lambda x, y, z: _dot(pair(x, y), z), ak, rk, sv)
    uv = each(lambda t, x: _dot(t, stack(above(x))), t4, akr)
    orv = each(below, akr)
    state = [s_ref[gq] for gq in groups]
    o_rows = []
    for ch in range(n_chunks):
        of = lambda xs: xs[ch * n_groups:(ch + 1) * n_groups]
        wr = each(lambda w_, r_, s_: _dot_nt(pair(w_, r_), s_), of(w4), of(rg), state)
        u4 = each(lambda x, uv_: above(x) + uv_, wr, of(uv))
        rs = each(below, wr)
        upd = each(lambda u_, v_, b_, k_: _dot_tn(jnp.concatenate([u_, v_], axis=0),
                                                  jnp.concatenate([b_, k_], axis=0)),
                   u4, of(vg), of(bg), of(kg))
        o4 = each(lambda rs_, rb_, u_, orv_: rs_ + _dot(rb_, stack(u_)) + orv_,
                  rs, of(rb), u4, of(orv))
        p_end = p_incl[(ch + 1) * C - 1:(ch + 1) * C, :]
        state = [(s + jnp.where(same_head, d, 0.0)) * p_end[:, sl] for s, d, sl in zip(state, upd, lanes)]
        o_rows.append(jnp.concatenate(o4, axis=1))
    for gq in groups:
        s_ref[gq] = state[gq]
    o = jnp.concatenate(o_rows, axis=0)

    inv_n = 1.0 / HEAD
    mean = head_sum(o, False) * inv_n
    d = o - mean
    var = head_sum(d * d, False) * inv_n
    on = d * lax.rsqrt(var + GN_EPS)
    o_ref[...] = (on * prm(PV_LNG) + prm(PV_LNB) + bonus) * g

    @pl.when(c == nc - 1)
    def _():
        for gq in range(n_groups):
            for h in range(HEADS_PER_GROUP):
                hs = slice(h * HEAD, (h + 1) * HEAD)
                so_ref[gq * HEADS_PER_GROUP + h] = s_ref[gq, hs, hs]


def _rwkv(proj, col_r, col_lora, last_rkv, last_lora, pvec, mu_lora, w_lora, s0):
    B, T, _ = proj.shape
    W = pvec.shape[1]
    C = CHUNK
    assert T % C == 0 and C == HEAD and HEAD == 1 << HEAD_SHIFT
    n_groups = W // GROUP
    R = C * RWKV_CHUNKS_PER_STEP if T % (C * RWKV_CHUNKS_PER_STEP) == 0 else C
    rb = R // PREV_ROWS

    def cur(col, width):
        return pl.BlockSpec((None, R, width), lambda b, c, col=col: (b, c, col))

    def prev(col, width):
        return pl.BlockSpec((None, PREV_ROWS, width),
                            lambda b, c, col=col: (b, jnp.maximum(c * rb - 1, 0), col))

    def last(col, width):
        return pl.BlockSpec((None, 1, width), lambda b, c, col=col: (b, 0, col))

    const = lambda shape: pl.BlockSpec(shape, lambda b, c: (0,) * len(shape))
    H = W // HEAD
    state_spec = pl.BlockSpec((None, H, HEAD, HEAD), lambda b, c: (b, 0, 0, 0))
    return pl.pallas_call(
        functools.partial(_rwkv_body, T // R),
        grid=(B, T // R),
        in_specs=[cur(col_r, W), cur(col_r + 1, W), cur(col_r + 2, W), cur(col_lora, LORA_PAD),
                  prev(col_r, W), prev(col_r + 1, W), prev(col_r + 2, W), prev(col_lora, LORA_PAD),
                  last(0, W), last(1, W), last(2, W), last(0, LORA_PAD),
                  const((PV_ROWS, W)), const((1, LORA_PAD)), const((LORA_PAD, 3 * W)), state_spec],
        out_specs=[pl.BlockSpec((None, R, W), lambda b, c: (b, c, 0)), state_spec],
        out_shape=[jax.ShapeDtypeStruct((B, T, W), f32),
                   jax.ShapeDtypeStruct((B, H, HEAD, HEAD), f32)],
        scratch_shapes=[pltpu.VMEM((n_groups, GROUP, GROUP), f32)],
        compiler_params=_cparams(("parallel", "arbitrary")),
        name="rwkv7_chunked",
    )(proj, proj, proj, proj, proj, proj, proj, proj,
      last_rkv, last_rkv, last_rkv, last_lora, pvec, mu_lora, w_lora, s0)


COL_GA, COL_GB = 0, 1
COL_R = 4
COL_LORA = 14
MAIN_COLS = 7680


def _rwkv_weights(mu_shift, w0, w2, a0, a2, g2, k_k, k_a, r_k, lnx_g, lnx_b):
    W = w0.shape[0]
    rows = [mu_shift[:W], mu_shift[W:2 * W], mu_shift[2 * W:3 * W], w0, a0, k_k, k_a,
            r_k.reshape(W), lnx_g, lnx_b]
    pvec = jnp.zeros((PV_ROWS, W), f32).at[:len(rows)].set(jnp.stack(rows).astype(f32))
    n_lora = W_LORA + A_LORA + G_LORA
    mu_lora = jnp.zeros((1, LORA_PAD), f32).at[0, :n_lora].set(mu_shift[3 * W:])
    w_lora = jnp.zeros((LORA_PAD, 3 * W), f32)
    w_lora = w_lora.at[:W_LORA, :W].set(w2)
    w_lora = w_lora.at[W_LORA:W_LORA + A_LORA, W:2 * W].set(a2)
    w_lora = w_lora.at[W_LORA + A_LORA:n_lora, 2 * W:].set(g2)
    return pvec, mu_lora, w_lora.astype(bf16)


def _rwkv_stage(proj, shift0, s0, rw):
    pvec, mu_lora, w_lora = rw
    B = proj.shape[0]
    W = pvec.shape[1]
    last_rkv = shift0[:, None, :3 * W].astype(f32)
    last_lora = jnp.zeros((B, 1, LORA_PAD), f32).at[:, 0, :shift0.shape[1] - 3 * W].set(shift0[:, 3 * W:])
    return _rwkv(proj, COL_R, COL_LORA, last_rkv, last_lora, pvec, mu_lora, w_lora, s0.astype(f32))


S5_GROUP = 16
S5_STATE = 64
S5_BLOCK_IN = 128
S5_BLOCK_STATE = 512
S5_STEP_T = 64


def _s5_prep_body(lr_ref, li_ref, ldt_ref, br_ref, bi_ref, lbr_ref, lbi_ref, bbr_ref, bbi_ref):
    lr, li = lr_ref[...], li_ref[...]
    dt = jnp.exp(ldt_ref[...])
    mag = jnp.exp(lr * dt)
    er, ei = mag * jnp.cos(li * dt), mag * jnp.sin(li * dt)
    lbr_ref[...] = er
    lbi_ref[...] = ei
    den = lr * lr + li * li
    cr = ((er - 1.0) * lr + ei * li) / den
    ci = (ei * lr - (er - 1.0) * li) / den
    br, bi = br_ref[...], bi_ref[...]
    bbr_ref[...] = cr * br - ci * bi
    bbi_ref[...] = cr * bi + ci * br


def _s5_weights(lam_re, lam_im, log_dt, b_re, b_im, c_re, c_im, d_skip, w_glu, b_glu):
    G, S = lam_re.shape
    P = b_re.shape[-1]
    n = G * S
    col = lambda t: t.reshape(n, 1).astype(f32)
    ldt = jnp.broadcast_to(log_dt[:, None], (G, S))
    shapes = [jax.ShapeDtypeStruct((n, 1), f32)] * 2 + [jax.ShapeDtypeStruct((n, P), f32)] * 2
    lbr, lbi, bbr, bbi = pl.pallas_call(_s5_prep_body, out_shape=shapes, name="s5_discretise")(
        col(lam_re), col(lam_im), col(ldt), b_re.reshape(n, P).astype(f32), b_im.reshape(n, P).astype(f32))
    gpb = S5_BLOCK_IN // P
    nb = G // gpb
    eye = jnp.eye(gpb, dtype=f32)

    def in_block(bb):
        t = bb.reshape(nb, gpb, S, P)
        return jnp.einsum('qgsp,gh->qgphs', t, eye).reshape(nb, gpb * P, gpb * S)

    def out_block(cc):
        t = cc.astype(f32).reshape(nb, gpb, P, S)
        return jnp.einsum('qgps,gh->qgshp', t, eye).reshape(nb, gpb * S, gpb * P)

    wb = jnp.concatenate([in_block(bbr), in_block(bbi)], axis=2).astype(bf16)
    wc = jnp.concatenate([out_block(c_re), -out_block(c_im)], axis=1).astype(bf16)
    W = G * P
    return (lbr.reshape(1, n), lbi.reshape(1, n), wb, wc, d_skip.reshape(1, W).astype(f32),
            w_glu.astype(bf16), b_glu.reshape(1, W).astype(f32))


def _s5_body(n_steps, u_ref, x0r_ref, x0i_ref, lbr_ref, lbi_ref, wb_ref, wc_ref, d_ref, wg_ref, bg_ref,
             o_ref, xr_out, xi_out, bur_ref, bui_ref, xr_ref, xi_ref, y_ref):
    i = pl.program_id(1)
    Tc, Bs, W = u_ref.shape
    rows = Tc * Bs
    nb = wb_ref.shape[0]
    ns = S5_BLOCK_STATE

    @pl.when(i == 0)
    def _():
        xr_ref[...] = x0r_ref[...]
        xi_ref[...] = x0i_ref[...]

    u = u_ref[...].reshape(rows, W)
    for q in range(nb):
        bu = jnp.dot(u[:, q * S5_BLOCK_IN:(q + 1) * S5_BLOCK_IN].astype(bf16), wb_ref[q],
                     preferred_element_type=f32)
        bur_ref[:, q * ns:(q + 1) * ns] = bu[:, :ns]
        bui_ref[:, q * ns:(q + 1) * ns] = bu[:, ns:]

    for q in range(nb):
        sl = slice(q * ns, (q + 1) * ns)
        lr = jnp.broadcast_to(lbr_ref[:, sl], (Bs, ns))
        li = jnp.broadcast_to(lbi_ref[:, sl], (Bs, ns))

        def step(t, carry):
            xr, xi = carry
            rs = pl.ds(pl.multiple_of(t * Bs, Bs), Bs)
            nr = lr * xr - li * xi + bur_ref[rs, sl]
            ni = lr * xi + li * xr + bui_ref[rs, sl]
            bur_ref[rs, sl] = nr
            bui_ref[rs, sl] = ni
            return nr, ni

        xr, xi = lax.fori_loop(0, Tc, step, (xr_ref[:, sl], xi_ref[:, sl]), unroll=True)
        xr_ref[:, sl] = xr
        xi_ref[:, sl] = xi
        y_ref[:, q * S5_BLOCK_IN:(q + 1) * S5_BLOCK_IN] = (
            jnp.dot(bur_ref[:, sl].astype(bf16), wc_ref[q, :ns, :], preferred_element_type=f32)
            + jnp.dot(bui_ref[:, sl].astype(bf16), wc_ref[q, ns:, :], preferred_element_type=f32))

    y = y_ref[...] + d_ref[...] * u
    y = 0.5 * y * (1.0 + jnp.tanh(math.sqrt(2.0 / math.pi) * (y + 0.044715 * (y * y * y))))
    z = jnp.dot(y.astype(bf16), wg_ref[...], preferred_element_type=f32) + bg_ref[...]
    o_ref[...] = (y * jax.nn.sigmoid(z)).reshape(Tc, Bs, W)

    @pl.when(i == n_steps - 1)
    def _():
        xr_out[...] = xr_ref[...]
        xi_out[...] = xi_ref[...]


def _s5_stage(u_tb, x0_re, x0_im, sw):
    lbr, lbi, wb, wc, d_skip, w_glu, b_glu = sw
    T, B, W = u_tb.shape
    G, S = x0_re.shape[1:]
    n = G * S
    Bs = V7X_SUBLANES
    Tc = min(S5_STEP_T, T)
    assert B % Bs == 0 and T % Tc == 0
    rows = Tc * Bs
    const = lambda shape: pl.BlockSpec(shape, lambda j, i: (0,) * len(shape))
    state_spec = pl.BlockSpec((Bs, n), lambda j, i: (j, 0))
    o, xr, xi = pl.pallas_call(
        functools.partial(_s5_body, T // Tc),
        grid=(B // Bs, T // Tc),
        in_specs=[pl.BlockSpec((Tc, Bs, W), lambda j, i: (i, j, 0)), state_spec, state_spec,
                  const((1, n)), const((1, n)), const(wb.shape), const(wc.shape),
                  const((1, W)), const((W, W)), const((1, W))],
        out_specs=[pl.BlockSpec((Tc, Bs, W), lambda j, i: (i, j, 0)), state_spec, state_spec],
        out_shape=[jax.ShapeDtypeStruct((T, B, W), f32),
                   jax.ShapeDtypeStruct((B, n), f32), jax.ShapeDtypeStruct((B, n), f32)],
        scratch_shapes=[pltpu.VMEM((rows, n), f32), pltpu.VMEM((rows, n), f32),
                        pltpu.VMEM((Bs, n), f32), pltpu.VMEM((Bs, n), f32), pltpu.VMEM((rows, W), f32)],
        compiler_params=_cparams(("parallel", "arbitrary")),
        name="s5_scan",
    )(u_tb, x0_re.reshape(B, n).astype(f32), x0_im.reshape(B, n).astype(f32),
      lbr, lbi, wb, wc, d_skip, w_glu, b_glu)
    return o, xr.reshape(B, G, S), xi.reshape(B, G, S)


PACK_ROWS = V7X_SUBLANES
PACK_SPAN = 2 * V7X_LANES
HIGH_HALF = 0xFFFF0000


def _pack_rows(x, out_ref):
    n, D = x.shape
    assert D == PACK_ROWS * PACK_SPAN
    bits = lax.bitcast_convert_type(x.astype(bf16).astype(f32), u32)
    for c in range(PACK_ROWS):
        lo = bits[:, c * PACK_SPAN:c * PACK_SPAN + V7X_LANES]
        hi = bits[:, c * PACK_SPAN + V7X_LANES:(c + 1) * PACK_SPAN]
        out_ref[pl.ds(c, n, stride=PACK_ROWS), :] = (lo >> 16) | (hi & jnp.uint32(HIGH_HALF))


def _unpack_chunk(ref, c, n):
    words = ref[pl.ds(c, n, stride=PACK_ROWS), :]
    lo = lax.bitcast_convert_type(words << 16, f32)
    hi = lax.bitcast_convert_type(words & jnp.uint32(HIGH_HALF), f32)
    return lo, hi


def _merge_body(x_ref, oa_ref, ob_ref, ga_ref, gb_ref, wa_ref, wb_ref, wo_ref, gf_ref, wr_ref, br_ref,
                x1_ref, h2_ref, lg_ref):
    ya = jnp.dot(oa_ref[...].astype(bf16), wa_ref[...], preferred_element_type=f32)
    yb = jnp.dot(ob_ref[...].astype(bf16), wb_ref[...], preferred_element_type=f32)
    merged = (jax.nn.sigmoid(ga_ref[...].astype(f32)) * ya
              + jax.nn.sigmoid(gb_ref[...].astype(f32)) * yb)
    x1 = x_ref[...] + jnp.dot(merged.astype(bf16), wo_ref[...], preferred_element_type=f32)
    x1_ref[...] = x1
    ms = jnp.mean(x1 * x1, axis=-1, keepdims=True)
    h2 = x1 * lax.rsqrt(ms + RMS_EPS) * gf_ref[...]
    _pack_rows(h2, h2_ref)
    lg_ref[...] = _dot_nt(wr_ref[...], h2) + br_ref[...]


def _resident(shape):
    return pl.BlockSpec(shape, lambda *_: (0,) * len(shape), pipeline_mode=pl.Buffered(1))


def _merge_stage(x, o_a, o_b, ob_time_major, proj, mw, tm):
    w_up_a, w_up_b, w_out, g_ffn, w_rt, b_r = mw
    B, T, D = x.shape
    W = o_a.shape[-1]
    E = w_rt.shape[0]
    assert T % tm == 0
    nt = T // tm
    tok = lambda width, col=0: pl.BlockSpec((None, tm, width), lambda b, i, col=col: (b, i, col))
    if ob_time_major:
        o_b = o_b.reshape(T, B * W)
        ob_spec = pl.BlockSpec((tm, W), lambda b, i: (i, b))
    else:
        ob_spec = tok(W)
    return pl.pallas_call(
        _merge_body,
        grid=(B, nt),
        in_specs=[tok(D), tok(W), ob_spec, tok(D, COL_GA), tok(D, COL_GB),
                  _resident((W, D)), _resident((W, D)), _resident((D, D)),
                  _resident((1, D)), _resident((E, D)), _resident((E, 1))],
        out_specs=[tok(D), pl.BlockSpec((tm * PACK_ROWS, V7X_LANES), lambda b, i: (b * nt + i, 0)),
                   pl.BlockSpec((E, tm), lambda b, i: (0, b * nt + i))],
        out_shape=[jax.ShapeDtypeStruct((B, T, D), f32),
                   jax.ShapeDtypeStruct((B * T * PACK_ROWS, V7X_LANES), u32),
                   jax.ShapeDtypeStruct((E, B * T), f32)],
        compiler_params=_cparams(("parallel", "parallel")),
        name="merge_outproj",
    )(x, o_a, o_b, proj, proj, w_up_a, w_up_b, w_out, g_ffn, w_rt, b_r)


ROUTE_TILE = 512


def _route_body(lg_ref, c0_ref, idx_ref, gate_ref, pos_ref, cnt_ref, carry_ref):
    i = pl.program_id(0)
    E, tt = lg_ref.shape

    @pl.when(i == 0)
    def _():
        carry_ref[...] = c0_ref[...]

    l = lg_ref[...]
    e_iota = lax.broadcasted_iota(i32, (E, tt), 0)
    vals, sels = [], []
    for r in range(TOP_K):
        m = jnp.max(l, axis=0, keepdims=True)
        idx = jnp.min(jnp.where(l == m, e_iota, E), axis=0, keepdims=True)
        sel = e_iota == idx
        l = jnp.where(sel, -jnp.inf, l)
        idx_ref[r:r + 1, :] = idx
        vals.append(m)
        sels.append(sel)
    ex = [jnp.exp(v - vals[0]) for v in vals]
    tot = ex[0] + ex[1] + ex[2] + ex[3]
    for r in range(TOP_K):
        gate_ref[r:r + 1, :] = ex[r] / tot
    member = sels[0] | sels[1] | sels[2] | sels[3]
    mb = jnp.where(member, 1.0, 0.0)
    ti = lax.broadcasted_iota(i32, (tt, tt), 0)
    tj = lax.broadcasted_iota(i32, (tt, tt), 1)
    before = (ti < tj).astype(bf16)
    rank = carry_ref[:, 0:1] + jnp.dot(mb.astype(bf16), before, preferred_element_type=f32)
    for r in range(TOP_K):
        pos_ref[r:r + 1, :] = jnp.sum(jnp.where(sels[r], rank, 0.0), axis=0, keepdims=True).astype(i32)
    carry_ref[...] = carry_ref[...] + jnp.sum(mb, axis=1, keepdims=True)
    cnt_ref[...] = carry_ref[...]


def _route(logits_t, count0):
    E, N = logits_t.shape
    tt = min(ROUTE_TILE, N)
    assert N % tt == 0
    tokspec = pl.BlockSpec((TOP_K, tt), lambda i: (0, i))
    cspec = pl.BlockSpec((E, V7X_LANES), lambda i: (0, 0))
    return pl.pallas_call(
        _route_body,
        grid=(N // tt,),
        in_specs=[pl.BlockSpec((E, tt), lambda i: (0, i)), cspec],
        out_specs=[tokspec, tokspec, tokspec, cspec],
        out_shape=[jax.ShapeDtypeStruct((TOP_K, N), i32), jax.ShapeDtypeStruct((TOP_K, N), f32),
                   jax.ShapeDtypeStruct((TOP_K, N), i32), jax.ShapeDtypeStruct((E, V7X_LANES), f32)],
        scratch_shapes=[pltpu.VMEM((E, V7X_LANES), f32)],
        compiler_params=_cparams(("arbitrary",)),
        name="moe_route",
    )(logits_t, count0)


def _dest_body(idx_ref, pos_ref, start_ref, dest_ref):
    E = start_ref.shape[0]
    tt = idx_ref.shape[1]
    e_iota = lax.broadcasted_iota(i32, (E, tt), 0)
    start = start_ref[:, 0:1]
    for r in range(TOP_K):
        base = jnp.sum(jnp.where(e_iota == idx_ref[r:r + 1, :], start, 0.0), axis=0, keepdims=True)
        dest_ref[r:r + 1, :] = base.astype(i32) + pos_ref[r:r + 1, :]


def _dest(idx, pos, pad_start):
    N = idx.shape[1]
    E = pad_start.shape[0]
    tt = min(ROUTE_TILE, N)
    tokspec = pl.BlockSpec((TOP_K, tt), lambda i: (0, i))
    start = jnp.broadcast_to(pad_start.astype(f32)[:, None], (E, V7X_LANES))
    return pl.pallas_call(
        _dest_body,
        grid=(N // tt,),
        in_specs=[tokspec, tokspec, pl.BlockSpec((E, V7X_LANES), lambda i: (0, 0))],
        out_specs=tokspec,
        out_shape=jax.ShapeDtypeStruct((TOP_K, N), i32),
        compiler_params=_cparams(("parallel",)),
        name="moe_dest",
    )(idx, pos, start)


ZERO_ROWS = 256
DMA_PRIORITIES = 2
DMA_LOOP_UNROLL = 8


def _scatter_body(zero_fill, block_rows, zoff_ref, zflag_ref, dest_ref, h_ref, *rest):
    if zero_fill:
        xs_ref, zeros_ref, sem = rest
    else:
        _, xs_ref, zeros_ref, sem = rest
    i = pl.program_id(0)
    tt = dest_ref.shape[1]

    def packed(ref, row, n):
        return ref.at[pl.ds(pl.multiple_of(row * PACK_ROWS, PACK_ROWS), n * PACK_ROWS)]

    def zero_copy(e, piece):
        return pltpu.make_async_copy(
            zeros_ref, packed(xs_ref, zoff_ref[e] + piece * ZERO_ROWS, ZERO_ROWS), sem.at[0])

    if zero_fill:
        @pl.when(i == 0)
        def _():
            zeros_ref[...] = jnp.zeros_like(zeros_ref)
            n_e = zoff_ref.shape[0]
            for phase in ("start", "wait"):
                def per_expert(e, carry, phase=phase):
                    @pl.when(zflag_ref[e] > 0)
                    def _():
                        for piece in range(block_rows // ZERO_ROWS):
                            cp = zero_copy(e, piece)
                            cp.start() if phase == "start" else cp.wait()
                    return carry
                lax.fori_loop(0, n_e, per_expert, 0)

    def row_copy(t, r):
        return pltpu.make_async_copy(packed(h_ref, t, 1), packed(xs_ref, dest_ref[r, t], 1), sem.at[1])

    def start(t, carry):
        for r in range(TOP_K):
            row_copy(t, r).start(priority=r % DMA_PRIORITIES)
        return carry

    def wait(t, carry):
        for r in range(TOP_K):
            row_copy(t, r).wait()
        return carry

    lax.fori_loop(0, tt, start, 0, unroll=DMA_LOOP_UNROLL)
    lax.fori_loop(0, tt, wait, 0, unroll=DMA_LOOP_UNROLL)


def _scatter(h2, dest, zoff, zflag, n_rows, block_rows, xs_prev):
    N = h2.shape[0] // PACK_ROWS
    tt = min(ROUTE_TILE, N)
    zero_fill = xs_prev is None
    any_spec = pl.BlockSpec(memory_space=pl.ANY)
    in_specs = [pl.BlockSpec((TOP_K, tt), lambda i, *_: (0, i), memory_space=pltpu.SMEM),
                pl.BlockSpec((tt * PACK_ROWS, V7X_LANES), lambda i, *_: (i, 0))]
    args = [dest, h2]
    aliases = {}
    if not zero_fill:
        in_specs.append(any_spec)
        args.append(xs_prev)
        aliases = {4: 0}
    return pl.pallas_call(
        functools.partial(_scatter_body, zero_fill, block_rows),
        grid_spec=pltpu.PrefetchScalarGridSpec(
            num_scalar_prefetch=2, grid=(N // tt,), in_specs=in_specs, out_specs=any_spec,
            scratch_shapes=[pltpu.VMEM((ZERO_ROWS * PACK_ROWS, V7X_LANES), u32),
                            pltpu.SemaphoreType.DMA((2,))]),
        out_shape=jax.ShapeDtypeStruct((n_rows * PACK_ROWS, V7X_LANES), u32),
        input_output_aliases=aliases,
        compiler_params=_cparams(("arbitrary",)),
        name="moe_scatter",
    )(zoff, zflag, *args)


def _expert_body(nf, n_sub, bexp_ref, nused_ref, nvalid_ref, x_ref, wg_ref, wl_ref, bgl_ref, bll_ref,
                 wd_ref, bd_ref, o_ref, xb_ref, acc_ref):
    b = pl.program_id(0)
    j = pl.program_id(1)
    n_valid = nvalid_ref[b]
    active = n_valid > 0
    tm = xb_ref.shape[0]
    sub = tm // n_sub

    @pl.when(j == 0)
    def _():
        for c in range(PACK_ROWS):
            lo, hi = _unpack_chunk(x_ref, c, tm)
            xb_ref[:, c * PACK_SPAN:c * PACK_SPAN + V7X_LANES] = lo.astype(bf16)
            xb_ref[:, c * PACK_SPAN + V7X_LANES:(c + 1) * PACK_SPAN] = hi.astype(bf16)
        acc_ref[...] = jnp.where(active, jnp.broadcast_to(bd_ref[...], acc_ref.shape), 0.0)

    for q in range(1, n_sub + 1):
        @pl.when((n_valid > (q - 1) * sub) & (n_valid <= q * sub))
        def _():
            rows = slice(0, q * sub)
            xb = xb_ref[rows, :]
            glu = jnp.dot(xb, wg_ref[...].astype(bf16), preferred_element_type=f32) + bgl_ref[...]
            lin = jnp.dot(xb, wl_ref[...].astype(bf16), preferred_element_type=f32) + bll_ref[...]
            glu = jnp.minimum(glu, SWIGLU_LIMIT)
            lin = jnp.clip(lin, -SWIGLU_LIMIT, SWIGLU_LIMIT)
            act = glu * jax.nn.sigmoid(SWIGLU_ALPHA * glu) * (lin + 1.0)
            acc_ref[rows, :] += jnp.dot(act.astype(bf16), wd_ref[...].astype(bf16),
                                        preferred_element_type=f32)

    @pl.when(j == nf - 1)
    def _():
        _pack_rows(acc_ref[...], o_ref)


def _experts(xs, bexp, nused, nvalid, w_gate_up, b_gate_up, w_down, b_down, tm, tf):
    n_rows = xs.shape[0] // PACK_ROWS
    E, D, F2 = w_gate_up.shape
    F = F2 // 2
    assert n_rows % tm == 0 and F % tf == 0
    nf = F // tf
    nb = n_rows // tm

    def blk(b, nused):
        return jnp.minimum(b, nused[0] - 1)

    def ftile(b, j, nused):
        return jnp.where(b < nused[0], j, nf - 1)

    b_gu = b_gate_up.reshape(E, 1, F2)
    b_d = b_down.reshape(E, 1, D)
    return pl.pallas_call(
        functools.partial(_expert_body, nf, EXPERT_SUB_BLOCKS),
        grid_spec=pltpu.PrefetchScalarGridSpec(
            num_scalar_prefetch=3, grid=(nb, nf),
            in_specs=[
                pl.BlockSpec((tm * PACK_ROWS, V7X_LANES), lambda b, j, be, nu, nv: (blk(b, nu), 0)),
                pl.BlockSpec((None, D, tf), lambda b, j, be, nu, nv: (be[blk(b, nu)], 0, ftile(b, j, nu))),
                pl.BlockSpec((None, D, tf),
                             lambda b, j, be, nu, nv: (be[blk(b, nu)], 0, nf + ftile(b, j, nu))),
                pl.BlockSpec((None, 1, tf), lambda b, j, be, nu, nv: (be[blk(b, nu)], 0, ftile(b, j, nu))),
                pl.BlockSpec((None, 1, tf),
                             lambda b, j, be, nu, nv: (be[blk(b, nu)], 0, nf + ftile(b, j, nu))),
                pl.BlockSpec((None, tf, D), lambda b, j, be, nu, nv: (be[blk(b, nu)], ftile(b, j, nu), 0)),
                pl.BlockSpec((None, 1, D), lambda b, j, be, nu, nv: (be[blk(b, nu)], 0, 0)),
            ],
            out_specs=pl.BlockSpec((tm * PACK_ROWS, V7X_LANES), lambda b, j, be, nu, nv: (b, 0)),
            scratch_shapes=[pltpu.VMEM((tm, D), bf16), pltpu.VMEM((tm, D), f32)]),
        out_shape=jax.ShapeDtypeStruct((n_rows * PACK_ROWS, V7X_LANES), u32),
        compiler_params=_cparams(("arbitrary", "arbitrary")),
        name="moe_experts",
    )(bexp, nused, nvalid, xs, w_gate_up, w_gate_up, b_gu, b_gu, w_down, b_d)


COMBINE_TILE = 256


def _combine_body(n, dcur_ref, dnext_ref, x1_ref, gate_ref, gf_ref, ys_ref, y_ref, buf_ref, sem):
    i = pl.program_id(0)
    tt = x1_ref.shape[0]

    def packed(row):
        return pl.ds(pl.multiple_of(row * PACK_ROWS, PACK_ROWS), PACK_ROWS)

    def row_copy(d_ref, slot, t, r):
        return pltpu.make_async_copy(ys_ref.at[packed(d_ref[r, t])], buf_ref.at[slot, r, packed(t)],
                                     sem.at[slot])

    def issue(d_ref, slot):
        def body(t, carry):
            for r in range(TOP_K):
                row_copy(d_ref, slot, t, r).start(priority=r % DMA_PRIORITIES)
            return carry
        lax.fori_loop(0, tt, body, 0, unroll=DMA_LOOP_UNROLL)

    @pl.when(i == 0)
    def _():
        issue(dcur_ref, 0)

    @pl.when(i + 1 < n)
    def _():
        issue(dnext_ref, (i + 1) % 2)

    slot = i % 2

    def wait(t, carry):
        for r in range(TOP_K):
            row_copy(dcur_ref, slot, t, r).wait()
        return carry
    lax.fori_loop(0, tt, wait, 0, unroll=DMA_LOOP_UNROLL)

    gates = gate_ref[...]
    parts = []
    for c in range(PACK_ROWS):
        lo_sum = x1_ref[:, c * PACK_SPAN:c * PACK_SPAN + V7X_LANES]
        hi_sum = x1_ref[:, c * PACK_SPAN + V7X_LANES:(c + 1) * PACK_SPAN]
        for r in range(TOP_K):
            lo, hi = _unpack_chunk(buf_ref.at[slot, r], c, tt)
            lo_sum = lo_sum + gates[:, r:r + 1] * lo
            hi_sum = hi_sum + gates[:, r:r + 1] * hi
        parts += [lo_sum, hi_sum]
    x = jnp.concatenate(parts, axis=1)
    ms = jnp.mean(x * x, axis=-1, keepdims=True)
    y_ref[...] = x * lax.rsqrt(ms + RMS_EPS) * gf_ref[...]


def _combine(x1, gates_tm, dest, ys, g_final):
    N, D = x1.shape
    tt = min(COMBINE_TILE, N)
    n = N // tt
    smem = lambda f: pl.BlockSpec((TOP_K, tt), f, memory_space=pltpu.SMEM)
    return pl.pallas_call(
        functools.partial(_combine_body, n),
        grid=(n,),
        in_specs=[smem(lambda i: (0, i)), smem(lambda i: (0, jnp.minimum(i + 1, n - 1))),
                  pl.BlockSpec((tt, D), lambda i: (i, 0)), pl.BlockSpec((tt, TOP_K), lambda i: (i, 0)),
                  pl.BlockSpec((1, D), lambda i: (0, 0)), pl.BlockSpec(memory_space=pl.ANY)],
        out_specs=pl.BlockSpec((tt, D), lambda i: (i, 0)),
        out_shape=jax.ShapeDtypeStruct((N, D), f32),
        scratch_shapes=[pltpu.VMEM((2, TOP_K, tt * PACK_ROWS, V7X_LANES), u32),
                        pltpu.SemaphoreType.DMA((2,))],
        compiler_params=_cparams(("arbitrary",)),
        name="moe_combine",
    )(dest, dest, x1, gates_tm, g_final.reshape(1, D), ys)


def _moe(h2_groups, logit_groups, x1_groups, ew, g_final, tm, tf):
    w_gate_up, b_gate_up, w_down, b_down = ew
    E = w_gate_up.shape[0]
    routed = []
    count = jnp.zeros((E, V7X_LANES), f32)
    for lg in logit_groups:
        idx, gate, pos, count = _route(lg, count)
        routed.append((idx, gate, pos))
    total = sum(lg.shape[1] for lg in logit_groups) * TOP_K
    n_blocks = -(-(total + E * (tm - 1)) // tm)
    n_rows = n_blocks * tm
    counts = count[:, 0].astype(i32)
    padded = (counts + tm - 1) // tm * tm
    pad_end = jnp.cumsum(padded)
    pad_start = pad_end - padded
    n_used = pad_end[-1] // tm
    blocks = jnp.minimum(jnp.arange(n_blocks, dtype=i32), n_used - 1)
    bexp = jnp.minimum(jnp.sum(blocks[:, None] * tm >= pad_end[None, :], axis=1), E - 1).astype(i32)
    all_blocks = jnp.arange(n_blocks, dtype=i32)
    zoff = jnp.concatenate([jnp.maximum(pad_end - tm, 0).astype(i32), all_blocks * tm])
    zflag = jnp.concatenate([counts > 0, all_blocks >= n_used]).astype(i32)
    xs = None
    dests = []
    for h2, (idx, gate, pos) in zip(h2_groups, routed):
        dest = _dest(idx, pos, pad_start)
        dests.append(dest)
        xs = _scatter(h2, dest, zoff, zflag, n_rows, tm, xs)
    row0 = all_blocks * tm
    nvalid = jnp.clip(pad_start[bexp] + counts[bexp] - row0, 0, tm)
    nvalid = jnp.where(all_blocks < n_used, nvalid, 0).astype(i32)
    ys = _experts(xs, bexp, n_used.reshape(1).astype(i32), nvalid, w_gate_up, b_gate_up, w_down, b_down,
                  tm, tf)
    return [_combine(x1, gate.T, dest, ys, g_final)
            for x1, (idx, gate, pos), dest in zip(x1_groups, routed, dests)]


PROJ_TILE_M = 1024
PROJ_TILE_N = 1920
MERGE_TILE_M = 256
EXPERT_TILE_M = 1024
EXPERT_TILE_F = 256
EXPERT_SUB_BLOCKS = 4


def _mixers(x, state, norm_g, w_main, w_u, rw, sw, mw):
    B, T, D = x.shape
    W = w_u.shape[1]
    shift_w = 3 * W + W_LORA + A_LORA + G_LORA
    if state is None:
        s_rwkv0 = jnp.zeros((B, W // HEAD, HEAD, HEAD), f32)
        s5_re0 = jnp.zeros((B, W // S5_GROUP, S5_STATE), f32)
        s5_im0 = jnp.zeros((B, W // S5_GROUP, S5_STATE), f32)
        shift0 = jnp.zeros((B, shift_w), f32)
    else:
        s_rwkv0, s5_re0, s5_im0, shift0 = state
    per_batch = T % PROJ_TILE_M == 0
    xf = x if per_batch else x.reshape(1, B * T, D)
    tm = PROJ_TILE_M if per_batch else min(PROJ_TILE_M, B * T)
    proj = _norm_proj(xf, norm_g, w_main, tm, PROJ_TILE_N, False, bf16).reshape(B, T, MAIN_COLS)
    if per_batch:
        u_tb = _norm_proj(xf, norm_g, w_u, tm, W, True, f32)
    else:
        u_tb = jnp.swapaxes(_norm_proj(xf, norm_g, w_u, tm, W, False, f32).reshape(B, T, W), 0, 1)
    o_a, s_rwkv = _rwkv_stage(proj, shift0, s_rwkv0, rw)
    o_b, s5_re, s5_im = _s5_stage(u_tb, s5_re0, s5_im0, sw)
    if per_batch:
        x1, h2, logits_t = _merge_stage(x, o_a, o_b, True, proj, mw, MERGE_TILE_M)
    else:
        flat = lambda t: t.reshape(1, B * T, t.shape[-1])
        x1, h2, logits_t = _merge_stage(flat(x), flat(o_a), flat(jnp.swapaxes(o_b, 0, 1)), False,
                                        flat(proj), mw, min(MERGE_TILE_M, B * T))
    last = proj[:, -1, :].astype(f32)
    n_lora = W_LORA + A_LORA + G_LORA
    shift = jnp.concatenate([last[:, COL_R * W:(COL_R + 3) * W],
                             last[:, COL_LORA * LORA_PAD:COL_LORA * LORA_PAD + n_lora]], axis=1)
    return (x1.reshape(B * T, D), h2, logits_t), (s_rwkv, s5_re, s5_im, shift)


def kernel(x_prompt, x_sample, state_rwkv, state_s5_re, state_s5_im, state_shift, norm_mix_g, w_in, mu_shift, w0, w2, a0, a2, g2, k_k, k_a, r_k, lnx_g, lnx_b, lam_re, lam_im, log_dt, b_re, b_im, c_re, c_im, d_skip, w_glu, b_glu, w_up_a, w_up_b, w_out, norm_ffn_g, w_router, b_router, w_gate_up, b_gate_up, w_down, b_down, norm_final_g):
    assert w_in.shape[0] == 1, "single-layer trunk"
    D = x_prompt.shape[-1]
    W = w0.shape[-1]
    n_lora = W_LORA + A_LORA + G_LORA
    shift_w = 3 * W + n_lora
    w = w_in[0]
    ga0 = shift_w + W
    w_main = jnp.concatenate(
        [w[:, ga0:ga0 + 2 * D], w[:, :3 * W], w[:, 3 * W:shift_w],
         jnp.zeros((D, LORA_PAD - n_lora), w.dtype)], axis=1).astype(bf16)
    assert w_main.shape[1] == MAIN_COLS
    w_u = w[:, shift_w:shift_w + W].astype(bf16)
    rw = _rwkv_weights(mu_shift[0], w0[0], w2[0], a0[0], a2[0], g2[0], k_k[0], k_a[0], r_k[0],
                       lnx_g[0], lnx_b[0])
    sw = _s5_weights(lam_re[0], lam_im[0], log_dt[0], b_re[0], b_im[0], c_re[0], c_im[0],
                     d_skip[0], w_glu[0], b_glu[0])
    E = w_router.shape[-1]
    mw = (w_up_a[0].astype(bf16), w_up_b[0].astype(bf16), w_out[0].astype(bf16),
          norm_ffn_g[0].reshape(1, D).astype(f32), w_router[0].T.astype(bf16),
          b_router[0].reshape(E, 1).astype(f32))
    ew = (w_gate_up[0], b_gate_up[0], w_down[0], b_down[0])

    state_s = (state_rwkv[0], state_s5_re[0], state_s5_im[0], state_shift[0])
    tok_p, st_p = _mixers(x_prompt, None, norm_mix_g[0], w_main, w_u, rw, sw, mw)
    tok_s, st_s = _mixers(x_sample, state_s, norm_mix_g[0], w_main, w_u, rw, sw, mw)
    y_p, y_s = _moe([tok_p[1], tok_s[1]], [tok_p[2], tok_s[2]], [tok_p[0], tok_s[0]], ew, norm_final_g,
                    EXPERT_TILE_M, EXPERT_TILE_F)
    lead = lambda t: t[None]
    return (y_p.reshape(x_prompt.shape), y_s.reshape(x_sample.shape),
            lead(st_p[0]), lead(st_p[1]), lead(st_p[2]), lead(st_p[3]),
            lead(st_s[0]), lead(st_s[1]), lead(st_s[2]), lead(st_s[3]))
```
